```python
import numpy as np
import jax
import jax.numpy as jnp
from jax import lax

D_MODEL = 1024
BATCH = 8
SEQ = 2048
DEPTH = 4
DEC_BATCH = 128
DEC_SEQ = 8
PAST_LEN = 2048
PAGE_SIZE = 128

N_A_LAYERS = DEPTH // 2
N_B_LAYERS = DEPTH - N_A_LAYERS
A_HEADS = 8
A_DK = 128
A_DV = D_MODEL // A_HEADS
A_DF = A_HEADS * A_DK
HGRN_CHUNK = 32
B_HEADS = 16
B_DH = D_MODEL // B_HEADS
B_KV_HEADS = 4
B_GROUP = B_HEADS // B_KV_HEADS
CMP_BLOCK = 32
CMP_STRIDE = 16
CMP_RATIO = CMP_BLOCK // CMP_STRIDE
CMP_HID = 2 * B_DH
SLC_BLOCK = 64
N_SELECT = 16
WINDOW = 512
WIN_QBLOCK = 128
SEL_ROWS = 128
FORCE_SCORE = 1e4
N_EXPERTS = 32
TOP_K = 4
D_FF = D_MODEL
SWIGLU_LIMIT = 7.0
SWIGLU_ALPHA = 1.702
MOE_BLOCK = 64
DN_ALPHA = (2 * DEPTH) ** 0.25
DN_BETA = (8 * DEPTH) ** -0.25
LN_EPS = 1e-5
RMS_EPS = 1e-6
NEG_INF = -1e30
F32 = jnp.float32

kernel_name = 'yoco_hgrn2_nsa_moe_step'


def layer_norm(x, g, b):
    xf = x.astype(F32)
    mu = jnp.mean(xf, axis=-1, keepdims=True)
    var = jnp.mean(jnp.square(xf - mu), axis=-1, keepdims=True)
    return ((xf - mu) * lax.rsqrt(var + LN_EPS) * g + b).astype(x.dtype)


def rms_norm(x, g):
    xf = x.astype(F32)
    return xf * lax.rsqrt(jnp.mean(xf * xf, axis=-1, keepdims=True) + RMS_EPS) * g


def masked_softmax(s, mask):
    s = jnp.where(mask, s.astype(F32), NEG_INF)
    m = jnp.max(s, axis=-1, keepdims=True)
    e = jnp.where(mask, jnp.exp(s - m), 0.0)
    return e / jnp.maximum(jnp.sum(e, axis=-1, keepdims=True), 1e-30)


def alibi_slopes():
    h = jnp.arange(1, B_HEADS + 1, dtype=F32)
    return (2.0 ** (-8.0 * h / B_HEADS)).reshape(B_KV_HEADS, B_GROUP)


def gather_pages(pool, page_table):
    n_seq, n_pages = page_table.shape
    rows = pool[page_table]
    return rows.reshape((n_seq, n_pages * PAGE_SIZE) + pool.shape[2:])


def gla_chunk_scan(q, k, v, logf, s0):
    B, T, H, DK = q.shape
    DV = v.shape[-1]
    C = HGRN_CHUNK if T % HGRN_CHUNK == 0 else T
    n = T // C

    def to_chunks(a):
        return a.astype(F32).reshape(B, n, C, H, a.shape[-1]).transpose(1, 0, 3, 2, 4)

    qc, kc, vc, gc = to_chunks(q), to_chunks(k), to_chunks(v), to_chunks(logf)
    causal = jnp.tril(jnp.ones((C, C), dtype=bool))[:, :, None]

    def step(S, inp):
        qi, ki, vi, gi = inp
        b = jnp.cumsum(gi, axis=2)
        o_inter = jnp.einsum('bhtk,bhkv->bhtv', qi * jnp.exp(b), S)
        diff = b[:, :, :, None, :] - b[:, :, None, :, :]
        decay = jnp.where(causal, jnp.exp(jnp.where(causal, diff, 0.0)), 0.0)
        att = jnp.einsum('bhtk,bhsk,bhtsk->bhts', qi, ki, decay)
        o = o_inter + jnp.einsum('bhts,bhsv->bhtv', att, vi)
        b_last = b[:, :, -1:, :]
        S_new = jnp.exp(b_last[:, :, 0, :])[..., None] * S + jnp.einsum(
            'bhsk,bhsv->bhkv', ki * jnp.exp(b_last - b), vi)
        return S_new, o

    S, o = lax.scan(step, s0.astype(F32), (qc, kc, vc, gc))
    o = o.transpose(1, 0, 3, 2, 4).reshape(B, T, H, DV)
    return o, S


def hgrn2_mixer(h, s0, w_in, lb, onorm_g, w_o):
    B, T, _ = h.shape
    proj = h @ w_in
    q, fz, inp, og = jnp.split(proj, [A_DF, 2 * A_DF, 2 * A_DF + D_MODEL], axis=-1)
    q = jax.nn.silu(q)
    lbf = lb.astype(F32)
    pos = lbf > 0.0
    log_lb = jnp.where(pos, jnp.log(jnp.where(pos, lbf, 1.0)), NEG_INF)
    logf = jnp.logaddexp(log_lb, jnp.log1p(-lbf) + jax.nn.log_sigmoid(fz.astype(F32)))
    k = -jnp.expm1(logf)
    o, S = gla_chunk_scan(q.reshape(B, T, A_HEADS, A_DK), k.reshape(B, T, A_HEADS, A_DK),
                          inp.reshape(B, T, A_HEADS, A_DV), logf.reshape(B, T, A_HEADS, A_DK), s0)
    o = rms_norm(o, onorm_g) * jax.nn.silu(og.astype(F32)).reshape(B, T, A_HEADS, A_DV)
    return o.reshape(B, T, D_MODEL).astype(h.dtype) @ w_o, S.astype(h.dtype)


def compress_kv(kv, pos_emb, w1, b1, w2, b2):
    B, L, G, DH = kv.shape
    nch = L // CMP_STRIDE
    nc = nch - CMP_RATIO + 1
    ch = kv[:, :nch * CMP_STRIDE].reshape(B, nch, CMP_STRIDE, G, DH)
    blk = jnp.concatenate([ch[:, r:r + nc] for r in range(CMP_RATIO)], axis=2)
    blk = blk + pos_emb[:, None, :]
    flat = blk.transpose(0, 1, 3, 2, 4).reshape(B, nc, G, CMP_BLOCK * DH)
    return jax.nn.silu(flat @ w1 + b1) @ w2 + b2


def cmp_to_slc_map(nc, ns):
    i = np.arange(nc)[:, None]
    j = np.arange(ns)[None, :]
    lo = np.maximum(i * CMP_STRIDE, j * SLC_BLOCK)
    hi = np.minimum(i * CMP_STRIDE + CMP_BLOCK, (j + 1) * SLC_BLOCK)
    return jnp.asarray(np.maximum(hi - lo, 0) / CMP_STRIDE, dtype=F32)


def build_shared_kv(x, c, past_cmp, past_slc, past_win, kv_ada_w, kv_ada_b, kv_w,
                    cmp_pos, cmp_w1, cmp_b1, cmp_w2, cmp_b2):
    B, T, _ = x.shape
    G, DH = B_KV_HEADS, B_DH
    shift, scale = jnp.split((jax.nn.silu(c) @ kv_ada_w + kv_ada_b)[:, None, :], 2, axis=-1)
    hkv = x * (1.0 + scale) + shift
    kv = (hkv @ kv_w).reshape(B, T, 6, G, DH)
    new_cmp, new_slc, new_win = kv[:, :, 0:2], kv[:, :, 2:4], kv[:, :, 4:6]
    if past_cmp is None:
        cmp_all, slc_all, win_all = new_cmp, new_slc, new_win
        keep = min(WINDOW, T)
    else:
        cmp_all = jnp.concatenate([past_cmp.astype(kv.dtype), new_cmp], axis=1)
        slc_all = jnp.concatenate([past_slc.astype(kv.dtype), new_slc], axis=1)
        win_all = jnp.concatenate([past_win.astype(kv.dtype), new_win], axis=1)
        keep = past_win.shape[1]
    ck = compress_kv(cmp_all[:, :, 0], cmp_pos[0], cmp_w1[0], cmp_b1[0], cmp_w2[0], cmp_b2[0])
    cv = compress_kv(cmp_all[:, :, 1], cmp_pos[1], cmp_w1[1], cmp_b1[1], cmp_w2[1], cmp_b2[1])
    L = slc_all.shape[1]
    ns = -(-L // SLC_BLOCK)

    def to_blocks(a):
        a = jnp.pad(a, ((0, 0), (0, ns * SLC_BLOCK - L), (0, 0), (0, 0)))
        a = a.reshape(B, ns, SLC_BLOCK, G, DH).transpose(0, 3, 1, 2, 4)
        return a.reshape(B * G * ns, SLC_BLOCK, DH)

    ctx = (ck, cv, to_blocks(slc_all[:, :, 0]), to_blocks(slc_all[:, :, 1]), ns,
           win_all[:, :, 0], win_all[:, :, 1])
    return ctx, new_cmp, new_slc, win_all[:, -keep:]


def selected_attn(q, idx, valid, q_pos, kflat, vflat, ns, slopes):
    B, T, G, R, DH = q.shape
    n = idx.shape[-1]
    rows = B * T
    nchunk = -(-rows // SEL_ROWS)
    pad = nchunk * SEL_ROWS - rows

    def rows_of(a):
        a = a.reshape((rows,) + a.shape[2:])
        a = jnp.pad(a, [(0, pad)] + [(0, 0)] * (a.ndim - 1))
        return a.reshape((nchunk, SEL_ROWS) + a.shape[1:])

    bidx = jnp.broadcast_to(jnp.arange(B, dtype=jnp.int32)[:, None], (B, T))
    pos = jnp.broadcast_to(q_pos[None, :], (B, T))
    offs = jnp.arange(SLC_BLOCK)

    def chunk(args):
        qc, ic, vc, pc, bc = args
        flat = (bc[:, None, None] * G + jnp.arange(G)[None, :, None]) * ns + ic
        kg = kflat[flat].astype(F32)
        vg = vflat[flat].astype(F32)
        kpos = ic[..., None] * SLC_BLOCK + offs
        dist = (pc[:, None, None, None] - kpos).astype(F32)
        mask = (vc[..., None] & (dist >= 0)).reshape(SEL_ROWS, G, 1, n * SLC_BLOCK)
        s = jnp.einsum('cgrd,cgnkd->cgrnk', qc, kg) - slopes[None, :, :, None, None] * dist[:, :, None]
        p = masked_softmax(s.reshape(SEL_ROWS, G, R, n * SLC_BLOCK), mask)
        return jnp.einsum('cgrnk,cgnkd->cgrd', p.reshape(SEL_ROWS, G, R, n, SLC_BLOCK), vg)

    out = lax.map(chunk, (rows_of(q), rows_of(idx), rows_of(valid), rows_of(pos), rows_of(bidx)))
    return out.reshape(nchunk * SEL_ROWS, G, R, DH)[:rows].reshape(B, T, G, R, DH)


def window_attn(q, wk, wv, slopes):
    B, Tq = q.shape[0], q.shape[1]
    Tk = wk.shape[1]
    pad_left = WINDOW - (Tk - Tq)
    kp = jnp.pad(wk, ((0, 0), (pad_left, 0), (0, 0), (0, 0)))
    vp = jnp.pad(wv, ((0, 0), (pad_left, 0), (0, 0), (0, 0)))
    real = jnp.arange(WINDOW + Tq) >= pad_left
    qb = WIN_QBLOCK if Tq % WIN_QBLOCK == 0 else Tq
    nb = Tq // qb

    def block(j):
        start = j * qb
        qj = lax.dynamic_slice_in_dim(q, start, qb, axis=1)
        kj = lax.dynamic_slice_in_dim(kp, start, WINDOW + qb, axis=1).astype(F32)
        vj = lax.dynamic_slice_in_dim(vp, start, WINDOW + qb, axis=1).astype(F32)
        rj = lax.dynamic_slice_in_dim(real, start, WINDOW + qb)
        dist = (jnp.arange(qb)[:, None] + WINDOW - jnp.arange(WINDOW + qb)[None, :])
        mask = rj[None, :] & (dist >= 0) & (dist < WINDOW)
        s = jnp.einsum('bqgrd,bkgd->bgrqk', qj, kj) - slopes[None, :, :, None, None] * dist.astype(F32)
        p = masked_softmax(s, mask)
        return jnp.einsum('bgrqk,bkgd->bqgrd', p, vj)

    out = lax.map(block, jnp.arange(nb))
    return out.transpose(1, 0, 2, 3, 4, 5).reshape(B, Tq, B_KV_HEADS, B_GROUP, B_DH)


def nsa_mixer(h, ctx, pos0, w_in, w_o):
    ck, cv, sk, sv, ns, wk, wv = ctx
    B, T, _ = h.shape
    G, R, DH = B_KV_HEADS, B_GROUP, B_DH
    proj = h @ w_in
    q = proj[..., :D_MODEL].astype(F32).reshape(B, T, G, R, DH) * (DH ** -0.5)
    gates = jax.nn.sigmoid(proj[..., D_MODEL:].astype(F32)).reshape(B, T, G, R, 3)
    slopes = alibi_slopes()
    q_pos = pos0 + jnp.arange(T)
    nc = ck.shape[1]
    c_end = jnp.arange(nc) * CMP_STRIDE + CMP_BLOCK - 1
    dist = (q_pos[:, None] - c_end[None, :]).astype(F32)
    s = jnp.einsum('btgrd,bcgd->bgrtc', q, ck.astype(F32)) - slopes[None, :, :, None, None] * dist
    p_cmp = masked_softmax(s, dist >= 0)
    o_cmp = jnp.einsum('bgrtc,bcgd->btgrd', p_cmp, cv.astype(F32))
    imp = jnp.einsum('bgrtc,cj->btgj', p_cmp, cmp_to_slc_map(nc, ns))
    blk = jnp.arange(ns)[None, :]
    cur = (q_pos // SLC_BLOCK)[:, None]
    causal_blk = blk * SLC_BLOCK <= q_pos[:, None]
    forced = (blk == 0) | (blk == cur) | (blk == cur - 1)
    score = jnp.where(forced[None, :, None, :], FORCE_SCORE,
                      jnp.where(causal_blk[None, :, None, :], imp, -1.0))
    top_v, top_i = lax.top_k(score, min(N_SELECT, ns))
    o_slc = selected_attn(q, top_i, top_v >= 0.0, q_pos, sk, sv, ns, slopes)
    o_win = window_attn(q, wk, wv, slopes)
    o = gates[..., 0:1] * o_cmp + gates[..., 1:2] * o_slc + gates[..., 2:3] * o_win
    return o.reshape(B, T, D_MODEL).astype(h.dtype) @ w_o


def moe_ffn(h, w_r, b_r, w_up, b_up, w_down, b_down):
    shp = h.shape
    x = h.reshape(-1, D_MODEL)
    N = x.shape[0]
    logits = (x @ w_r + b_r).astype(F32)
    top_v, top_e = lax.top_k(logits, TOP_K)
    gate = jax.nn.softmax(top_v, axis=-1)
    M = N * TOP_K
    flat_e = top_e.reshape(M)
    flat_tok = (jnp.arange(M) // TOP_K).astype(jnp.int32)
    order = jnp.argsort(flat_e)
    se = flat_e[order]
    tok_sorted = flat_tok[order]
    counts = jnp.bincount(flat_e, length=N_EXPERTS)
    padded = (counts + MOE_BLOCK - 1) // MOE_BLOCK * MOE_BLOCK
    pend = jnp.cumsum(padded)
    pstart = pend - padded
    gstart = jnp.cumsum(counts) - counts
    dest = pstart[se] + jnp.arange(M) - gstart[se]
    n_blk = -(-(M + N_EXPERTS * (MOE_BLOCK - 1)) // MOE_BLOCK)
    m_pad = n_blk * MOE_BLOCK
    row_tok = jnp.full((m_pad,), N, dtype=jnp.int32).at[dest].set(tok_sorted)
    blk_exp = jnp.minimum(jnp.searchsorted(pend, jnp.arange(n_blk) * MOE_BLOCK, side='right'), N_EXPERTS - 1)
    x_pad = jnp.concatenate([x, jnp.zeros((1, D_MODEL), x.dtype)], axis=0)
    xr = x_pad[row_tok].reshape(n_blk, MOE_BLOCK, D_MODEL)

    def expert_block(args):
        xb, e = args
        up = xb @ w_up[e] + b_up[e]
        glu = jnp.minimum(up[:, 0::2], SWIGLU_LIMIT)
        lin = jnp.clip(up[:, 1::2], -SWIGLU_LIMIT, SWIGLU_LIMIT)
        act = glu * jax.nn.sigmoid(SWIGLU_ALPHA * glu) * (lin + 1.0)
        return act @ w_down[e] + b_down[e]

    yr = lax.map(expert_block, (xr, blk_exp)).reshape(m_pad, D_MODEL)
    y_assign = yr[dest] * gate.reshape(M)[order][:, None].astype(yr.dtype)
    y = jax.ops.segment_sum(y_assign, tok_sorted, num_segments=N)
    return y.reshape(shp)


def run_trunk(x, c, pos0, hgrn_s0, past_cmp, past_slc, past_win,
              ada_w, ada_b, ln_g, ln_b, a_w_in, a_lb, a_onorm_g, a_w_o,
              kv_ada_w, kv_ada_b, kv_w, cmp_pos, cmp_w1, cmp_b1, cmp_w2, cmp_b2,
              b_w_in, b_w_o, moe_wr, moe_br, moe_w_up, moe_b_up, moe_w_down, moe_b_down):
    B = x.shape[0]
    sc = jax.nn.silu(c)
    lb_p = jax.nn.softmax(a_lb.astype(F32), axis=0)
    lb_all = jnp.maximum(jnp.cumsum(lb_p, axis=0) - lb_p[0], 0.0)
    hgrn_states = []
    ctx = None
    new_cmp = new_slc = new_win = None
    for l in range(DEPTH):
        mod = (sc @ ada_w[l] + ada_b[l])[:, None, :]
        sh_m, sc_m, g_m, sh_f, sc_f, g_f = jnp.split(mod, 6, axis=-1)
        h = x * (1.0 + sc_m) + sh_m
        if l < N_A_LAYERS:
            if hgrn_s0 is None:
                s0 = jnp.zeros((B, A_HEADS, A_DK, A_DV), F32)
            else:
                s0 = hgrn_s0[l]
            y, S = hgrn2_mixer(h, s0, a_w_in[l], lb_all[l], a_onorm_g[l], a_w_o[l])
            hgrn_states.append(S)
        else:
            if ctx is None:
                ctx, new_cmp, new_slc, new_win = build_shared_kv(
                    x, c, past_cmp, past_slc, past_win, kv_ada_w, kv_ada_b, kv_w,
                    cmp_pos, cmp_w1, cmp_b1, cmp_w2, cmp_b2)
            j = l - N_A_LAYERS
            y = nsa_mixer(h, ctx, pos0, b_w_in[j], b_w_o[j])
        x = layer_norm(DN_ALPHA * x + (1.0 + g_m) * y, ln_g[l, 0], ln_b[l, 0])
        h = x * (1.0 + sc_f) + sh_f
        y = moe_ffn(h, moe_wr[l], moe_br[l], moe_w_up[l], moe_b_up[l], moe_w_down[l], moe_b_down[l])
        x = layer_norm(DN_ALPHA * x + (1.0 + g_f) * y, ln_g[l, 1], ln_b[l, 1])
    return x, jnp.stack(hgrn_states, axis=0), new_cmp, new_slc, new_win


def setup_inputs(seed: int = 0) -> dict:
    key = jax.random.key(seed)
    keys = iter(jax.random.split(key, 40))

    def nrm(shape, scale):
        return jax.random.normal(next(keys), shape, F32) * scale

    n_pages = PAST_LEN // PAGE_SIZE
    n_used = DEC_BATCH * n_pages
    n_pool = n_used + n_used // 4
    win_keep = min(WINDOW, PAST_LEN)
    G, DH = B_KV_HEADS, B_DH
    d_is = D_MODEL ** -0.5
    a_cols = jnp.concatenate([jnp.ones((2 * A_DF,), F32), jnp.full((D_MODEL,), DN_BETA, F32),
                              jnp.ones((D_MODEL,), F32)])
    kv_pair = jnp.concatenate([jnp.ones((G * DH,), F32), jnp.full((G * DH,), DN_BETA, F32)])
    kv_cols = jnp.concatenate([kv_pair, kv_pair, kv_pair])
    return {
        'x_prompt': nrm((BATCH, SEQ, D_MODEL), 1.0),
        'x_sample': nrm((DEC_BATCH, DEC_SEQ, D_MODEL), 1.0),
        'state_hgrn': nrm((N_A_LAYERS, DEC_BATCH, A_HEADS, A_DK, A_DV), 0.3),
        'cache_cmp_kv': nrm((n_pool, PAGE_SIZE, 2, G, DH), 1.0),
        'cache_slc_kv': nrm((n_pool, PAGE_SIZE, 2, G, DH), 1.0),
        'state_win_kv': nrm((DEC_BATCH, win_keep, 2, G, DH), 1.0),
        'page_table': jax.random.permutation(next(keys), n_pool)[:n_used].reshape(DEC_BATCH, n_pages).astype(jnp.int32),
        'c_prompt': nrm((BATCH, D_MODEL), 1.0),
        'c_sample': nrm((DEC_BATCH, D_MODEL), 1.0),
        'ada_w': nrm((DEPTH, D_MODEL, 6 * D_MODEL), 0.1 * d_is),
        'ada_b': nrm((DEPTH, 6 * D_MODEL), 0.01),
        'ln_g': 1.0 + nrm((DEPTH, 2, D_MODEL), 0.02),
        'ln_b': nrm((DEPTH, 2, D_MODEL), 0.02),
        'a_w_in': nrm((N_A_LAYERS, D_MODEL, 2 * A_DF + 2 * D_MODEL), d_is) * a_cols,
        'a_lb': nrm((N_A_LAYERS, A_DF), 0.5),
        'a_onorm_g': 1.0 + nrm((N_A_LAYERS, A_DV), 0.02),
        'a_w_o': nrm((N_A_LAYERS, D_MODEL, D_MODEL), d_is * DN_BETA),
        'kv_ada_w': nrm((D_MODEL, 2 * D_MODEL), 0.1 * d_is),
        'kv_ada_b': nrm((2 * D_MODEL,), 0.01),
        'kv_w': nrm((D_MODEL, 6 * G * DH), d_is) * kv_cols,
        'cmp_pos': nrm((2, CMP_BLOCK, DH), 0.02),
        'cmp_w1': nrm((2, CMP_BLOCK * DH, CMP_HID), (CMP_BLOCK * DH) ** -0.5),
        'cmp_b1': nrm((2, CMP_HID), 0.01),
        'cmp_w2': nrm((2, CMP_HID, DH), CMP_HID ** -0.5),
        'cmp_b2': nrm((2, DH), 0.01),
        'b_w_in': nrm((N_B_LAYERS, D_MODEL, D_MODEL + 3 * B_HEADS), d_is),
        'b_w_o': nrm((N_B_LAYERS, D_MODEL, D_MODEL), d_is * DN_BETA),
        'moe_wr': nrm((DEPTH, D_MODEL, N_EXPERTS), d_is),
        'moe_br': nrm((DEPTH, N_EXPERTS), 0.01),
        'moe_w_up': nrm((DEPTH, N_EXPERTS, D_MODEL, 2 * D_FF), d_is),
        'moe_b_up': nrm((DEPTH, N_EXPERTS, 2 * D_FF), 0.01),
        'moe_w_down': nrm((DEPTH, N_EXPERTS, D_FF, D_MODEL), (D_FF ** -0.5) * DN_BETA),
        'moe_b_down': nrm((DEPTH, N_EXPERTS, D_MODEL), 0.01),
    }


def reference(x_prompt, x_sample, state_hgrn, cache_cmp_kv, cache_slc_kv, state_win_kv, page_table,
              c_prompt, c_sample, ada_w, ada_b, ln_g, ln_b, a_w_in, a_lb, a_onorm_g, a_w_o,
              kv_ada_w, kv_ada_b, kv_w, cmp_pos, cmp_w1, cmp_b1, cmp_w2, cmp_b2,
              b_w_in, b_w_o, moe_wr, moe_br, moe_w_up, moe_b_up, moe_w_down, moe_b_down):
    past_cmp = gather_pages(cache_cmp_kv, page_table)
    past_slc = gather_pages(cache_slc_kv, page_table)
    y_prompt, hgrn_prompt, cmp_prompt, slc_prompt, win_prompt = run_trunk(
        x_prompt, c_prompt, 0, None, None, None, None,
        ada_w, ada_b, ln_g, ln_b, a_w_in, a_lb, a_onorm_g, a_w_o,
        kv_ada_w, kv_ada_b, kv_w, cmp_pos, cmp_w1, cmp_b1, cmp_w2, cmp_b2,
        b_w_in, b_w_o, moe_wr, moe_br, moe_w_up, moe_b_up, moe_w_down, moe_b_down)
    y_sample, hgrn_sample, cmp_sample, slc_sample, win_sample = run_trunk(
        x_sample, c_sample, PAST_LEN, state_hgrn, past_cmp, past_slc, state_win_kv,
        ada_w, ada_b, ln_g, ln_b, a_w_in, a_lb, a_onorm_g, a_w_o,
        kv_ada_w, kv_ada_b, kv_w, cmp_pos, cmp_w1, cmp_b1, cmp_w2, cmp_b2,
        b_w_in, b_w_o, moe_wr, moe_br, moe_w_up, moe_b_up, moe_w_down, moe_b_down)
    return (y_prompt, y_sample, hgrn_prompt, hgrn_sample, cmp_prompt, cmp_sample,
            slc_prompt, slc_sample, win_prompt, win_sample)
```

```python
import functools

import numpy as np
import jax
import jax.numpy as jnp
from jax import lax
from jax.experimental import pallas as pl
from jax.experimental.pallas import tpu as pltpu

F32 = jnp.float32
BF16 = jnp.bfloat16
HIGHEST = lax.Precision.HIGHEST

D_MODEL = 1024
DEPTH = 4
PAGE_SIZE = 128
N_A_LAYERS = DEPTH // 2
A_HEADS = 8
A_DK = 128
A_DV = D_MODEL // A_HEADS
A_DF = A_HEADS * A_DK
HGRN_CHUNK = 32
B_HEADS = 16
B_DH = D_MODEL // B_HEADS
B_KV_HEADS = 4
B_GROUP = B_HEADS // B_KV_HEADS
CMP_BLOCK = 32
CMP_STRIDE = 16
CMP_RATIO = CMP_BLOCK // CMP_STRIDE
CMP_HID = 2 * B_DH
SLC_BLOCK = 64
N_SELECT = 16
WINDOW = 512
FORCE_SCORE = 1e4
N_EXPERTS = 32
TOP_K = 4
D_FF = D_MODEL
SWIGLU_LIMIT = 7.0
SWIGLU_ALPHA = 1.702
DN_ALPHA = (2 * DEPTH) ** 0.25
LN_EPS = 1e-5
RMS_EPS = 1e-6
NEG_INF = -1e30

LANES = 128
SUBLANES = 8
ROW_TILE = 256
MOE_ROWS = 256
KEY_TILE = 256
VMEM_LIMIT = 56 * 1024 * 1024


def _cparams(*sem):
    return pltpu.CompilerParams(dimension_semantics=sem, vmem_limit_bytes=VMEM_LIMIT)


def _row_tiles(B, T):
    if T >= ROW_TILE:
        assert T % ROW_TILE == 0
        return 1, ROW_TILE
    sb = max(1, ROW_TILE // T)
    while B % sb:
        sb //= 2
    return sb, T


def _sigmoid(x):
    return 1.0 / (1.0 + jnp.exp(-x))


def _linear_kernel(x_ref, w_ref, b_ref, o_ref):
    x = x_ref[...]
    x = x * _sigmoid(x)
    o_ref[...] = jnp.dot(x.astype(BF16), w_ref[...].astype(BF16), preferred_element_type=F32) + b_ref[...]


def _silu_linear(x, w, b):
    L, K, N = w.shape
    M = x.shape[0]
    tn = 1024
    return pl.pallas_call(
        _linear_kernel,
        grid=(L, N // tn),
        in_specs=[pl.BlockSpec((M, K), lambda l, j: (0, 0)),
                  pl.BlockSpec((None, K, tn), lambda l, j: (l, 0, j)),
                  pl.BlockSpec((None, 1, tn), lambda l, j: (l, 0, j))],
        out_specs=pl.BlockSpec((None, M, tn), lambda l, j: (l, 0, j)),
        out_shape=jax.ShapeDtypeStruct((L, M, N), F32),
        compiler_params=_cparams("arbitrary", "arbitrary"),
        name="silu_linear",
    )(x, w, b.reshape(L, 1, N))


def _modulate(x_ref, sc_ref, sh_ref):
    sb, tt, d = x_ref.shape
    h = x_ref[...] * (1.0 + sc_ref[...]) + sh_ref[...]
    return h.reshape(sb * tt, d)


def _hgrn_in_kernel(x_ref, sh_ref, sc_ref, w_ref, lb_ref, q_ref, k_ref, g_ref, v_ref, og_ref):
    sb, tt, d = x_ref.shape
    h = _modulate(x_ref, sc_ref, sh_ref).astype(BF16)
    proj = jnp.dot(h, w_ref[...], preferred_element_type=F32)
    qz = proj[:, :A_DF]
    fz = proj[:, A_DF:2 * A_DF]
    lb = lb_ref[...]
    log_sig = jnp.minimum(fz, 0.0) - jnp.log1p(jnp.exp(-jnp.abs(fz)))
    pos = lb > 0.0
    log_lb = jnp.where(pos, jnp.log(jnp.where(pos, lb, 1.0)), NEG_INF)
    other = jnp.log1p(-lb) + log_sig
    logf = jnp.maximum(log_lb, other) + jnp.log1p(jnp.exp(-jnp.abs(log_lb - other)))
    q_ref[...] = (qz * _sigmoid(qz)).reshape(sb, tt, A_DF)
    g_ref[...] = logf.reshape(sb, tt, A_DF)
    k_ref[...] = ((1.0 - lb) * _sigmoid(-fz)).reshape(sb, tt, A_DF)
    v_ref[...] = proj[:, 2 * A_DF:2 * A_DF + D_MODEL].reshape(sb, tt, D_MODEL)
    og_ref[...] = proj[:, 2 * A_DF + D_MODEL:].reshape(sb, tt, D_MODEL)


def _mod_spec(sb, comp):
    return pl.BlockSpec((sb, 1, D_MODEL), lambda i, j, c=comp: (i, 0, c))


def _tok_spec(sb, tt, width=D_MODEL):
    return pl.BlockSpec((sb, tt, width), lambda i, j: (i, j, 0))


def _hgrn_in(x, mod, w_bf, lb):
    B, T, _ = x.shape
    sb, tt = _row_tiles(B, T)
    n_out = w_bf.shape[1]
    shp = jax.ShapeDtypeStruct((B, T, D_MODEL), F32)
    return pl.pallas_call(
        _hgrn_in_kernel,
        grid=(B // sb, T // tt),
        in_specs=[_tok_spec(sb, tt), _mod_spec(sb, 0), _mod_spec(sb, 1),
                  pl.BlockSpec((D_MODEL, n_out), lambda i, j: (0, 0)),
                  pl.BlockSpec((1, A_DF), lambda i, j: (0, 0))],
        out_specs=[_tok_spec(sb, tt)] * 5,
        out_shape=[shp] * 5,
        compiler_params=_cparams("arbitrary", "arbitrary"),
        name="hgrn_in",
    )(x, mod, mod, w_bf, lb.reshape(1, A_DF))


def _nsa_in_kernel(x_ref, sh_ref, sc_ref, wq_ref, wg_ref, q_ref, gate_ref):
    sb, tt, d = x_ref.shape
    h = _modulate(x_ref, sc_ref, sh_ref).astype(BF16)
    q = jnp.dot(h, wq_ref[...], preferred_element_type=F32) * (B_DH ** -0.5)
    gz = jnp.dot(h, wg_ref[...], preferred_element_type=F32)
    q_ref[...] = q.reshape(sb, tt, D_MODEL)
    gate_ref[...] = _sigmoid(gz).reshape(sb, tt, LANES)


def _nsa_in(x, mod, wq_bf, wg_bf):
    B, T, _ = x.shape
    sb, tt = _row_tiles(B, T)
    return pl.pallas_call(
        _nsa_in_kernel,
        grid=(B // sb, T // tt),
        in_specs=[_tok_spec(sb, tt), _mod_spec(sb, 0), _mod_spec(sb, 1),
                  pl.BlockSpec((D_MODEL, D_MODEL), lambda i, j: (0, 0)),
                  pl.BlockSpec((D_MODEL, LANES), lambda i, j: (0, 0))],
        out_specs=[_tok_spec(sb, tt), _tok_spec(sb, tt, LANES)],
        out_shape=[jax.ShapeDtypeStruct((B, T, D_MODEL), F32), jax.ShapeDtypeStruct((B, T, LANES), F32)],
        compiler_params=_cparams("arbitrary", "arbitrary"),
        name="nsa_in",
    )(x, mod, mod, wq_bf, wg_bf)


def _kv_proj_kernel(x_ref, sh_ref, sc_ref, w_ref, kv_ref):
    sb, tt, d = x_ref.shape
    h = _modulate(x_ref, sc_ref, sh_ref).astype(BF16)
    kv = jnp.dot(h, w_ref[...], preferred_element_type=F32)
    kv_ref[...] = kv.reshape(sb, tt, kv.shape[-1])


def _kv_proj(x, kvmod, w_bf):
    B, T, _ = x.shape
    sb, tt = _row_tiles(B, T)
    n_out = w_bf.shape[1]
    return pl.pallas_call(
        _kv_proj_kernel,
        grid=(B // sb, T // tt),
        in_specs=[_tok_spec(sb, tt), _mod_spec(sb, 0), _mod_spec(sb, 1),
                  pl.BlockSpec((D_MODEL, n_out), lambda i, j: (0, 0))],
        out_specs=_tok_spec(sb, tt, n_out),
        out_shape=jax.ShapeDtypeStruct((B, T, n_out), F32),
        compiler_params=_cparams("arbitrary", "arbitrary"),
        name="kv_proj",
    )(x, kvmod, kvmod, w_bf)


def _layer_norm_rows(z, g, b):
    mu = jnp.mean(z, axis=-1, keepdims=True)
    zc = z - mu
    var = jnp.mean(zc * zc, axis=-1, keepdims=True)
    return zc * lax.rsqrt(var + LN_EPS) * g + b


def _mixer_out_kernel(o_ref, w_ref, x_ref, gm_ref, shf_ref, scf_ref, lng_ref, lnb_ref, wr_ref, br_ref,
                      xn_ref, hf_ref, te_ref, tg_ref):
    sb, tt, d = x_ref.shape
    rows = sb * tt
    o = o_ref[...].reshape(rows, d).astype(BF16)
    y = jnp.dot(o, w_ref[...], preferred_element_type=F32).reshape(sb, tt, d)
    z = DN_ALPHA * x_ref[...] + (1.0 + gm_ref[...]) * y
    xn = _layer_norm_rows(z, lng_ref[...], lnb_ref[...])
    xn_ref[...] = xn
    hf = xn * (1.0 + scf_ref[...]) + shf_ref[...]
    hf_ref[...] = hf
    logits = jnp.dot(hf.reshape(rows, d), wr_ref[...], precision=HIGHEST,
                     preferred_element_type=F32) + br_ref[...]
    lane = lax.broadcasted_iota(jnp.int32, (rows, LANES), 1)
    te = jnp.zeros((rows, LANES), jnp.int32)
    tv = jnp.zeros((rows, LANES), F32)
    v0 = None
    denom = None
    for k in range(TOP_K):
        m = jnp.max(logits, axis=-1, keepdims=True)
        idx = jnp.min(jnp.where(logits == m, lane, LANES), axis=-1, keepdims=True)
        if k == 0:
            v0 = m
            e = jnp.ones_like(m)
            denom = e
        else:
            e = jnp.exp(m - v0)
            denom = denom + e
        te = jnp.where(lane == k, idx, te)
        tv = jnp.where(lane == k, e, tv)
        logits = jnp.where(lane == idx, -jnp.inf, logits)
    te_ref[...] = te.reshape(sb, tt, LANES)
    tg_ref[...] = (tv / denom).reshape(sb, tt, LANES)


def _mixer_out(o, w_bf, x, mod, ln_g, ln_b, wr_pad, br_pad):
    B, T, _ = x.shape
    sb, tt = _row_tiles(B, T)
    vec = pl.BlockSpec((1, 1, D_MODEL), lambda i, j: (0, 0, 0))
    return pl.pallas_call(
        _mixer_out_kernel,
        grid=(B // sb, T // tt),
        in_specs=[_tok_spec(sb, tt), pl.BlockSpec((D_MODEL, D_MODEL), lambda i, j: (0, 0)), _tok_spec(sb, tt),
                  _mod_spec(sb, 2), _mod_spec(sb, 3), _mod_spec(sb, 4), vec, vec,
                  pl.BlockSpec((D_MODEL, LANES), lambda i, j: (0, 0)),
                  pl.BlockSpec((1, LANES), lambda i, j: (0, 0))],
        out_specs=[_tok_spec(sb, tt), _tok_spec(sb, tt), _tok_spec(sb, tt, LANES), _tok_spec(sb, tt, LANES)],
        out_shape=[jax.ShapeDtypeStruct((B, T, D_MODEL), F32), jax.ShapeDtypeStruct((B, T, D_MODEL), F32),
                   jax.ShapeDtypeStruct((B, T, LANES), jnp.int32), jax.ShapeDtypeStruct((B, T, LANES), F32)],
        compiler_params=_cparams("arbitrary", "arbitrary"),
        name="mixer_out",
    )(o, w_bf, x, mod, mod, mod, ln_g.reshape(1, 1, D_MODEL), ln_b.reshape(1, 1, D_MODEL), wr_pad, br_pad)


def _ffn_out_kernel(x_ref, y_ref, gf_ref, lng_ref, lnb_ref, xn_ref):
    z = DN_ALPHA * x_ref[...] + (1.0 + gf_ref[...]) * y_ref[...]
    xn_ref[...] = _layer_norm_rows(z, lng_ref[...], lnb_ref[...])


def _ffn_out(x, y, mod, ln_g, ln_b):
    B, T, _ = x.shape
    sb, tt = _row_tiles(B, T)
    vec = pl.BlockSpec((1, 1, D_MODEL), lambda i, j: (0, 0, 0))
    return pl.pallas_call(
        _ffn_out_kernel,
        grid=(B // sb, T // tt),
        in_specs=[_tok_spec(sb, tt), _tok_spec(sb, tt), _mod_spec(sb, 5), vec, vec],
        out_specs=_tok_spec(sb, tt),
        out_shape=jax.ShapeDtypeStruct((B, T, D_MODEL), F32),
        compiler_params=_cparams("arbitrary", "arbitrary"),
        name="ffn_out",
    )(x, y, mod, ln_g.reshape(1, 1, D_MODEL), ln_b.reshape(1, 1, D_MODEL))


def _hgrn_scan_kernel(q_ref, k_ref, g_ref, v_ref, og_ref, s0_ref, gn_ref, o_ref, sout_ref,
                      st_ref, b_ref, k_scr, v_scr, *, chunk, n_chunks):
    tb = pl.program_id(1)
    C = chunk
    nsub = C // SUBLANES

    @pl.when(tb == 0)
    def _():
        for h in range(A_HEADS):
            st_ref[h] = s0_ref[0, h].T

    row = lax.broadcasted_iota(jnp.int32, (C, C), 0)
    col = lax.broadcasted_iota(jnp.int32, (C, C), 1)
    tri = (row >= col).astype(F32)
    sub_row = lax.broadcasted_iota(jnp.int32, (SUBLANES, A_DK), 0)
    gn = gn_ref[...]

    def chunk_body(c, carry):
        r0 = pl.multiple_of(c * C, C)
        for h in range(A_HEADS):
            ls = slice(h * A_DK, (h + 1) * A_DK)
            g = g_ref[0, pl.ds(r0, C), ls]
            b = jnp.dot(tri, g, precision=HIGHEST, preferred_element_type=F32)
            b_ref[...] = b
            q = q_ref[0, pl.ds(r0, C), ls]
            k = k_ref[0, pl.ds(r0, C), ls]
            v = v_ref[0, pl.ds(r0, C), ls]
            k_scr[...] = k
            v_scr[...] = v
            st = st_ref[h]
            o_inter = lax.dot_general((q * jnp.exp(b)).astype(BF16), st.astype(BF16),
                                      (((1,), (1,)), ((), ())), preferred_element_type=F32)
            acc = [jnp.zeros((SUBLANES, A_DV), F32) for _ in range(nsub)]
            for s in range(C):
                j = s // SUBLANES
                b_s = b_ref[pl.ds(s, 1), :]
                k_s = k_scr[pl.ds(s, 1), :]
                v_s = v_scr[pl.ds(s, 1), :]
                for i in range(j, nsub):
                    rs = slice(i * SUBLANES, (i + 1) * SUBLANES)
                    dlt = b[rs] - b_s
                    if i == j:
                        causal = sub_row >= (s % SUBLANES)
                        e = jnp.where(causal, jnp.exp(jnp.where(causal, dlt, 0.0)), 0.0)
                    else:
                        e = jnp.exp(dlt)
                    w = jnp.sum(q[rs] * e * k_s, axis=-1, keepdims=True)
                    acc[i] = acc[i] + w * v_s
            o = o_inter + (jnp.concatenate(acc, axis=0) if nsub > 1 else acc[0])
            b_last = b_ref[pl.ds(C - 1, 1), :]
            kd = k * jnp.exp(b_last - b)
            st_new = st * jnp.exp(b_last) + lax.dot_general(
                v.astype(BF16), kd.astype(BF16), (((0,), (0,)), ((), ())), preferred_element_type=F32)
            st_ref[h] = st_new
            og = og_ref[0, pl.ds(r0, C), ls]
            on = o * lax.rsqrt(jnp.mean(o * o, axis=-1, keepdims=True) + RMS_EPS) * gn
            o_ref[0, pl.ds(r0, C), ls] = on * (og * _sigmoid(og))
        return carry

    lax.fori_loop(0, n_chunks, chunk_body, 0)

    @pl.when(tb == pl.num_programs(1) - 1)
    def _():
        for h in range(A_HEADS):
            sout_ref[0, h] = st_ref[h].T


def _hgrn_scan(q, k, g, v, og, s0, gn):
    B, T, _ = q.shape
    C = HGRN_CHUNK if T % HGRN_CHUNK == 0 else T
    assert C % SUBLANES == 0
    tblk = min(T, ROW_TILE)
    assert T % tblk == 0 and tblk % C == 0
    tok = pl.BlockSpec((1, tblk, D_MODEL), lambda b, t: (b, t, 0))
    st = pl.BlockSpec((1, A_HEADS, A_DK, A_DV), lambda b, t: (b, 0, 0, 0))
    return pl.pallas_call(
        functools.partial(_hgrn_scan_kernel, chunk=C, n_chunks=tblk // C),
        grid=(B, T // tblk),
        in_specs=[tok, tok, tok, tok, tok, st, pl.BlockSpec((1, A_DV), lambda b, t: (0, 0))],
        out_specs=[tok, st],
        out_shape=[jax.ShapeDtypeStruct((B, T, D_MODEL), F32),
                   jax.ShapeDtypeStruct((B, A_HEADS, A_DK, A_DV), F32)],
        scratch_shapes=[pltpu.VMEM((A_HEADS, A_DV, A_DK), F32), pltpu.VMEM((C, A_DK), F32),
                        pltpu.VMEM((C, A_DK), F32), pltpu.VMEM((C, A_DV), F32)],
        compiler_params=_cparams("arbitrary", "arbitrary"),
        name="hgrn_scan",
    )(q, k, g, v, og, s0, gn.reshape(1, A_DV))


def _compress_kernel(x_ref, w1_ref, pos_ref, w1full_ref, b1_ref, w2_ref, b2_ref, o_ref):
    n = x_ref.shape[0]
    uv = jnp.dot(x_ref[...].astype(BF16), w1_ref[...].astype(BF16), preferred_element_type=F32)
    u = uv[:, :CMP_HID]
    v = uv[:, CMP_HID:]
    r = lax.broadcasted_iota(jnp.int32, (n, n), 0)
    c = lax.broadcasted_iota(jnp.int32, (n, n), 1)
    shift = (c == r + 1).astype(F32)
    v_next = jnp.dot(shift, v, precision=HIGHEST, preferred_element_type=F32)
    pos8 = jnp.broadcast_to(pos_ref[...], (SUBLANES, pos_ref.shape[1]))
    cvec = jnp.dot(pos8, w1full_ref[...], precision=HIGHEST, preferred_element_type=F32)[0:1] + b1_ref[...]
    hid = u + v_next + cvec
    act = hid * _sigmoid(hid)
    o_ref[...] = jnp.dot(act.astype(BF16), w2_ref[...].astype(BF16), preferred_element_type=F32) + b2_ref[...]


def _compress(xc, w1cat, pos_flat, w1, b1, w2dup, b2dup):
    _, BG, nch, kdim = xc.shape
    return pl.pallas_call(
        _compress_kernel,
        grid=(2, BG),
        in_specs=[pl.BlockSpec((None, None, nch, kdim), lambda a, i: (a, i, 0, 0)),
                  pl.BlockSpec((None, kdim, 2 * CMP_HID), lambda a, i: (a, 0, 0)),
                  pl.BlockSpec((None, 1, 2 * kdim), lambda a, i: (a, 0, 0)),
                  pl.BlockSpec((None, 2 * kdim, CMP_HID), lambda a, i: (a, 0, 0)),
                  pl.BlockSpec((None, 1, CMP_HID), lambda a, i: (a, 0, 0)),
                  pl.BlockSpec((None, CMP_HID, LANES), lambda a, i: (a, 0, 0)),
                  pl.BlockSpec((None, 1, LANES), lambda a, i: (a, 0, 0))],
        out_specs=pl.BlockSpec((None, None, nch, LANES), lambda a, i: (a, i, 0, 0)),
        out_shape=jax.ShapeDtypeStruct((2, BG, nch, LANES), F32),
        compiler_params=_cparams("arbitrary", "arbitrary"),
        name="compress_kv",
    )(xc, w1cat, pos_flat, w1, b1, w2dup, b2dup)


def _cmp_to_slc_map(ncp, ns):
    i = np.arange(ncp)[:, None]
    j = np.arange(LANES)[None, :]
    lo = np.maximum(i * CMP_STRIDE, j * SLC_BLOCK)
    hi = np.minimum(i * CMP_STRIDE + CMP_BLOCK, (j + 1) * SLC_BLOCK)
    m = np.maximum(hi - lo, 0) / CMP_STRIDE
    m = np.where(j < ns, m, 0.0)
    return jnp.asarray(m, dtype=F32)


def _nsa_attn_kernel(q_ref, gate_ref, ck_ref, cv_ref, cmap_ref, slc_ref, win_ref, o_ref,
                     *, tq, pos0, win_pos0, nc, ns, n_sel):
    qi = pl.program_id(1)
    R = B_GROUP
    rows = R * tq
    ncp = ck_ref.shape[2]
    n_slc_tiles = slc_ref.shape[1] // KEY_TILE
    n_win_tiles = win_ref.shape[1] // KEY_TILE
    half = B_DH
    lane_q = lax.broadcasted_iota(jnp.int32, (tq, LANES), 1)
    lo_q = lane_q < half
    lane_k = lax.broadcasted_iota(jnp.int32, (KEY_TILE, LANES), 1)
    lo_k = lane_k < half
    t_idx = lax.broadcasted_iota(jnp.int32, (tq, 1), 0)
    q_lo = pos0 + qi * tq
    qpos_i = q_lo + t_idx
    qpos = jnp.concatenate([qpos_i.astype(F32)] * R, axis=0)

    def flash_step(carry, s, mask, vals):
        m_old, l_old, acc = carry
        s = jnp.where(mask, s, NEG_INF)
        m_new = jnp.maximum(m_old, jnp.max(s, axis=-1, keepdims=True))
        alpha = jnp.exp(m_old - m_new)
        e = jnp.where(mask, jnp.exp(s - m_new), 0.0)
        l_new = alpha * l_old + jnp.sum(e, axis=-1, keepdims=True)
        acc = alpha * acc + jnp.dot(e.astype(BF16), vals, preferred_element_type=F32)
        return m_new, l_new, acc

    def flash_init():
        return (jnp.full((rows, 1), NEG_INF, F32), jnp.zeros((rows, 1), F32), jnp.zeros((rows, LANES), F32))

    def flash_done(carry):
        _, l, acc = carry
        return acc / jnp.maximum(l, 1e-30)

    def both_halves(raw_ref, t0, col, g):
        pair = raw_ref[0, pl.ds(t0, KEY_TILE), col * 2 * LANES + (g // 2) * LANES:
                       col * 2 * LANES + (g // 2 + 1) * LANES]
        swapped = pltpu.roll(pair, half, axis=1)
        if g % 2 == 0:
            return jnp.where(lo_k, pair, swapped).astype(BF16)
        return jnp.where(lo_k, swapped, pair).astype(BF16)

    for g in range(B_KV_HEADS):
        c0 = q_ref[0, :, g * 2 * LANES:g * 2 * LANES + LANES]
        c1 = q_ref[0, :, g * 2 * LANES + LANES:(g + 1) * 2 * LANES]
        q_st = jnp.concatenate([jnp.where(lo_q, c0, 0.0), jnp.where(lo_q, 0.0, c0),
                                jnp.where(lo_q, c1, 0.0), jnp.where(lo_q, 0.0, c1)], axis=0).astype(BF16)
        slope = jnp.concatenate(
            [jnp.full((tq, 1), 2.0 ** (-8.0 * (g * R + r + 1) / B_HEADS), F32) for r in range(R)], axis=0)

        ckb = ck_ref[0, g].astype(BF16)
        cvb = cv_ref[0, g].astype(BF16)
        s = lax.dot_general(q_st, ckb, (((1,), (1,)), ((), ())), preferred_element_type=F32)
        c_end = (lax.broadcasted_iota(jnp.int32, (1, ncp), 1) * CMP_STRIDE + (CMP_BLOCK - 1)).astype(F32)
        c_real = lax.broadcasted_iota(jnp.int32, (1, ncp), 1) < nc
        dist = qpos - c_end
        mask = (dist >= 0.0) & c_real
        s = jnp.where(mask, s - slope * dist, NEG_INF)
        m = jnp.max(s, axis=-1, keepdims=True)
        e = jnp.where(mask, jnp.exp(s - m), 0.0)
        p = e / jnp.maximum(jnp.sum(e, axis=-1, keepdims=True), 1e-30)
        o_cmp = jnp.dot(p.astype(BF16), cvb, preferred_element_type=F32)

        p_sum = p[0:tq] + p[tq:2 * tq] + p[2 * tq:3 * tq] + p[3 * tq:4 * tq]
        imp = jnp.dot(p_sum, cmap_ref[...], precision=HIGHEST, preferred_element_type=F32)
        cur = qpos_i // SLC_BLOCK
        causal_blk = lane_q * SLC_BLOCK <= qpos_i
        forced = (lane_q == 0) | (lane_q == cur) | (lane_q == cur - 1)
        score = jnp.where(forced, FORCE_SCORE, jnp.where(causal_blk, imp, -1.0))
        score = jnp.where(lane_q < ns, score, -jnp.inf)
        sel = jnp.zeros((tq, LANES), F32)
        for _ in range(n_sel):
            mx = jnp.max(score, axis=-1, keepdims=True)
            idx = jnp.min(jnp.where(score == mx, lane_q, LANES), axis=-1, keepdims=True)
            pick = lane_q == idx
            sel = jnp.where(pick & (mx >= 0.0), 1.0, sel)
            score = jnp.where(pick, -jnp.inf, score)
        sel_bf = sel.astype(BF16)

        q_hi = q_lo + tq - 1
        n_kt = jnp.minimum(q_hi // KEY_TILE + 1, n_slc_tiles)
        blk_row = lax.broadcasted_iota(jnp.int32, (LANES, KEY_TILE), 0)
        key_col = lax.broadcasted_iota(jnp.int32, (LANES, KEY_TILE), 1)
        kidx = lax.broadcasted_iota(jnp.int32, (1, KEY_TILE), 1)

        def slc_body(kt, carry):
            t0 = pl.multiple_of(kt * KEY_TILE, KEY_TILE)
            kb = both_halves(slc_ref, t0, 0, g)
            vb = both_halves(slc_ref, t0, 1, g)
            s = lax.dot_general(q_st, kb, (((1,), (1,)), ((), ())), preferred_element_type=F32)
            dist = qpos - (t0 + kidx).astype(F32)
            expand = (blk_row == (t0 + key_col) // SLC_BLOCK).astype(BF16)
            selk = jnp.dot(sel_bf, expand, preferred_element_type=F32)
            selk = jnp.concatenate([selk] * R, axis=0)
            mask = (selk > 0.5) & (dist >= 0.0)
            return flash_step(carry, s - slope * dist, mask, vb)

        o_slc = flash_done(lax.fori_loop(0, n_kt, slc_body, flash_init()))

        w_first = jnp.maximum(q_lo - (WINDOW - 1) - win_pos0, 0) // KEY_TILE
        w_last = jnp.minimum((q_hi - win_pos0) // KEY_TILE + 1, n_win_tiles)

        def win_body(kt, carry):
            t0 = pl.multiple_of(kt * KEY_TILE, KEY_TILE)
            kb = both_halves(win_ref, t0, 0, g)
            vb = both_halves(win_ref, t0, 1, g)
            s = lax.dot_general(q_st, kb, (((1,), (1,)), ((), ())), preferred_element_type=F32)
            dist = qpos - (win_pos0 + t0 + kidx).astype(F32)
            mask = (dist >= 0.0) & (dist < float(WINDOW))
            return flash_step(carry, s - slope * dist, mask, vb)

        o_win = flash_done(lax.fori_loop(w_first, w_last, win_body, flash_init()))

        def gate_col(j):
            base = j * B_HEADS + g * R
            return jnp.concatenate([gate_ref[0, :, base + r:base + r + 1] for r in range(R)], axis=0)

        o_st = gate_col(0) * o_cmp + gate_col(1) * o_slc + gate_col(2) * o_win
        o_ref[0, :, g * 2 * LANES:g * 2 * LANES + LANES] = jnp.where(lo_q, o_st[0:tq], o_st[tq:2 * tq])
        o_ref[0, :, g * 2 * LANES + LANES:(g + 1) * 2 * LANES] = jnp.where(lo_q, o_st[2 * tq:3 * tq], o_st[3 * tq:])


def _nsa_attn(q, gates, ck, cv, slc, slc_col, win, win_col, *, pos0, win_pos0, nc, ns):
    B, T, _ = q.shape
    tq = min(T, LANES)
    assert T % tq == 0
    ncp = ck.shape[2]
    n_sel = min(N_SELECT, ns)
    Ls, Lw = slc.shape[1], win.shape[1]
    assert Ls % KEY_TILE == 0 and Lw % KEY_TILE == 0
    kern = functools.partial(_nsa_attn_kernel, tq=tq, pos0=pos0, win_pos0=win_pos0, nc=nc, ns=ns, n_sel=n_sel)
    return pl.pallas_call(
        kern,
        grid=(B, T // tq),
        in_specs=[pl.BlockSpec((1, tq, D_MODEL), lambda b, i: (b, i, 0)),
                  pl.BlockSpec((1, tq, LANES), lambda b, i: (b, i, 0)),
                  pl.BlockSpec((1, B_KV_HEADS, ncp, LANES), lambda b, i: (b, 0, 0, 0)),
                  pl.BlockSpec((1, B_KV_HEADS, ncp, LANES), lambda b, i: (b, 0, 0, 0)),
                  pl.BlockSpec((ncp, LANES), lambda b, i: (0, 0)),
                  pl.BlockSpec((1, Ls, 4 * LANES), lambda b, i, c=slc_col: (b, 0, c)),
                  pl.BlockSpec((1, Lw, 4 * LANES), lambda b, i, c=win_col: (b, 0, c))],
        out_specs=pl.BlockSpec((1, tq, D_MODEL), lambda b, i: (b, i, 0)),
        out_shape=jax.ShapeDtypeStruct((B, T, D_MODEL), F32),
        compiler_params=_cparams("arbitrary", "arbitrary"),
        name="nsa_attn",
    )(q, gates, ck, cv, _cmp_to_slc_map(ncp, ns), slc, win)


def _moe_kernel(be_ref, nv_ref, x_ref, wg_ref, wl_ref, bg_ref, bl_ref, wd_ref, bd_ref, o_ref):
    i = pl.program_id(0)

    @pl.when(i < nv_ref[0])
    def _():
        x = x_ref[...]
        glu = jnp.dot(x, wg_ref[...], preferred_element_type=F32) + bg_ref[...]
        lin = jnp.dot(x, wl_ref[...], preferred_element_type=F32) + bl_ref[...]
        glu = jnp.minimum(glu, SWIGLU_LIMIT)
        lin = jnp.clip(lin, -SWIGLU_LIMIT, SWIGLU_LIMIT)
        act = glu * _sigmoid(SWIGLU_ALPHA * glu) * (lin + 1.0)
        o_ref[...] = jnp.dot(act.astype(BF16), wd_ref[...], preferred_element_type=F32) + bd_ref[...]

    @pl.when(i >= nv_ref[0])
    def _():
        o_ref[...] = jnp.zeros_like(o_ref)


def _moe_ffn_blocks(xr, blk_exp, nvalid, wg, wl, bg, bl, wd, bd):
    m_pad = xr.shape[0]
    n_blk = m_pad // MOE_ROWS
    wspec = lambda cols: pl.BlockSpec((None, D_MODEL, cols), lambda i, be, nv: (be[i], 0, 0))
    bspec = lambda cols: pl.BlockSpec((None, 1, cols), lambda i, be, nv: (be[i], 0, 0))
    grid_spec = pltpu.PrefetchScalarGridSpec(
        num_scalar_prefetch=2,
        grid=(n_blk,),
        in_specs=[pl.BlockSpec((MOE_ROWS, D_MODEL), lambda i, be, nv: (i, 0)),
                  wspec(D_FF), wspec(D_FF), bspec(D_FF), bspec(D_FF), wspec(D_MODEL), bspec(D_MODEL)],
        out_specs=pl.BlockSpec((MOE_ROWS, D_MODEL), lambda i, be, nv: (i, 0)),
    )
    return pl.pallas_call(
        _moe_kernel,
        grid_spec=grid_spec,
        out_shape=jax.ShapeDtypeStruct((m_pad, D_MODEL), F32),
        compiler_params=_cparams("arbitrary"),
        name="moe_ffn",
    )(blk_exp, nvalid, xr, wg, wl, bg, bl, wd, bd)


def _moe(h_flat, top_e, gate, wg, wl, bg, bl, wd, bd):
    N = h_flat.shape[0]
    M = N * TOP_K
    flat_e = top_e.reshape(M)
    order = jnp.argsort(flat_e, stable=True)
    se = flat_e[order]
    counts = jnp.bincount(flat_e, length=N_EXPERTS)
    padded = (counts + MOE_ROWS - 1) // MOE_ROWS * MOE_ROWS
    pend = jnp.cumsum(padded)
    pstart = pend - padded
    gstart = jnp.cumsum(counts) - counts
    dest = (pstart[se] + jnp.arange(M) - gstart[se]).astype(jnp.int32)
    n_blk = -(-(M + N_EXPERTS * (MOE_ROWS - 1)) // MOE_ROWS)
    m_pad = n_blk * MOE_ROWS
    row_tok = jnp.full((m_pad,), N, dtype=jnp.int32).at[dest].set((order // TOP_K).astype(jnp.int32))
    pos = jnp.zeros((M,), jnp.int32).at[order].set(dest)
    blk_exp = jnp.minimum(jnp.searchsorted(pend, jnp.arange(n_blk) * MOE_ROWS, side='right'),
                          N_EXPERTS - 1).astype(jnp.int32)
    nvalid = (pend[-1] // MOE_ROWS).astype(jnp.int32).reshape(1)
    x_pad = jnp.concatenate([h_flat.astype(BF16), jnp.zeros((1, D_MODEL), BF16)], axis=0)
    xr = x_pad[row_tok]
    yr = _moe_ffn_blocks(xr, blk_exp, nvalid, wg, wl, bg, bl, wd, bd)
    y = jnp.sum(yr[pos].reshape(N, TOP_K, D_MODEL) * gate[..., None], axis=1)
    return y


def _compress_inputs(cmp_raw):
    B, L, _ = cmp_raw.shape
    nch = L // CMP_STRIDE
    x = cmp_raw[:, :nch * CMP_STRIDE].reshape(B, nch, CMP_STRIDE, 2, B_KV_HEADS, B_DH)
    x = x.transpose(3, 0, 4, 1, 2, 5)
    return x.reshape(2, B * B_KV_HEADS, nch, CMP_STRIDE * B_DH), nch


def _pad_rows(a, mult, axis=1):
    n = a.shape[axis]
    pad = (-n) % mult
    if pad == 0:
        return a
    widths = [(0, 0)] * a.ndim
    widths[axis] = (0, pad)
    return jnp.pad(a, widths)


def kernel(x_prompt, x_sample, state_hgrn, cache_cmp_kv, cache_slc_kv, state_win_kv, page_table, c_prompt, c_sample,
           ada_w, ada_b, ln_g, ln_b, a_w_in, a_lb, a_onorm_g, a_w_o, kv_ada_w, kv_ada_b, kv_w, cmp_pos, cmp_w1, cmp_b1,
           cmp_w2, cmp_b2, b_w_in, b_w_o, moe_wr, moe_br, moe_w_up, moe_b_up, moe_w_down, moe_b_down):
    Bp, Tp, _ = x_prompt.shape
    Bs, Ts, _ = x_sample.shape
    n_pages = page_table.shape[1]
    past_len = n_pages * PAGE_SIZE
    keep_s = state_win_kv.shape[1]
    G, DH = B_KV_HEADS, B_DH
    Np, Ns = Bp * Tp, Bs * Ts

    a_w_in_bf = a_w_in.astype(BF16)
    a_w_o_bf = a_w_o.astype(BF16)
    b_w_o_bf = b_w_o.astype(BF16)
    kv_w_bf = kv_w.astype(BF16)
    wq_bf = b_w_in[:, :, :D_MODEL].astype(BF16)
    n_l = b_w_in.shape[0]
    wg = b_w_in[:, :, D_MODEL:].reshape(n_l, D_MODEL, B_HEADS, 3).transpose(0, 1, 3, 2).reshape(n_l, D_MODEL, 3 * B_HEADS)
    wg_bf = jnp.pad(wg, ((0, 0), (0, 0), (0, LANES - 3 * B_HEADS))).astype(BF16)
    wr_pad = jnp.pad(moe_wr, ((0, 0), (0, 0), (0, LANES - N_EXPERTS)))
    br_pad = jnp.pad(moe_br, ((0, 0), (0, LANES - N_EXPERTS)), constant_values=NEG_INF).reshape(DEPTH, 1, LANES)
    w_glu = moe_w_up[..., 0::2].astype(BF16)
    w_lin = moe_w_up[..., 1::2].astype(BF16)
    b_glu = moe_b_up[..., 0::2].reshape(DEPTH, N_EXPERTS, 1, D_FF)
    b_lin = moe_b_up[..., 1::2].reshape(DEPTH, N_EXPERTS, 1, D_FF)
    w_dn = moe_w_down.astype(BF16)
    b_dn = moe_b_down.reshape(DEPTH, N_EXPERTS, 1, D_MODEL)
    lb_p = jax.nn.softmax(a_lb.astype(F32), axis=0)
    lb_all = jnp.maximum(jnp.cumsum(lb_p, axis=0) - lb_p[0], 0.0)
    w1cat = jnp.concatenate([cmp_w1[:, :CMP_STRIDE * DH], cmp_w1[:, CMP_STRIDE * DH:]], axis=-1)
    pos_flat = cmp_pos.reshape(2, 1, CMP_BLOCK * DH)
    w2dup = jnp.concatenate([cmp_w2, cmp_w2], axis=-1)
    b2dup = jnp.concatenate([cmp_b2, cmp_b2], axis=-1).reshape(2, 1, LANES)

    n_c = Bp + Bs
    c_all = _pad_rows(jnp.concatenate([c_prompt, c_sample], axis=0), SUBLANES, axis=0)
    mod_all = _silu_linear(c_all, ada_w, ada_b)
    kvmod_all = _silu_linear(c_all, kv_ada_w[None], kv_ada_b[None])[0]
    mods = [(mod_all[l, :Bp].reshape(Bp, 1, 6 * D_MODEL), mod_all[l, Bp:n_c].reshape(Bs, 1, 6 * D_MODEL))
            for l in range(DEPTH)]
    kvmods = (kvmod_all[:Bp].reshape(Bp, 1, 2 * D_MODEL), kvmod_all[Bp:n_c].reshape(Bs, 1, 2 * D_MODEL))

    xs = [x_prompt, x_sample]
    s0s = [jnp.zeros((N_A_LAYERS, Bp, A_HEADS, A_DK, A_DV), F32), state_hgrn]
    hgrn_out = [[], []]
    ctx = [None, None]
    new_kv = [None, None]

    def build_ctx(t, x):
        B, T, _ = x.shape
        kv = _kv_proj(x, kvmods[t], kv_w_bf)
        if t == 0:
            cmp_raw = kv[:, :, :4 * LANES]
            slc_arr, slc_col, win_arr, win_col = kv, 1, kv, 2
            L = T
            win_pos0, pos0 = 0, 0
            win_out = kv[:, T - min(WINDOW, T):, 8 * LANES:]
            if T % KEY_TILE:
                slc_arr = _pad_rows(kv[:, :, 4 * LANES:8 * LANES], KEY_TILE)
                win_arr = _pad_rows(kv[:, :, 8 * LANES:], KEY_TILE)
                slc_col = win_col = 0
        else:
            past_cmp = cache_cmp_kv.reshape(-1, PAGE_SIZE, 4 * LANES)[page_table].reshape(B, past_len, 4 * LANES)
            past_slc = cache_slc_kv.reshape(-1, PAGE_SIZE, 4 * LANES)[page_table].reshape(B, past_len, 4 * LANES)
            cmp_raw = jnp.concatenate([past_cmp, kv[:, :, :4 * LANES]], axis=1)
            slc_arr = _pad_rows(jnp.concatenate([past_slc, kv[:, :, 4 * LANES:8 * LANES]], axis=1), KEY_TILE)
            win_all = jnp.concatenate([state_win_kv.reshape(B, keep_s, 4 * LANES), kv[:, :, 8 * LANES:]], axis=1)
            win_arr = _pad_rows(win_all, KEY_TILE)
            slc_col = win_col = 0
            L = past_len + T
            pos0 = past_len
            win_pos0 = past_len - keep_s
            win_out = win_all[:, -keep_s:]
        xc, nch = _compress_inputs(cmp_raw)
        nc = nch - CMP_RATIO + 1
        ns = -(-L // SLC_BLOCK)
        ckv = _compress(xc, w1cat, pos_flat, cmp_w1, cmp_b1.reshape(2, 1, CMP_HID), w2dup, b2dup)
        ckv = _pad_rows(ckv.reshape(2, B, G, nch, LANES), LANES, axis=3)
        info = dict(ck=ckv[0], cv=ckv[1], slc=slc_arr, slc_col=slc_col, win=win_arr, win_col=win_col,
                    pos0=pos0, win_pos0=win_pos0, nc=nc, ns=ns)
        outs = (kv[:, :, :4 * LANES].reshape(B, T, 2, G, DH), kv[:, :, 4 * LANES:8 * LANES].reshape(B, T, 2, G, DH),
                win_out.reshape(B, -1, 2, G, DH))
        return info, outs

    for l in range(DEPTH):
        o_mix = []
        for t in range(2):
            x = xs[t]
            mod = mods[l][t]
            if l < N_A_LAYERS:
                q, k, g, v, og = _hgrn_in(x, mod, a_w_in_bf[l], lb_all[l])
                o, S = _hgrn_scan(q, k, g, v, og, s0s[t][l], a_onorm_g[l])
                hgrn_out[t].append(S)
            else:
                j = l - N_A_LAYERS
                if ctx[t] is None:
                    ctx[t], new_kv[t] = build_ctx(t, x)
                c = ctx[t]
                q, gates = _nsa_in(x, mod, wq_bf[j], wg_bf[j])
                o = _nsa_attn(q, gates, c['ck'], c['cv'], c['slc'], c['slc_col'], c['win'], c['win_col'],
                              pos0=c['pos0'], win_pos0=c['win_pos0'], nc=c['nc'], ns=c['ns'])
            o_mix.append(o)
        w_o = a_w_o_bf[l] if l < N_A_LAYERS else b_w_o_bf[l - N_A_LAYERS]
        hf, te, tg = [], [], []
        for t in range(2):
            xn, h, e, gt = _mixer_out(o_mix[t], w_o, xs[t], mods[l][t], ln_g[l, 0], ln_b[l, 0], wr_pad[l], br_pad[l])
            xs[t] = xn
            hf.append(h.reshape(-1, D_MODEL))
            te.append(e.reshape(-1, LANES)[:, :TOP_K])
            tg.append(gt.reshape(-1, LANES)[:, :TOP_K])
        y = _moe(jnp.concatenate(hf, axis=0), jnp.concatenate(te, axis=0), jnp.concatenate(tg, axis=0),
                 w_glu[l], w_lin[l], b_glu[l], b_lin[l], w_dn[l], b_dn[l])
        ys = [y[:Np].reshape(Bp, Tp, D_MODEL), y[Np:].reshape(Bs, Ts, D_MODEL)]
        for t in range(2):
            xs[t] = _ffn_out(xs[t], ys[t], mods[l][t], ln_g[l, 1], ln_b[l, 1])

    return (xs[0], xs[1], jnp.stack(hgrn_out[0], axis=0), jnp.stack(hgrn_out[1], axis=0),
            new_kv[0][0], new_kv[1][0], new_kv[0][1], new_kv[1][1], new_kv[0][2], new_kv[1][2])
```

```python
import functools

import numpy as np
import jax
import jax.numpy as jnp
from jax import lax
from jax.experimental import pallas as pl
from jax.experimental.pallas import tpu as pltpu

F32 = jnp.float32
BF16 = jnp.bfloat16
I32 = jnp.int32
HIGHEST = lax.Precision.HIGHEST

D_MODEL = 1024
DEPTH = 4
PAGE_SIZE = 128
N_A_LAYERS = DEPTH // 2
A_HEADS = 8
A_DK = 128
A_DV = D_MODEL // A_HEADS
A_DF = A_HEADS * A_DK
HGRN_CHUNK = 32
B_HEADS = 16
B_DH = D_MODEL // B_HEADS
B_KV_HEADS = 4
B_GROUP = B_HEADS // B_KV_HEADS
CMP_BLOCK = 32
CMP_STRIDE = 16
CMP_RATIO = CMP_BLOCK // CMP_STRIDE
CMP_HID = 2 * B_DH
SLC_BLOCK = 64
N_SELECT = 16
WINDOW = 512
FORCE_SCORE = 1e4
N_EXPERTS = 32
TOP_K = 4
D_FF = D_MODEL
SWIGLU_LIMIT = 7.0
SWIGLU_ALPHA = 1.702
DN_ALPHA = (2 * DEPTH) ** 0.25
LN_EPS = 1e-5
RMS_EPS = 1e-6
NEG_INF = -1e30

LANES = 128
SUBLANES = 8
MXU_DIM = 256
ROW_TILE = 256
MOE_ROWS = 256
KEY_TILE = 256
VMEM_LIMIT = 56 * 1024 * 1024

AUG_POS_LANES = 4
AUG_BLK_LANE0 = 8
UNSELECTED = -(2.0 ** 30)


def _cparams(*sem):
    return pltpu.CompilerParams(dimension_semantics=sem, vmem_limit_bytes=VMEM_LIMIT)


def _row_tiles(B, T):
    if T >= ROW_TILE:
        assert T % ROW_TILE == 0
        return 1, ROW_TILE
    sb = max(1, ROW_TILE // T)
    while B % sb:
        sb //= 2
    return sb, T


def _largest_divisor(n, candidates):
    for c in candidates:
        if n % c == 0:
            return c
    raise ValueError(n)


def _sigmoid(x):
    return 1.0 / (1.0 + jnp.exp(-x))


def _linear_kernel(x_ref, w_ref, b_ref, o_ref):
    x = x_ref[...]
    x = x * _sigmoid(x)
    o_ref[...] = jnp.dot(x.astype(BF16), w_ref[...].astype(BF16), preferred_element_type=F32) + b_ref[...]


def _silu_linear(x, w, b):
    L, K, N = w.shape
    M = x.shape[0]
    tn = 1024
    return pl.pallas_call(
        _linear_kernel,
        grid=(L, N // tn),
        in_specs=[pl.BlockSpec((M, K), lambda l, j: (0, 0)),
                  pl.BlockSpec((None, K, tn), lambda l, j: (l, 0, j)),
                  pl.BlockSpec((None, 1, tn), lambda l, j: (l, 0, j))],
        out_specs=pl.BlockSpec((None, M, tn), lambda l, j: (l, 0, j)),
        out_shape=jax.ShapeDtypeStruct((L, M, N), F32),
        compiler_params=_cparams("arbitrary", "arbitrary"),
        name="silu_linear",
    )(x, w, b.reshape(L, 1, N))


def _modulate(x_ref, sc_ref, sh_ref):
    sb, tt, d = x_ref.shape
    h = x_ref[...] * (1.0 + sc_ref[...]) + sh_ref[...]
    return h.reshape(sb * tt, d)


def _hgrn_in_kernel(x_ref, sh_ref, sc_ref, w_ref, lb_ref, q_ref, k_ref, g_ref, v_ref, og_ref):
    sb, tt, d = x_ref.shape
    h = _modulate(x_ref, sc_ref, sh_ref).astype(BF16)
    proj = jnp.dot(h, w_ref[...], preferred_element_type=F32)
    qz = proj[:, :A_DF]
    fz = proj[:, A_DF:2 * A_DF]
    lb = lb_ref[...]
    log_sig = jnp.minimum(fz, 0.0) - jnp.log1p(jnp.exp(-jnp.abs(fz)))
    pos = lb > 0.0
    log_lb = jnp.where(pos, jnp.log(jnp.where(pos, lb, 1.0)), NEG_INF)
    other = jnp.log1p(-lb) + log_sig
    logf = jnp.maximum(log_lb, other) + jnp.log1p(jnp.exp(-jnp.abs(log_lb - other)))
    q_ref[...] = (qz * _sigmoid(qz)).reshape(sb, tt, A_DF)
    g_ref[...] = logf.reshape(sb, tt, A_DF)
    k_ref[...] = ((1.0 - lb) * _sigmoid(-fz)).reshape(sb, tt, A_DF)
    v_ref[...] = proj[:, 2 * A_DF:2 * A_DF + D_MODEL].reshape(sb, tt, D_MODEL)
    og_ref[...] = proj[:, 2 * A_DF + D_MODEL:].reshape(sb, tt, D_MODEL)


def _mod_spec(sb, comp):
    return pl.BlockSpec((sb, 1, D_MODEL), lambda i, j, c=comp: (i, 0, c))


def _tok_spec(sb, tt, width=D_MODEL):
    return pl.BlockSpec((sb, tt, width), lambda i, j: (i, j, 0))


def _hgrn_in(x, mod, w_bf, lb):
    B, T, _ = x.shape
    sb, tt = _row_tiles(B, T)
    n_out = w_bf.shape[1]
    shp = jax.ShapeDtypeStruct((B, T, D_MODEL), F32)
    return pl.pallas_call(
        _hgrn_in_kernel,
        grid=(B // sb, T // tt),
        in_specs=[_tok_spec(sb, tt), _mod_spec(sb, 0), _mod_spec(sb, 1),
                  pl.BlockSpec((D_MODEL, n_out), lambda i, j: (0, 0)),
                  pl.BlockSpec((1, A_DF), lambda i, j: (0, 0))],
        out_specs=[_tok_spec(sb, tt)] * 5,
        out_shape=[shp] * 5,
        compiler_params=_cparams("arbitrary", "arbitrary"),
        name="hgrn_in",
    )(x, mod, mod, w_bf, lb.reshape(1, A_DF))


def _nsa_in_kernel(x_ref, sh_ref, sc_ref, wq_ref, wg_ref, q_ref, gate_ref):
    sb, tt, d = x_ref.shape
    h = _modulate(x_ref, sc_ref, sh_ref).astype(BF16)
    q = jnp.dot(h, wq_ref[...], preferred_element_type=F32) * (B_DH ** -0.5)
    gz = jnp.dot(h, wg_ref[...], preferred_element_type=F32)
    q_ref[...] = q.reshape(sb, tt, D_MODEL)
    gate_ref[...] = _sigmoid(gz).reshape(sb, tt, LANES)


def _nsa_in(x, mod, wq_bf, wg_bf):
    B, T, _ = x.shape
    sb, tt = _row_tiles(B, T)
    return pl.pallas_call(
        _nsa_in_kernel,
        grid=(B // sb, T // tt),
        in_specs=[_tok_spec(sb, tt), _mod_spec(sb, 0), _mod_spec(sb, 1),
                  pl.BlockSpec((D_MODEL, D_MODEL), lambda i, j: (0, 0)),
                  pl.BlockSpec((D_MODEL, LANES), lambda i, j: (0, 0))],
        out_specs=[_tok_spec(sb, tt), _tok_spec(sb, tt, LANES)],
        out_shape=[jax.ShapeDtypeStruct((B, T, D_MODEL), F32), jax.ShapeDtypeStruct((B, T, LANES), F32)],
        compiler_params=_cparams("arbitrary", "arbitrary"),
        name="nsa_in",
    )(x, mod, mod, wq_bf, wg_bf)


KV_RAW = 6 * B_KV_HEADS * B_DH
KV_DUP = 4 * B_KV_HEADS * LANES


def _kv_proj_kernel(x_ref, sh_ref, sc_ref, w_ref, kv_ref, dup_ref):
    sb, tt, d = x_ref.shape
    h = _modulate(x_ref, sc_ref, sh_ref).astype(BF16)
    kv = jnp.dot(h, w_ref[...], preferred_element_type=F32)
    kv_ref[...] = kv[:, :KV_RAW].reshape(sb, tt, KV_RAW)
    dup_ref[...] = kv[:, KV_RAW:].reshape(sb, tt, KV_DUP).astype(dup_ref.dtype)


def _kv_proj(x, kvmod, w_bf, dup_dtype):
    B, T, _ = x.shape
    sb, tt = _row_tiles(B, T)
    n_out = w_bf.shape[1]
    return pl.pallas_call(
        _kv_proj_kernel,
        grid=(B // sb, T // tt),
        in_specs=[_tok_spec(sb, tt), _mod_spec(sb, 0), _mod_spec(sb, 1),
                  pl.BlockSpec((D_MODEL, n_out), lambda i, j: (0, 0))],
        out_specs=[_tok_spec(sb, tt, KV_RAW), _tok_spec(sb, tt, KV_DUP)],
        out_shape=[jax.ShapeDtypeStruct((B, T, KV_RAW), F32), jax.ShapeDtypeStruct((B, T, KV_DUP), dup_dtype)],
        compiler_params=_cparams("arbitrary", "arbitrary"),
        name="kv_proj",
    )(x, kvmod, kvmod, w_bf)


def _layer_norm_rows(z, g, b):
    mu = jnp.mean(z, axis=-1, keepdims=True)
    zc = z - mu
    var = jnp.mean(zc * zc, axis=-1, keepdims=True)
    return zc * lax.rsqrt(var + LN_EPS) * g + b


def _mixer_out_kernel(o_ref, w_ref, x_ref, gm_ref, shf_ref, scf_ref, lng_ref, lnb_ref, wr_ref, br_ref,
                      xn_ref, hf_ref, te_ref, tg_ref):
    sb, tt, d = x_ref.shape
    rows = sb * tt
    o = o_ref[...].reshape(rows, d).astype(BF16)
    y = jnp.dot(o, w_ref[...], preferred_element_type=F32).reshape(sb, tt, d)
    z = DN_ALPHA * x_ref[...] + (1.0 + gm_ref[...]) * y
    xn = _layer_norm_rows(z, lng_ref[...], lnb_ref[...])
    xn_ref[...] = xn
    hf = xn * (1.0 + scf_ref[...]) + shf_ref[...]
    hf_ref[...] = hf
    logits = jnp.dot(hf.reshape(rows, d), wr_ref[...], precision=HIGHEST,
                     preferred_element_type=F32) + br_ref[...]
    lane = lax.broadcasted_iota(I32, (rows, LANES), 1)
    te = jnp.zeros((rows, LANES), I32)
    tv = jnp.zeros((rows, LANES), F32)
    v0 = None
    denom = None
    for k in range(TOP_K):
        m = jnp.max(logits, axis=-1, keepdims=True)
        idx = jnp.min(jnp.where(logits == m, lane, LANES), axis=-1, keepdims=True)
        if k == 0:
            v0 = m
            e = jnp.ones_like(m)
            denom = e
        else:
            e = jnp.exp(m - v0)
            denom = denom + e
        te = jnp.where(lane == k, idx, te)
        tv = jnp.where(lane == k, e, tv)
        logits = jnp.where(lane == idx, -jnp.inf, logits)
    te_ref[...] = te.reshape(sb, tt, LANES)
    tg_ref[...] = (tv / denom).reshape(sb, tt, LANES)


def _mixer_out(o, w_bf, x, mod, ln_g, ln_b, wr_pad, br_pad):
    B, T, _ = x.shape
    sb, tt = _row_tiles(B, T)
    vec = pl.BlockSpec((1, 1, D_MODEL), lambda i, j: (0, 0, 0))
    return pl.pallas_call(
        _mixer_out_kernel,
        grid=(B // sb, T // tt),
        in_specs=[_tok_spec(sb, tt), pl.BlockSpec((D_MODEL, D_MODEL), lambda i, j: (0, 0)), _tok_spec(sb, tt),
                  _mod_spec(sb, 2), _mod_spec(sb, 3), _mod_spec(sb, 4), vec, vec,
                  pl.BlockSpec((D_MODEL, LANES), lambda i, j: (0, 0)),
                  pl.BlockSpec((1, LANES), lambda i, j: (0, 0))],
        out_specs=[_tok_spec(sb, tt), _tok_spec(sb, tt), _tok_spec(sb, tt, LANES), _tok_spec(sb, tt, LANES)],
        out_shape=[jax.ShapeDtypeStruct((B, T, D_MODEL), F32), jax.ShapeDtypeStruct((B, T, D_MODEL), F32),
                   jax.ShapeDtypeStruct((B, T, LANES), I32), jax.ShapeDtypeStruct((B, T, LANES), F32)],
        compiler_params=_cparams("arbitrary", "arbitrary"),
        name="mixer_out",
    )(o, w_bf, x, mod, mod, mod, ln_g.reshape(1, 1, D_MODEL), ln_b.reshape(1, 1, D_MODEL), wr_pad, br_pad)


def _hgrn_scan_kernel(q_ref, k_ref, g_ref, v_ref, og_ref, s0_ref, gn_ref, o_ref, sout_ref,
                      st_ref, b_ref, k_scr, v_scr, *, chunk, n_chunks):
    tb = pl.program_id(1)
    C = chunk
    nsub = C // SUBLANES

    @pl.when(tb == 0)
    def _():
        for h in range(A_HEADS):
            st_ref[h] = s0_ref[0, h].T

    row = lax.broadcasted_iota(I32, (C, C), 0)
    col = lax.broadcasted_iota(I32, (C, C), 1)
    tri = (row >= col).astype(F32)
    sub_row = lax.broadcasted_iota(I32, (SUBLANES, A_DK), 0)
    gn = gn_ref[...]

    def chunk_body(c, carry):
        r0 = pl.multiple_of(c * C, C)
        for h in range(A_HEADS):
            ls = slice(h * A_DK, (h + 1) * A_DK)
            g = g_ref[0, pl.ds(r0, C), ls]
            b = jnp.dot(tri, g, precision=HIGHEST, preferred_element_type=F32)
            b_ref[...] = b
            q = q_ref[0, pl.ds(r0, C), ls]
            k = k_ref[0, pl.ds(r0, C), ls]
            v = v_ref[0, pl.ds(r0, C), ls]
            k_scr[...] = k
            v_scr[...] = v
            st = st_ref[h]
            o_inter = lax.dot_general((q * jnp.exp(b)).astype(BF16), st.astype(BF16),
                                      (((1,), (1,)), ((), ())), preferred_element_type=F32)
            acc = [jnp.zeros((SUBLANES, A_DV), F32) for _ in range(nsub)]
            for s in range(C):
                j = s // SUBLANES
                b_s = b_ref[pl.ds(s, 1), :]
                k_s = k_scr[pl.ds(s, 1), :]
                v_s = v_scr[pl.ds(s, 1), :]
                for i in range(j, nsub):
                    rs = slice(i * SUBLANES, (i + 1) * SUBLANES)
                    dlt = b[rs] - b_s
                    if i == j:
                        causal = sub_row >= (s % SUBLANES)
                        e = jnp.where(causal, jnp.exp(jnp.where(causal, dlt, 0.0)), 0.0)
                    else:
                        e = jnp.exp(dlt)
                    w = jnp.sum(q[rs] * e * k_s, axis=-1, keepdims=True)
                    acc[i] = acc[i] + w * v_s
            o = o_inter + (jnp.concatenate(acc, axis=0) if nsub > 1 else acc[0])
            b_last = b_ref[pl.ds(C - 1, 1), :]
            kd = k * jnp.exp(b_last - b)
            st_new = st * jnp.exp(b_last) + lax.dot_general(
                v.astype(BF16), kd.astype(BF16), (((0,), (0,)), ((), ())), preferred_element_type=F32)
            st_ref[h] = st_new
            og = og_ref[0, pl.ds(r0, C), ls]
            on = o * lax.rsqrt(jnp.mean(o * o, axis=-1, keepdims=True) + RMS_EPS) * gn
            o_ref[0, pl.ds(r0, C), ls] = on * (og * _sigmoid(og))
        return carry

    lax.fori_loop(0, n_chunks, chunk_body, 0)

    @pl.when(tb == pl.num_programs(1) - 1)
    def _():
        for h in range(A_HEADS):
            sout_ref[0, h] = st_ref[h].T


def _hgrn_scan(q, k, g, v, og, s0, gn):
    B, T, _ = q.shape
    C = HGRN_CHUNK if T % HGRN_CHUNK == 0 else T
    assert C % SUBLANES == 0
    tblk = min(T, ROW_TILE)
    assert T % tblk == 0 and tblk % C == 0
    tok = pl.BlockSpec((1, tblk, D_MODEL), lambda b, t: (b, t, 0))
    st = pl.BlockSpec((1, A_HEADS, A_DK, A_DV), lambda b, t: (b, 0, 0, 0))
    return pl.pallas_call(
        functools.partial(_hgrn_scan_kernel, chunk=C, n_chunks=tblk // C),
        grid=(B, T // tblk),
        in_specs=[tok, tok, tok, tok, tok, st, pl.BlockSpec((1, A_DV), lambda b, t: (0, 0))],
        out_specs=[tok, st],
        out_shape=[jax.ShapeDtypeStruct((B, T, D_MODEL), F32),
                   jax.ShapeDtypeStruct((B, A_HEADS, A_DK, A_DV), F32)],
        scratch_shapes=[pltpu.VMEM((A_HEADS, A_DV, A_DK), F32), pltpu.VMEM((C, A_DK), F32),
                        pltpu.VMEM((C, A_DK), F32), pltpu.VMEM((C, A_DV), F32)],
        compiler_params=_cparams("arbitrary", "arbitrary"),
        name="hgrn_scan",
    )(q, k, g, v, og, s0, gn.reshape(1, A_DV))


def _cmp_const_kernel(pos_ref, w1_ref, b1_ref, o_ref):
    pos8 = jnp.broadcast_to(pos_ref[...], (SUBLANES, pos_ref.shape[1]))
    o_ref[...] = jnp.dot(pos8, w1_ref[...], precision=HIGHEST, preferred_element_type=F32)[0:1] + b1_ref[...]


def _cmp_const(pos_flat, w1, b1):
    kdim = w1.shape[1]
    return pl.pallas_call(
        _cmp_const_kernel,
        grid=(2,),
        in_specs=[pl.BlockSpec((None, 1, kdim), lambda a: (a, 0, 0)),
                  pl.BlockSpec((None, kdim, CMP_HID), lambda a: (a, 0, 0)),
                  pl.BlockSpec((None, 1, CMP_HID), lambda a: (a, 0, 0))],
        out_specs=pl.BlockSpec((None, 1, CMP_HID), lambda a: (a, 0, 0)),
        out_shape=jax.ShapeDtypeStruct((2, 1, CMP_HID), F32),
        compiler_params=_cparams("arbitrary"),
        name="cmp_const",
    )(pos_flat, w1, b1)


def _compress_kernel(x_ref, w1_ref, cvec_ref, w2_ref, b2_ref, o_ref):
    gb, n, kdim = x_ref.shape
    rows = gb * n
    uv = jnp.dot(x_ref[...].reshape(rows, kdim).astype(BF16), w1_ref[...].astype(BF16),
                 preferred_element_type=F32)
    u = uv[:, :CMP_HID]
    v = uv[:, CMP_HID:]
    v_next = pltpu.roll(v, rows - 1, axis=0)
    hid = u + v_next + cvec_ref[...]
    act = hid * _sigmoid(hid)
    out = jnp.dot(act.astype(BF16), w2_ref[...].astype(BF16), preferred_element_type=F32) + b2_ref[...]
    o_ref[...] = out.reshape(gb, n, LANES)


def _compress(xc, w1cat, cvec, w2dup, b2dup):
    _, BG, nch, kdim = xc.shape
    gb = _largest_divisor(BG, (8, 4, 2, 1))
    return pl.pallas_call(
        _compress_kernel,
        grid=(2, BG // gb),
        in_specs=[pl.BlockSpec((None, gb, nch, kdim), lambda a, i: (a, i, 0, 0)),
                  pl.BlockSpec((None, kdim, 2 * CMP_HID), lambda a, i: (a, 0, 0)),
                  pl.BlockSpec((None, 1, CMP_HID), lambda a, i: (a, 0, 0)),
                  pl.BlockSpec((None, CMP_HID, LANES), lambda a, i: (a, 0, 0)),
                  pl.BlockSpec((None, 1, LANES), lambda a, i: (a, 0, 0))],
        out_specs=pl.BlockSpec((None, gb, nch, LANES), lambda a, i: (a, i, 0, 0)),
        out_shape=jax.ShapeDtypeStruct((2, BG, nch, LANES), F32),
        compiler_params=_cparams("arbitrary", "arbitrary"),
        name="compress_kv",
    )(xc, w1cat, cvec, w2dup, b2dup)


def _pages_kernel(pt_ref, page_ref, newk_ref, newv_ref, ko_ref, vo_ref, *, n_pages):
    p = pl.program_id(1)
    half = B_DH
    lo = lax.broadcasted_iota(I32, (PAGE_SIZE, LANES), 1) < half

    @pl.when(p < n_pages)
    def _():
        for col, o_ref in ((0, ko_ref), (1, vo_ref)):
            for m in range(B_KV_HEADS // 2):
                c0 = col * B_KV_HEADS * B_DH + m * LANES
                pair = page_ref[:, c0:c0 + LANES]
                swapped = pltpu.roll(pair, half, axis=1)
                o_ref[0, :, (2 * m) * LANES:(2 * m + 1) * LANES] = jnp.where(lo, pair, swapped).astype(BF16)
                o_ref[0, :, (2 * m + 1) * LANES:(2 * m + 2) * LANES] = jnp.where(lo, swapped, pair).astype(BF16)

    @pl.when(p == n_pages)
    def _():
        t = newk_ref.shape[1]
        pad = jnp.zeros((PAGE_SIZE - t, ko_ref.shape[2]), F32)
        ko_ref[0] = jnp.concatenate([newk_ref[0], pad], axis=0).astype(BF16)
        vo_ref[0] = jnp.concatenate([newv_ref[0], pad], axis=0).astype(BF16)

    @pl.when(p > n_pages)
    def _():
        ko_ref[...] = jnp.zeros_like(ko_ref)
        vo_ref[...] = jnp.zeros_like(vo_ref)


def _pages_to_dup(pool, table, new_dup, k_col, v_col, total_len):
    B, n_pages = table.shape
    T = new_dup.shape[1]
    assert T <= PAGE_SIZE and total_len % PAGE_SIZE == 0 and total_len // PAGE_SIZE > n_pages
    width = B_KV_HEADS * LANES
    grid_spec = pltpu.PrefetchScalarGridSpec(
        num_scalar_prefetch=1,
        grid=(B, total_len // PAGE_SIZE),
        in_specs=[pl.BlockSpec((None, PAGE_SIZE, width),
                               lambda b, p, pt: (pt[b, jnp.minimum(p, n_pages - 1)], 0, 0)),
                  pl.BlockSpec((1, T, width), lambda b, p, pt, c=k_col: (b, 0, c)),
                  pl.BlockSpec((1, T, width), lambda b, p, pt, c=v_col: (b, 0, c))],
        out_specs=[pl.BlockSpec((1, PAGE_SIZE, width), lambda b, p, pt: (b, p, 0))] * 2,
    )
    return pl.pallas_call(
        functools.partial(_pages_kernel, n_pages=n_pages),
        grid_spec=grid_spec,
        out_shape=[jax.ShapeDtypeStruct((B, total_len, width), BF16)] * 2,
        compiler_params=_cparams("arbitrary", "arbitrary"),
        name="pages_to_dup",
    )(table, pool, new_dup, new_dup)


def _cmp_to_slc_map_t(ncp, ns):
    j = np.arange(LANES)[:, None]
    i = np.arange(ncp)[None, :]
    lo = np.maximum(i * CMP_STRIDE, j * SLC_BLOCK)
    hi = np.minimum(i * CMP_STRIDE + CMP_BLOCK, (j + 1) * SLC_BLOCK)
    m = np.maximum(hi - lo, 0) / CMP_STRIDE
    return jnp.asarray(np.where(j < ns, m, 0.0), dtype=F32)


def _aug_table(length, pos0, with_blocks):
    pos = np.arange(length) + pos0
    a = np.zeros((length, LANES), np.float32)
    a[:, 0] = (pos // MXU_DIM) * MXU_DIM
    a[:, 1] = pos % MXU_DIM
    a[:, 2] = a[:, 0]
    a[:, 3] = a[:, 1]
    if with_blocks:
        blk = np.arange(length) // SLC_BLOCK
        assert AUG_BLK_LANE0 + blk.max() < LANES
        a[np.arange(length), AUG_BLK_LANE0 + blk] = 1.0
    return jnp.asarray(a, dtype=BF16)


def _split_bf16(x):
    hi = float(np.float32(x).astype(BF16).astype(np.float32))
    lo = float(np.float32(x - hi).astype(BF16).astype(np.float32))
    return hi, lo


def _nsa_attn_kernel(q_ref, gate_ref, ck_ref, cv_ref, cmapt_ref, kslc_ref, vslc_ref, kwin_ref, vwin_ref,
                     augs_ref, augw_ref, o_ref, qa_ref, m_ref, l_ref, acc_ref, sc_ref,
                     *, tq, pos0, win_pos0, nc, ns, n_sel):
    qi = pl.program_id(1)
    R = B_GROUP
    rows = R * tq
    chunk = min(rows, LANES)
    n_chunk = rows // chunk
    ncp = ck_ref.shape[2]
    nsp = sc_ref.shape[0]
    n_slc_tiles = kslc_ref.shape[1] // KEY_TILE
    n_win_tiles = kwin_ref.shape[1] // KEY_TILE
    half = B_DH
    lane_q = lax.broadcasted_iota(I32, (tq, LANES), 1)
    lo_q = lane_q < half
    t_idx = lax.broadcasted_iota(I32, (tq, 1), 0)
    q_lo = pos0 + qi * tq
    q_hi = q_lo + tq - 1
    qpos_i = q_lo + t_idx
    qpos_st = jnp.concatenate([qpos_i] * R, axis=0)
    qpos_f = qpos_st.astype(F32)
    kidx = lax.broadcasted_iota(I32, (1, KEY_TILE), 1)
    blk_t = lax.broadcasted_iota(I32, (nsp, tq), 0)
    qpos_t = q_lo + lax.broadcasted_iota(I32, (nsp, tq), 1)
    place = (lax.broadcasted_iota(I32, (nsp, LANES), 1) ==
             lax.broadcasted_iota(I32, (nsp, LANES), 0) + AUG_BLK_LANE0).astype(BF16)
    sel_lanes = (lane_q >= AUG_BLK_LANE0) & (lane_q < AUG_BLK_LANE0 + ns)

    def flash_init():
        m_ref[...] = jnp.full(m_ref.shape, NEG_INF, F32)
        l_ref[...] = jnp.zeros(l_ref.shape, F32)
        acc_ref[...] = jnp.zeros(acc_ref.shape, F32)

    def flash_tile(kb, vb, mask_fn):
        for c in range(n_chunk):
            rs = pl.ds(c * chunk, chunk)
            s = lax.dot_general(qa_ref[rs, :], kb, (((1,), (1,)), ((), ())), preferred_element_type=F32)
            if mask_fn is not None:
                s = jnp.where(mask_fn(qpos_st[c * chunk:(c + 1) * chunk]), s, NEG_INF)
            m_old = m_ref[rs, :]
            m_new = jnp.maximum(m_old, jnp.max(s, axis=-1, keepdims=True))
            alpha = jnp.exp(m_old - m_new)
            e = jnp.exp(s - m_new)
            l_ref[rs, :] = alpha * l_ref[rs, :] + jnp.sum(e, axis=-1, keepdims=True)
            acc_ref[rs, :] = alpha * acc_ref[rs, :] + jnp.dot(e.astype(BF16), vb, preferred_element_type=F32)
            m_ref[rs, :] = m_new

    def flash_done():
        return acc_ref[...] / l_ref[...]

    for g in range(B_KV_HEADS):
        gl = slice(g * LANES, (g + 1) * LANES)
        c0 = q_ref[0, :, g * 2 * LANES:g * 2 * LANES + LANES]
        c1 = q_ref[0, :, g * 2 * LANES + LANES:(g + 1) * 2 * LANES]
        q_st = jnp.concatenate([jnp.where(lo_q, c0, 0.0), jnp.where(lo_q, 0.0, c0),
                                jnp.where(lo_q, c1, 0.0), jnp.where(lo_q, 0.0, c1)], axis=0).astype(BF16)
        slopes = [2.0 ** (-8.0 * (g * R + r + 1) / B_HEADS) for r in range(R)]
        slope = jnp.concatenate([jnp.full((tq, 1), s, F32) for s in slopes], axis=0)

        ckb = ck_ref[0, g].astype(BF16)
        cvb = cv_ref[0, g].astype(BF16)
        s = lax.dot_general(q_st, ckb, (((1,), (1,)), ((), ())), preferred_element_type=F32)
        c_iota = lax.broadcasted_iota(I32, (1, ncp), 1)
        c_end = (c_iota * CMP_STRIDE + (CMP_BLOCK - 1)).astype(F32)
        dist = qpos_f - c_end
        mask = (dist >= 0.0) & (c_iota < nc)
        s = jnp.where(mask, s - slope * dist, NEG_INF)
        m = jnp.max(s, axis=-1, keepdims=True)
        e = jnp.where(mask, jnp.exp(s - m), 0.0)
        p = e / jnp.maximum(jnp.sum(e, axis=-1, keepdims=True), 1e-30)
        o_cmp = jnp.dot(p.astype(BF16), cvb, preferred_element_type=F32)

        p_sum = p[0:tq] + p[tq:2 * tq] + p[2 * tq:3 * tq] + p[3 * tq:4 * tq]
        imp_t = lax.dot_general(cmapt_ref[...], p_sum, (((1,), (1,)), ((), ())), precision=HIGHEST,
                                preferred_element_type=F32)[0:nsp]
        cur = qpos_t // SLC_BLOCK
        forced = (blk_t == 0) | (blk_t == cur) | (blk_t == cur - 1)
        score = jnp.where(forced, FORCE_SCORE, jnp.where(blk_t * SLC_BLOCK <= qpos_t, imp_t, -1.0))
        score = jnp.where(blk_t < ns, score, -jnp.inf)
        sc_ref[...] = score
        rank = jnp.zeros((nsp, tq), F32)
        for j in range(ns):
            other = sc_ref[pl.ds(j, 1), :]
            ahead = (other > score) | ((other == score) & (j < blk_t))
            rank = rank + ahead.astype(F32)
        sel_t = ((rank < float(n_sel)) & (score >= 0.0)).astype(BF16)
        sel_q = lax.dot_general(sel_t, place, (((0,), (0,)), ((), ())), preferred_element_type=F32)
        sel_term = jnp.where(sel_q > 0.5, 0.0, UNSELECTED)

        q_hi_half = []
        for r in range(R):
            s_hi, s_lo = _split_bf16(slopes[r])
            consts = jnp.where(lane_q < 2, s_hi, jnp.where(lane_q < AUG_POS_LANES, s_lo, 0.0))
            q_hi_half.append(jnp.where(sel_lanes, sel_term, consts))
        qa_ref[...] = jnp.concatenate([q_st, jnp.concatenate(q_hi_half, axis=0).astype(BF16)], axis=1)

        n_kt = jnp.minimum(q_hi // KEY_TILE + 1, n_slc_tiles)
        n_full = jnp.minimum((q_lo + 1) // KEY_TILE, n_kt)

        def slc_tile(kt, masked):
            t0 = pl.multiple_of(kt * KEY_TILE, KEY_TILE)
            kb = jnp.concatenate([kslc_ref[0, pl.ds(t0, KEY_TILE), gl], augs_ref[pl.ds(t0, KEY_TILE), :]], axis=1)
            vb = vslc_ref[0, pl.ds(t0, KEY_TILE), gl]
            flash_tile(kb, vb, (lambda qp: (t0 + kidx) <= qp) if masked else None)

        flash_init()
        lax.fori_loop(0, n_full, lambda kt, c: (slc_tile(kt, False), c)[1], 0)
        lax.fori_loop(n_full, n_kt, lambda kt, c: (slc_tile(kt, True), c)[1], 0)
        o_slc = flash_done()

        w_first = jnp.maximum(q_lo - (WINDOW - 1) - win_pos0, 0) // KEY_TILE
        w_last = jnp.minimum((q_hi - win_pos0) // KEY_TILE + 1, n_win_tiles)

        def win_tile(kt, c):
            t0 = pl.multiple_of(kt * KEY_TILE, KEY_TILE)
            kb = jnp.concatenate([kwin_ref[0, pl.ds(t0, KEY_TILE), gl], augw_ref[pl.ds(t0, KEY_TILE), :]], axis=1)
            vb = vwin_ref[0, pl.ds(t0, KEY_TILE), gl]
            kpos = win_pos0 + t0 + kidx
            flash_tile(kb, vb, lambda qp: (kpos <= qp) & (kpos > qp - WINDOW))
            return c

        flash_init()
        lax.fori_loop(w_first, w_last, win_tile, 0)
        o_win = flash_done()

        def gate_col(j):
            base = j * B_HEADS + g * R
            return jnp.concatenate([gate_ref[0, :, base + r:base + r + 1] for r in range(R)], axis=0)

        o_st = gate_col(0) * o_cmp + gate_col(1) * o_slc + gate_col(2) * o_win
        o_ref[0, :, g * 2 * LANES:g * 2 * LANES + LANES] = jnp.where(lo_q, o_st[0:tq], o_st[tq:2 * tq])
        o_ref[0, :, g * 2 * LANES + LANES:(g + 1) * 2 * LANES] = jnp.where(lo_q, o_st[2 * tq:3 * tq], o_st[3 * tq:])


def _nsa_attn(q, gates, ck, cv, kv_arrays, kv_cols, *, pos0, win_pos0, nc, ns):
    B, T, _ = q.shape
    tq = min(T, LANES)
    assert T % tq == 0
    ncp = ck.shape[2]
    n_sel = min(N_SELECT, ns)
    nsp = -(-ns // (2 * SUBLANES)) * (2 * SUBLANES)
    Ls, Lw = kv_arrays[0].shape[1], kv_arrays[2].shape[1]
    assert Ls % KEY_TILE == 0 and Lw % KEY_TILE == 0
    rows = B_GROUP * tq
    width = B_KV_HEADS * LANES
    kern = functools.partial(_nsa_attn_kernel, tq=tq, pos0=pos0, win_pos0=win_pos0, nc=nc, ns=ns, n_sel=n_sel)
    kv_specs = [pl.BlockSpec((1, a.shape[1], width), lambda b, i, c=c: (b, 0, c))
                for a, c in zip(kv_arrays, kv_cols)]
    return pl.pallas_call(
        kern,
        grid=(B, T // tq),
        in_specs=[pl.BlockSpec((1, tq, D_MODEL), lambda b, i: (b, i, 0)),
                  pl.BlockSpec((1, tq, LANES), lambda b, i: (b, i, 0)),
                  pl.BlockSpec((1, B_KV_HEADS, ncp, LANES), lambda b, i: (b, 0, 0, 0)),
                  pl.BlockSpec((1, B_KV_HEADS, ncp, LANES), lambda b, i: (b, 0, 0, 0)),
                  pl.BlockSpec((LANES, ncp), lambda b, i: (0, 0))] + kv_specs +
                 [pl.BlockSpec((Ls, LANES), lambda b, i: (0, 0)),
                  pl.BlockSpec((Lw, LANES), lambda b, i: (0, 0))],
        out_specs=pl.BlockSpec((1, tq, D_MODEL), lambda b, i: (b, i, 0)),
        out_shape=jax.ShapeDtypeStruct((B, T, D_MODEL), F32),
        scratch_shapes=[pltpu.VMEM((rows, 2 * LANES), BF16), pltpu.VMEM((rows, 1), F32),
                        pltpu.VMEM((rows, 1), F32), pltpu.VMEM((rows, LANES), F32),
                        pltpu.VMEM((nsp, tq), F32)],
        compiler_params=_cparams("arbitrary", "arbitrary"),
        name="nsa_attn",
    )(q, gates, ck, cv, _cmp_to_slc_map_t(ncp, ns), *kv_arrays,
      _aug_table(Ls, 0, True), _aug_table(Lw, win_pos0, False))


def _deinterleave_kernel(w_ref, perm_ref, glu_ref, lin_ref):
    for c in range(w_ref.shape[1] // MXU_DIM):
        t = jnp.dot(w_ref[:, c * MXU_DIM:(c + 1) * MXU_DIM].astype(BF16), perm_ref[...],
                    preferred_element_type=F32)
        glu_ref[:, c * LANES:(c + 1) * LANES] = t[:, :LANES].astype(BF16)
        lin_ref[:, c * LANES:(c + 1) * LANES] = t[:, LANES:].astype(BF16)


def _deinterleave_up(w_up):
    E, K, N2 = w_up.shape
    rt = 512
    i = np.arange(MXU_DIM)[:, None]
    j = np.arange(MXU_DIM)[None, :]
    perm = jnp.asarray(((i % 2) * LANES + i // 2) == j, dtype=BF16)
    return pl.pallas_call(
        _deinterleave_kernel,
        grid=(E, K // rt),
        in_specs=[pl.BlockSpec((None, rt, N2), lambda e, r: (e, r, 0)),
                  pl.BlockSpec((MXU_DIM, MXU_DIM), lambda e, r: (0, 0))],
        out_specs=[pl.BlockSpec((None, rt, N2 // 2), lambda e, r: (e, r, 0))] * 2,
        out_shape=[jax.ShapeDtypeStruct((E, K, N2 // 2), BF16)] * 2,
        compiler_params=_cparams("arbitrary", "arbitrary"),
        name="deinterleave_up",
    )(w_up, perm)


def _route_kernel(te_ref, start_ref, pos_ref, cnt_ref):
    rt = te_ref.shape[0]

    @pl.when(pl.program_id(0) == 0)
    def _():
        cnt_ref[...] = start_ref[...]

    te = te_ref[...]
    lane = lax.broadcasted_iota(I32, (rt, LANES), 1)
    onehots = [(lane == te[:, k:k + 1]).astype(F32) for k in range(TOP_K)]
    total = onehots[0] + onehots[1] + onehots[2] + onehots[3]
    r = lax.broadcasted_iota(I32, (rt, rt), 0)
    c = lax.broadcasted_iota(I32, (rt, rt), 1)
    earlier = (c < r).astype(BF16)
    base = cnt_ref[...] + jnp.dot(earlier, total.astype(BF16), preferred_element_type=F32)
    pos = jnp.zeros((rt, LANES), I32)
    for k in range(TOP_K):
        pk = jnp.sum(onehots[k] * base, axis=-1, keepdims=True)
        pos = jnp.where(lane == k, pk.astype(I32), pos)
    pos_ref[...] = pos
    cnt_ref[...] = cnt_ref[...] + jnp.sum(total, axis=0, keepdims=True)


def _route(te, start):
    N = te.shape[0]
    rt = _largest_divisor(N, (256, 128, 64, 32, 16, 8))
    return pl.pallas_call(
        _route_kernel,
        grid=(N // rt,),
        in_specs=[pl.BlockSpec((rt, LANES), lambda i: (i, 0)), pl.BlockSpec((1, LANES), lambda i: (0, 0))],
        out_specs=pl.BlockSpec((rt, LANES), lambda i: (i, 0)),
        out_shape=jax.ShapeDtypeStruct((N, LANES), I32),
        scratch_shapes=[pltpu.VMEM((1, LANES), F32)],
        compiler_params=_cparams("arbitrary"),
        name="moe_route",
    )(te, start)


def _row_copy_wait(src_rows, dst_rows, sem):
    pltpu.make_async_copy(src_rows, dst_rows, sem).wait()


def _dispatch_kernel(pos_ref, x_ref, xr_in_ref, xr_ref, sem):
    del xr_in_ref
    rt = x_ref.shape[0]

    def body(i, carry):
        for k in range(TOP_K):
            dst = pos_ref[i * TOP_K + k]
            pltpu.make_async_copy(x_ref.at[pl.ds(i, 1)], xr_ref.at[pl.ds(dst, 1)], sem).start()
        return carry

    lax.fori_loop(0, rt, body, 0)
    for k in range(TOP_K):
        _row_copy_wait(x_ref, xr_ref.at[pl.ds(0, rt)], sem)


def _dispatch(x, pos_flat, m_pad):
    N, D = x.shape
    rt = _largest_divisor(N, (256, 128, 64, 32, 16, 8))
    return pl.pallas_call(
        _dispatch_kernel,
        grid=(N // rt,),
        in_specs=[pl.BlockSpec((rt * TOP_K,), lambda i: (i,), memory_space=pltpu.SMEM),
                  pl.BlockSpec((rt, D), lambda i: (i, 0)),
                  pl.BlockSpec(memory_space=pl.ANY)],
        out_specs=pl.BlockSpec(memory_space=pl.ANY),
        out_shape=jax.ShapeDtypeStruct((m_pad, D), F32),
        scratch_shapes=[pltpu.SemaphoreType.DMA(())],
        input_output_aliases={2: 0},
        compiler_params=_cparams("arbitrary"),
        name="moe_dispatch",
    )(pos_flat, x, jnp.zeros((m_pad, D), F32))


def _moe_kernel(be_ref, nv_ref, x_ref, wg_ref, wl_ref, bg_ref, bl_ref, wd_ref, bd_ref, o_ref):
    i = pl.program_id(0)

    @pl.when(i < nv_ref[0])
    def _():
        x = x_ref[...].astype(BF16)
        glu = jnp.dot(x, wg_ref[...], preferred_element_type=F32) + bg_ref[...]
        lin = jnp.dot(x, wl_ref[...], preferred_element_type=F32) + bl_ref[...]
        glu = jnp.minimum(glu, SWIGLU_LIMIT)
        lin = jnp.clip(lin, -SWIGLU_LIMIT, SWIGLU_LIMIT)
        act = glu * _sigmoid(SWIGLU_ALPHA * glu) * (lin + 1.0)
        o_ref[...] = jnp.dot(act.astype(BF16), wd_ref[...], preferred_element_type=F32) + bd_ref[...]

    @pl.when(i >= nv_ref[0])
    def _():
        o_ref[...] = jnp.zeros_like(o_ref)


def _moe_ffn_blocks(xr, blk_exp, nvalid, wg, wl, bg, bl, wd, bd):
    m_pad = xr.shape[0]
    n_blk = m_pad // MOE_ROWS
    wspec = lambda cols: pl.BlockSpec((None, D_MODEL, cols), lambda i, be, nv: (be[i], 0, 0))
    bspec = lambda cols: pl.BlockSpec((None, 1, cols), lambda i, be, nv: (be[i], 0, 0))
    grid_spec = pltpu.PrefetchScalarGridSpec(
        num_scalar_prefetch=2,
        grid=(n_blk,),
        in_specs=[pl.BlockSpec((MOE_ROWS, D_MODEL), lambda i, be, nv: (i, 0)),
                  wspec(D_FF), wspec(D_FF), bspec(D_FF), bspec(D_FF), wspec(D_MODEL), bspec(D_MODEL)],
        out_specs=pl.BlockSpec((MOE_ROWS, D_MODEL), lambda i, be, nv: (i, 0)),
    )
    return pl.pallas_call(
        _moe_kernel,
        grid_spec=grid_spec,
        out_shape=jax.ShapeDtypeStruct((m_pad, D_MODEL), F32),
        compiler_params=_cparams("arbitrary"),
        name="moe_ffn",
    )(blk_exp, nvalid, xr, wg, wl, bg, bl, wd, bd)


def _combine_kernel(pos_ref, x_ref, tg_ref, gf_ref, lng_ref, lnb_ref, yr_ref, xn_ref, ybuf, sem):
    sb, tt, d = x_ref.shape
    rows = sb * tt

    def body(i, carry):
        for k in range(TOP_K):
            src = pos_ref[i * TOP_K + k]
            pltpu.make_async_copy(yr_ref.at[pl.ds(src, 1)], ybuf.at[k, pl.ds(i, 1)], sem).start()
        return carry

    lax.fori_loop(0, rows, body, 0)
    for k in range(TOP_K):
        _row_copy_wait(yr_ref.at[pl.ds(0, rows)], ybuf.at[k], sem)
    tg = tg_ref[...].reshape(rows, LANES)
    y = tg[:, 0:1] * ybuf[0]
    for k in range(1, TOP_K):
        y = y + tg[:, k:k + 1] * ybuf[k]
    z = DN_ALPHA * x_ref[...] + (1.0 + gf_ref[...]) * y.reshape(sb, tt, d)
    xn_ref[...] = _layer_norm_rows(z, lng_ref[...], lnb_ref[...])


def _combine(x, yr, pos_flat, tg, mod, ln_g, ln_b):
    B, T, _ = x.shape
    sb, tt = _row_tiles(B, T)
    rows = sb * tt
    nj = T // tt
    vec = pl.BlockSpec((1, 1, D_MODEL), lambda i, j: (0, 0, 0))
    return pl.pallas_call(
        _combine_kernel,
        grid=(B // sb, nj),
        in_specs=[pl.BlockSpec((rows * TOP_K,), lambda i, j: (i * nj + j,), memory_space=pltpu.SMEM),
                  _tok_spec(sb, tt), _tok_spec(sb, tt, LANES), _mod_spec(sb, 5), vec, vec,
                  pl.BlockSpec(memory_space=pl.ANY)],
        out_specs=_tok_spec(sb, tt),
        out_shape=jax.ShapeDtypeStruct((B, T, D_MODEL), F32),
        scratch_shapes=[pltpu.VMEM((TOP_K, rows, D_MODEL), F32), pltpu.SemaphoreType.DMA(())],
        compiler_params=_cparams("arbitrary", "arbitrary"),
        name="moe_combine",
    )(pos_flat, x, tg, mod, ln_g.reshape(1, 1, D_MODEL), ln_b.reshape(1, 1, D_MODEL), yr)


def _moe_plan(te_all):
    N = te_all.shape[0]
    M = N * TOP_K
    onehot = (te_all[:, :TOP_K, None] == jnp.arange(N_EXPERTS, dtype=I32)[None, None, :])
    counts = jnp.sum(onehot.astype(I32), axis=(0, 1))
    padded = (counts + MOE_ROWS - 1) // MOE_ROWS * MOE_ROWS
    pend = jnp.cumsum(padded)
    pstart = pend - padded
    n_blk = -(-(M + N_EXPERTS * (MOE_ROWS - 1)) // MOE_ROWS)
    blk_first = jnp.arange(n_blk, dtype=I32)[:, None] * MOE_ROWS
    blk_exp = jnp.minimum(jnp.sum((pend[None, :] <= blk_first).astype(I32), axis=1), N_EXPERTS - 1).astype(I32)
    nvalid = (pend[-1] // MOE_ROWS).astype(I32).reshape(1)
    start = jnp.pad(pstart.astype(F32), (0, LANES - N_EXPERTS)).reshape(1, LANES)
    return start, blk_exp, nvalid, n_blk * MOE_ROWS


def _compress_inputs(cmp_raw):
    B, L, _ = cmp_raw.shape
    nch = L // CMP_STRIDE
    x = cmp_raw[:, :nch * CMP_STRIDE].reshape(B, nch, CMP_STRIDE, 2, B_KV_HEADS, B_DH)
    x = x.transpose(3, 0, 4, 1, 2, 5)
    return x.reshape(2, B * B_KV_HEADS, nch, CMP_STRIDE * B_DH), nch


def _pad_rows(a, mult, axis=1):
    n = a.shape[axis]
    pad = (-n) % mult
    if pad == 0:
        return a
    widths = [(0, 0)] * a.ndim
    widths[axis] = (0, pad)
    return jnp.pad(a, widths)


def kernel(x_prompt, x_sample, state_hgrn, cache_cmp_kv, cache_slc_kv, state_win_kv, page_table, c_prompt, c_sample,
           ada_w, ada_b, ln_g, ln_b, a_w_in, a_lb, a_onorm_g, a_w_o, kv_ada_w, kv_ada_b, kv_w, cmp_pos, cmp_w1, cmp_b1,
           cmp_w2, cmp_b2, b_w_in, b_w_o, moe_wr, moe_br, moe_w_up, moe_b_up, moe_w_down, moe_b_down):
    Bp, Tp, _ = x_prompt.shape
    Bs, Ts, _ = x_sample.shape
    n_pages = page_table.shape[1]
    past_len = n_pages * PAGE_SIZE
    keep_s = state_win_kv.shape[1]
    G, DH = B_KV_HEADS, B_DH
    GD = G * DH
    Np, Ns = Bp * Tp, Bs * Ts
    assert keep_s % PAGE_SIZE == 0

    a_w_in_bf = a_w_in.astype(BF16)
    a_w_o_bf = a_w_o.astype(BF16)
    b_w_o_bf = b_w_o.astype(BF16)
    dup_cols = [jnp.concatenate([kv_w[:, c * GD + g * DH:c * GD + (g + 1) * DH]] * 2, axis=1)
                for c in range(2, 6) for g in range(G)]
    kv_w_bf = jnp.concatenate([kv_w] + dup_cols, axis=1).astype(BF16)
    wq_bf = b_w_in[:, :, :D_MODEL].astype(BF16)
    n_l = b_w_in.shape[0]
    wg = b_w_in[:, :, D_MODEL:].reshape(n_l, D_MODEL, B_HEADS, 3).transpose(0, 1, 3, 2).reshape(n_l, D_MODEL, 3 * B_HEADS)
    wg_bf = jnp.pad(wg, ((0, 0), (0, 0), (0, LANES - 3 * B_HEADS))).astype(BF16)
    wr_pad = jnp.pad(moe_wr, ((0, 0), (0, 0), (0, LANES - N_EXPERTS)))
    br_pad = jnp.pad(moe_br, ((0, 0), (0, LANES - N_EXPERTS)), constant_values=NEG_INF).reshape(DEPTH, 1, LANES)
    w_glu, w_lin = _deinterleave_up(moe_w_up.reshape(DEPTH * N_EXPERTS, D_MODEL, 2 * D_FF))
    w_glu = w_glu.reshape(DEPTH, N_EXPERTS, D_MODEL, D_FF)
    w_lin = w_lin.reshape(DEPTH, N_EXPERTS, D_MODEL, D_FF)
    b_glu = moe_b_up[..., 0::2].reshape(DEPTH, N_EXPERTS, 1, D_FF)
    b_lin = moe_b_up[..., 1::2].reshape(DEPTH, N_EXPERTS, 1, D_FF)
    w_dn = moe_w_down.astype(BF16)
    b_dn = moe_b_down.reshape(DEPTH, N_EXPERTS, 1, D_MODEL)
    lb_p = jax.nn.softmax(a_lb.astype(F32), axis=0)
    lb_all = jnp.maximum(jnp.cumsum(lb_p, axis=0) - lb_p[0], 0.0)
    w1cat = jnp.concatenate([cmp_w1[:, :CMP_STRIDE * DH], cmp_w1[:, CMP_STRIDE * DH:]], axis=-1)
    cvec = _cmp_const(cmp_pos.reshape(2, 1, CMP_BLOCK * DH), cmp_w1, cmp_b1.reshape(2, 1, CMP_HID))
    w2dup = jnp.concatenate([cmp_w2, cmp_w2], axis=-1)
    b2dup = jnp.concatenate([cmp_b2, cmp_b2], axis=-1).reshape(2, 1, LANES)

    n_c = Bp + Bs
    c_all = _pad_rows(jnp.concatenate([c_prompt, c_sample], axis=0), SUBLANES, axis=0)
    mod_all = _silu_linear(c_all, ada_w, ada_b)
    kvmod_all = _silu_linear(c_all, kv_ada_w[None], kv_ada_b[None])[0]
    mods = [(mod_all[l, :Bp].reshape(Bp, 1, 6 * D_MODEL), mod_all[l, Bp:n_c].reshape(Bs, 1, 6 * D_MODEL))
            for l in range(DEPTH)]
    kvmods = (kvmod_all[:Bp].reshape(Bp, 1, 2 * D_MODEL), kvmod_all[Bp:n_c].reshape(Bs, 1, 2 * D_MODEL))

    xs = [x_prompt, x_sample]
    s0s = [jnp.zeros((N_A_LAYERS, Bp, A_HEADS, A_DK, A_DV), F32), state_hgrn]
    hgrn_out = [[], []]
    ctx = [None, None]
    new_kv = [None, None]

    def build_ctx(t, x):
        B, T, _ = x.shape
        if t == 0:
            kv, dup = _kv_proj(x, kvmods[t], kv_w_bf, BF16)
            cmp_raw = kv[:, :, :2 * GD]
            if T % KEY_TILE:
                dup = _pad_rows(dup, KEY_TILE)
            kv_arrays, kv_cols = [dup] * 4, [0, 1, 2, 3]
            L, pos0, win_pos0 = T, 0, 0
            win_out = kv[:, T - min(WINDOW, T):, 4 * GD:]
        else:
            kv, dup = _kv_proj(x, kvmods[t], kv_w_bf, F32)
            pool_cmp = cache_cmp_kv.reshape(-1, PAGE_SIZE, 2 * GD)
            past_cmp = pool_cmp[page_table].reshape(B, past_len, 2 * GD)
            cmp_raw = jnp.concatenate([past_cmp, kv[:, :, :2 * GD]], axis=1)
            L = past_len + T
            ls_pad = -(-L // KEY_TILE) * KEY_TILE
            lw_pad = -(-(keep_s + T) // KEY_TILE) * KEY_TILE
            kslc, vslc = _pages_to_dup(cache_slc_kv.reshape(-1, PAGE_SIZE, 2 * GD), page_table, dup, 0, 1, ls_pad)
            n_wp = keep_s // PAGE_SIZE
            win_table = jnp.arange(B * n_wp, dtype=I32).reshape(B, n_wp)
            kwin, vwin = _pages_to_dup(state_win_kv.reshape(B * n_wp, PAGE_SIZE, 2 * GD), win_table, dup, 2, 3, lw_pad)
            kv_arrays, kv_cols = [kslc, vslc, kwin, vwin], [0, 0, 0, 0]
            pos0 = past_len
            win_pos0 = past_len - keep_s
            win_all = jnp.concatenate([state_win_kv.reshape(B, keep_s, 2 * GD), kv[:, :, 4 * GD:]], axis=1)
            win_out = win_all[:, -keep_s:]
        xc, nch = _compress_inputs(cmp_raw)
        nc = nch - CMP_RATIO + 1
        ns = -(-L // SLC_BLOCK)
        ckv = _compress(xc, w1cat, cvec, w2dup, b2dup)
        ckv = _pad_rows(ckv.reshape(2, B, G, nch, LANES), LANES, axis=3)
        info = dict(ck=ckv[0], cv=ckv[1], kv_arrays=kv_arrays, kv_cols=kv_cols,
                    pos0=pos0, win_pos0=win_pos0, nc=nc, ns=ns)
        outs = (kv[:, :, :2 * GD].reshape(B, T, 2, G, DH), kv[:, :, 2 * GD:4 * GD].reshape(B, T, 2, G, DH),
                win_out.reshape(B, -1, 2, G, DH))
        return info, outs

    for l in range(DEPTH):
        o_mix = []
        for t in range(2):
            x = xs[t]
            mod = mods[l][t]
            if l < N_A_LAYERS:
                q, k, g, v, og = _hgrn_in(x, mod, a_w_in_bf[l], lb_all[l])
                o, S = _hgrn_scan(q, k, g, v, og, s0s[t][l], a_onorm_g[l])
                hgrn_out[t].append(S)
            else:
                j = l - N_A_LAYERS
                if ctx[t] is None:
                    ctx[t], new_kv[t] = build_ctx(t, x)
                c = ctx[t]
                q, gates = _nsa_in(x, mod, wq_bf[j], wg_bf[j])
                o = _nsa_attn(q, gates, c['ck'], c['cv'], c['kv_arrays'], c['kv_cols'],
                              pos0=c['pos0'], win_pos0=c['win_pos0'], nc=c['nc'], ns=c['ns'])
            o_mix.append(o)
        w_o = a_w_o_bf[l] if l < N_A_LAYERS else b_w_o_bf[l - N_A_LAYERS]
        hf, te, tg = [], [], []
        for t in range(2):
            xn, h, e, gt = _mixer_out(o_mix[t], w_o, xs[t], mods[l][t], ln_g[l, 0], ln_b[l, 0], wr_pad[l], br_pad[l])
            xs[t] = xn
            hf.append(h.reshape(-1, D_MODEL))
            te.append(e.reshape(-1, LANES))
            tg.append(gt)
        te_all = jnp.concatenate(te, axis=0)
        start, blk_exp, nvalid, m_pad = _moe_plan(te_all)
        pos_flat = _route(te_all, start)[:, :TOP_K].reshape(-1)
        xr = _dispatch(jnp.concatenate(hf, axis=0), pos_flat, m_pad)
        yr = _moe_ffn_blocks(xr, blk_exp, nvalid, w_glu[l], w_lin[l], b_glu[l], b_lin[l], w_dn[l], b_dn[l])
        pos_t = [pos_flat[:Np * TOP_K], pos_flat[Np * TOP_K:]]
        for t in range(2):
            xs[t] = _combine(xs[t], yr, pos_t[t], tg[t], mods[l][t], ln_g[l, 1], ln_b[l, 1])

    return (xs[0], xs[1], jnp.stack(hgrn_out[0], axis=0), jnp.stack(hgrn_out[1], axis=0),
            new_kv[0][0], new_kv[1][0], new_kv[0][1], new_kv[1][1], new_kv[0][2], new_kv[1][2])
```

```python
import functools

import numpy as np
import jax
import jax.numpy as jnp
from jax import lax
from jax.experimental import pallas as pl
from jax.experimental.pallas import tpu as pltpu

F32 = jnp.float32
BF16 = jnp.bfloat16
I32 = jnp.int32
HIGHEST = lax.Precision.HIGHEST

D_MODEL = 1024
DEPTH = 4
PAGE_SIZE = 128
N_A_LAYERS = DEPTH // 2
A_HEADS = 8
A_DK = 128
A_DV = D_MODEL // A_HEADS
A_DF = A_HEADS * A_DK
HGRN_CHUNK = 32
B_HEADS = 16
B_DH = D_MODEL // B_HEADS
B_KV_HEADS = 4
B_GROUP = B_HEADS // B_KV_HEADS
CMP_BLOCK = 32
CMP_STRIDE = 16
CMP_RATIO = CMP_BLOCK // CMP_STRIDE
CMP_HID = 2 * B_DH
SLC_BLOCK = 64
N_SELECT = 16
WINDOW = 512
FORCE_SCORE = 1e4
N_EXPERTS = 32
TOP_K = 4
D_FF = D_MODEL
SWIGLU_LIMIT = 7.0
SWIGLU_ALPHA = 1.702
DN_ALPHA = (2 * DEPTH) ** 0.25
LN_EPS = 1e-5
RMS_EPS = 1e-6
NEG_INF = -1e30

LANES = 128
SUBLANES = 8
MXU_DIM = 256
ROW_TILE = 256
MOE_ROWS = 256
KEY_TILE = 256
VMEM_LIMIT = 56 * 1024 * 1024

AUG_POS_LANES = 4
AUG_BLK_LANE0 = 8
UNSELECTED = -(2.0 ** 30)


def _cparams(*sem):
    return pltpu.CompilerParams(dimension_semantics=sem, vmem_limit_bytes=VMEM_LIMIT)


def _row_tiles(B, T):
    if T >= ROW_TILE:
        assert T % ROW_TILE == 0
        return 1, ROW_TILE
    sb = max(1, ROW_TILE // T)
    while B % sb:
        sb //= 2
    return sb, T


def _largest_divisor(n, candidates):
    for c in candidates:
        if n % c == 0:
            return c
    raise ValueError(n)


def _sigmoid(x):
    return 1.0 / (1.0 + jnp.exp(-x))


def _linear_kernel(x_ref, w_ref, b_ref, o_ref):
    x = x_ref[...]
    x = x * _sigmoid(x)
    o_ref[...] = jnp.dot(x.astype(BF16), w_ref[...].astype(BF16), preferred_element_type=F32) + b_ref[...]


def _silu_linear(x, w, b):
    L, K, N = w.shape
    M = x.shape[0]
    tn = 1024
    return pl.pallas_call(
        _linear_kernel,
        grid=(L, N // tn),
        in_specs=[pl.BlockSpec((M, K), lambda l, j: (0, 0)),
                  pl.BlockSpec((None, K, tn), lambda l, j: (l, 0, j)),
                  pl.BlockSpec((None, 1, tn), lambda l, j: (l, 0, j))],
        out_specs=pl.BlockSpec((None, M, tn), lambda l, j: (l, 0, j)),
        out_shape=jax.ShapeDtypeStruct((L, M, N), F32),
        compiler_params=_cparams("arbitrary", "arbitrary"),
        name="silu_linear",
    )(x, w, b.reshape(L, 1, N))


def _modulate(x_ref, sc_ref, sh_ref):
    sb, tt, d = x_ref.shape
    h = x_ref[...] * (1.0 + sc_ref[...]) + sh_ref[...]
    return h.reshape(sb * tt, d)


def _hgrn_in_kernel(x_ref, sh_ref, sc_ref, w_ref, lb_ref, q_ref, k_ref, g_ref, v_ref, og_ref):
    sb, tt, d = x_ref.shape
    h = _modulate(x_ref, sc_ref, sh_ref).astype(BF16)
    proj = jnp.dot(h, w_ref[...], preferred_element_type=F32)
    qz = proj[:, :A_DF]
    fz = proj[:, A_DF:2 * A_DF]
    lb = lb_ref[...]
    log_sig = jnp.minimum(fz, 0.0) - jnp.log1p(jnp.exp(-jnp.abs(fz)))
    pos = lb > 0.0
    log_lb = jnp.where(pos, jnp.log(jnp.where(pos, lb, 1.0)), NEG_INF)
    other = jnp.log1p(-lb) + log_sig
    logf = jnp.maximum(log_lb, other) + jnp.log1p(jnp.exp(-jnp.abs(log_lb - other)))
    q_ref[...] = (qz * _sigmoid(qz)).reshape(sb, tt, A_DF)
    g_ref[...] = logf.reshape(sb, tt, A_DF)
    k_ref[...] = ((1.0 - lb) * _sigmoid(-fz)).reshape(sb, tt, A_DF)
    v_ref[...] = proj[:, 2 * A_DF:2 * A_DF + D_MODEL].reshape(sb, tt, D_MODEL)
    og_ref[...] = proj[:, 2 * A_DF + D_MODEL:].reshape(sb, tt, D_MODEL)


def _mod_spec(sb, comp):
    return pl.BlockSpec((sb, 1, D_MODEL), lambda i, j, c=comp: (i, 0, c))


def _tok_spec(sb, tt, width=D_MODEL):
    return pl.BlockSpec((sb, tt, width), lambda i, j: (i, j, 0))


def _hgrn_in(x, mod, w_bf, lb):
    B, T, _ = x.shape
    sb, tt = _row_tiles(B, T)
    n_out = w_bf.shape[1]
    shp = jax.ShapeDtypeStruct((B, T, D_MODEL), F32)
    return pl.pallas_call(
        _hgrn_in_kernel,
        grid=(B // sb, T // tt),
        in_specs=[_tok_spec(sb, tt), _mod_spec(sb, 0), _mod_spec(sb, 1),
                  pl.BlockSpec((D_MODEL, n_out), lambda i, j: (0, 0)),
                  pl.BlockSpec((1, A_DF), lambda i, j: (0, 0))],
        out_specs=[_tok_spec(sb, tt)] * 5,
        out_shape=[shp] * 5,
        compiler_params=_cparams("arbitrary", "arbitrary"),
        name="hgrn_in",
    )(x, mod, mod, w_bf, lb.reshape(1, A_DF))


def _nsa_in_kernel(x_ref, sh_ref, sc_ref, wq_ref, wg_ref, q_ref, gate_ref):
    sb, tt, d = x_ref.shape
    h = _modulate(x_ref, sc_ref, sh_ref).astype(BF16)
    q = jnp.dot(h, wq_ref[...], preferred_element_type=F32) * (B_DH ** -0.5)
    gz = jnp.dot(h, wg_ref[...], preferred_element_type=F32)
    q_ref[...] = q.reshape(sb, tt, D_MODEL)
    gate_ref[...] = _sigmoid(gz).reshape(sb, tt, LANES)


def _nsa_in(x, mod, wq_bf, wg_bf):
    B, T, _ = x.shape
    sb, tt = _row_tiles(B, T)
    return pl.pallas_call(
        _nsa_in_kernel,
        grid=(B // sb, T // tt),
        in_specs=[_tok_spec(sb, tt), _mod_spec(sb, 0), _mod_spec(sb, 1),
                  pl.BlockSpec((D_MODEL, D_MODEL), lambda i, j: (0, 0)),
                  pl.BlockSpec((D_MODEL, LANES), lambda i, j: (0, 0))],
        out_specs=[_tok_spec(sb, tt), _tok_spec(sb, tt, LANES)],
        out_shape=[jax.ShapeDtypeStruct((B, T, D_MODEL), F32), jax.ShapeDtypeStruct((B, T, LANES), F32)],
        compiler_params=_cparams("arbitrary", "arbitrary"),
        name="nsa_in",
    )(x, mod, mod, wq_bf, wg_bf)


KV_RAW = 6 * B_KV_HEADS * B_DH
KV_DUP = 4 * B_KV_HEADS * LANES


def _kv_proj_kernel(x_ref, sh_ref, sc_ref, w_ref, kv_ref, dup_ref):
    sb, tt, d = x_ref.shape
    h = _modulate(x_ref, sc_ref, sh_ref).astype(BF16)
    kv = jnp.dot(h, w_ref[...], preferred_element_type=F32)
    kv_ref[...] = kv[:, :KV_RAW].reshape(sb, tt, KV_RAW)
    dup_ref[...] = kv[:, KV_RAW:].reshape(sb, tt, KV_DUP).astype(dup_ref.dtype)


def _kv_proj(x, kvmod, w_bf, dup_dtype):
    B, T, _ = x.shape
    sb, tt = _row_tiles(B, T)
    n_out = w_bf.shape[1]
    return pl.pallas_call(
        _kv_proj_kernel,
        grid=(B // sb, T // tt),
        in_specs=[_tok_spec(sb, tt), _mod_spec(sb, 0), _mod_spec(sb, 1),
                  pl.BlockSpec((D_MODEL, n_out), lambda i, j: (0, 0))],
        out_specs=[_tok_spec(sb, tt, KV_RAW), _tok_spec(sb, tt, KV_DUP)],
        out_shape=[jax.ShapeDtypeStruct((B, T, KV_RAW), F32), jax.ShapeDtypeStruct((B, T, KV_DUP), dup_dtype)],
        compiler_params=_cparams("arbitrary", "arbitrary"),
        name="kv_proj",
    )(x, kvmod, kvmod, w_bf)


def _layer_norm_rows(z, g, b):
    mu = jnp.mean(z, axis=-1, keepdims=True)
    zc = z - mu
    var = jnp.mean(zc * zc, axis=-1, keepdims=True)
    return zc * lax.rsqrt(var + LN_EPS) * g + b


def _mixer_out_kernel(o_ref, w_ref, x_ref, gm_ref, shf_ref, scf_ref, lng_ref, lnb_ref, wr_ref, br_ref,
                      xn_ref, hf_ref, te_ref, tg_ref):
    sb, tt, d = x_ref.shape
    rows = sb * tt
    o = o_ref[...].reshape(rows, d).astype(BF16)
    y = jnp.dot(o, w_ref[...], preferred_element_type=F32).reshape(sb, tt, d)
    z = DN_ALPHA * x_ref[...] + (1.0 + gm_ref[...]) * y
    xn = _layer_norm_rows(z, lng_ref[...], lnb_ref[...])
    xn_ref[...] = xn
    hf = xn * (1.0 + scf_ref[...]) + shf_ref[...]
    hf_ref[...] = hf
    logits = jnp.dot(hf.reshape(rows, d), wr_ref[...], precision=HIGHEST,
                     preferred_element_type=F32) + br_ref[...]
    lane = lax.broadcasted_iota(I32, (rows, LANES), 1)
    te = jnp.zeros((rows, LANES), I32)
    tv = jnp.zeros((rows, LANES), F32)
    v0 = None
    denom = None
    for k in range(TOP_K):
        m = jnp.max(logits, axis=-1, keepdims=True)
        idx = jnp.min(jnp.where(logits == m, lane, LANES), axis=-1, keepdims=True)
        if k == 0:
            v0 = m
            e = jnp.ones_like(m)
            denom = e
        else:
            e = jnp.exp(m - v0)
            denom = denom + e
        te = jnp.where(lane == k, idx, te)
        tv = jnp.where(lane == k, e, tv)
        logits = jnp.where(lane == idx, -jnp.inf, logits)
    te_ref[...] = te.reshape(sb, tt, LANES)
    tg_ref[...] = (tv / denom).reshape(sb, tt, LANES)


def _mixer_out(o, w_bf, x, mod, ln_g, ln_b, wr_pad, br_pad):
    B, T, _ = x.shape
    sb, tt = _row_tiles(B, T)
    vec = pl.BlockSpec((1, 1, D_MODEL), lambda i, j: (0, 0, 0))
    return pl.pallas_call(
        _mixer_out_kernel,
        grid=(B // sb, T // tt),
        in_specs=[_tok_spec(sb, tt), pl.BlockSpec((D_MODEL, D_MODEL), lambda i, j: (0, 0)), _tok_spec(sb, tt),
                  _mod_spec(sb, 2), _mod_spec(sb, 3), _mod_spec(sb, 4), vec, vec,
                  pl.BlockSpec((D_MODEL, LANES), lambda i, j: (0, 0)),
                  pl.BlockSpec((1, LANES), lambda i, j: (0, 0))],
        out_specs=[_tok_spec(sb, tt), _tok_spec(sb, tt), _tok_spec(sb, tt, LANES), _tok_spec(sb, tt, LANES)],
        out_shape=[jax.ShapeDtypeStruct((B, T, D_MODEL), F32), jax.ShapeDtypeStruct((B, T, D_MODEL), F32),
                   jax.ShapeDtypeStruct((B, T, LANES), I32), jax.ShapeDtypeStruct((B, T, LANES), F32)],
        compiler_params=_cparams("arbitrary", "arbitrary"),
        name="mixer_out",
    )(o, w_bf, x, mod, mod, mod, ln_g.reshape(1, 1, D_MODEL), ln_b.reshape(1, 1, D_MODEL), wr_pad, br_pad)


def _hgrn_scan_kernel(q_ref, k_ref, g_ref, v_ref, og_ref, s0_ref, gn_ref, o_ref, sout_ref,
                      st_ref, b_ref, k_scr, v_scr, *, chunk, n_chunks):
    tb = pl.program_id(1)
    C = chunk
    nsub = C // SUBLANES

    @pl.when(tb == 0)
    def _():
        for h in range(A_HEADS):
            st_ref[h] = s0_ref[0, h].T

    row = lax.broadcasted_iota(I32, (C, C), 0)
    col = lax.broadcasted_iota(I32, (C, C), 1)
    tri = (row >= col).astype(F32)
    sub_row = lax.broadcasted_iota(I32, (SUBLANES, A_DK), 0)
    gn = gn_ref[...]

    def chunk_body(c, carry):
        r0 = pl.multiple_of(c * C, C)
        for h in range(A_HEADS):
            ls = slice(h * A_DK, (h + 1) * A_DK)
            g = g_ref[0, pl.ds(r0, C), ls]
            b = jnp.dot(tri, g, precision=HIGHEST, preferred_element_type=F32)
            b_ref[...] = b
            q = q_ref[0, pl.ds(r0, C), ls]
            k = k_ref[0, pl.ds(r0, C), ls]
            v = v_ref[0, pl.ds(r0, C), ls]
            k_scr[...] = k
            v_scr[...] = v
            st = st_ref[h]
            o_inter = lax.dot_general((q * jnp.exp(b)).astype(BF16), st.astype(BF16),
                                      (((1,), (1,)), ((), ())), preferred_element_type=F32)
            acc = [jnp.zeros((SUBLANES, A_DV), F32) for _ in range(nsub)]
            for s in range(C):
                j = s // SUBLANES
                b_s = b_ref[pl.ds(s, 1), :]
                k_s = k_scr[pl.ds(s, 1), :]
                v_s = v_scr[pl.ds(s, 1), :]
                for i in range(j, nsub):
                    rs = slice(i * SUBLANES, (i + 1) * SUBLANES)
                    dlt = b[rs] - b_s
                    if i == j:
                        causal = sub_row >= (s % SUBLANES)
                        e = jnp.where(causal, jnp.exp(jnp.where(causal, dlt, 0.0)), 0.0)
                    else:
                        e = jnp.exp(dlt)
                    w = jnp.sum(q[rs] * e * k_s, axis=-1, keepdims=True)
                    acc[i] = acc[i] + w * v_s
            o = o_inter + (jnp.concatenate(acc, axis=0) if nsub > 1 else acc[0])
            b_last = b_ref[pl.ds(C - 1, 1), :]
            kd = k * jnp.exp(b_last - b)
            st_new = st * jnp.exp(b_last) + lax.dot_general(
                v.astype(BF16), kd.astype(BF16), (((0,), (0,)), ((), ())), preferred_element_type=F32)
            st_ref[h] = st_new
            og = og_ref[0, pl.ds(r0, C), ls]
            on = o * lax.rsqrt(jnp.mean(o * o, axis=-1, keepdims=True) + RMS_EPS) * gn
            o_ref[0, pl.ds(r0, C), ls] = on * (og * _sigmoid(og))
        return carry

    lax.fori_loop(0, n_chunks, chunk_body, 0)

    @pl.when(tb == pl.num_programs(1) - 1)
    def _():
        for h in range(A_HEADS):
            sout_ref[0, h] = st_ref[h].T


def _hgrn_scan(q, k, g, v, og, s0, gn):
    B, T, _ = q.shape
    C = HGRN_CHUNK if T % HGRN_CHUNK == 0 else T
    assert C % SUBLANES == 0
    tblk = min(T, ROW_TILE)
    assert T % tblk == 0 and tblk % C == 0
    tok = pl.BlockSpec((1, tblk, D_MODEL), lambda b, t: (b, t, 0))
    st = pl.BlockSpec((1, A_HEADS, A_DK, A_DV), lambda b, t: (b, 0, 0, 0))
    return pl.pallas_call(
        functools.partial(_hgrn_scan_kernel, chunk=C, n_chunks=tblk // C),
        grid=(B, T // tblk),
        in_specs=[tok, tok, tok, tok, tok, st, pl.BlockSpec((1, A_DV), lambda b, t: (0, 0))],
        out_specs=[tok, st],
        out_shape=[jax.ShapeDtypeStruct((B, T, D_MODEL), F32),
                   jax.ShapeDtypeStruct((B, A_HEADS, A_DK, A_DV), F32)],
        scratch_shapes=[pltpu.VMEM((A_HEADS, A_DV, A_DK), F32), pltpu.VMEM((C, A_DK), F32),
                        pltpu.VMEM((C, A_DK), F32), pltpu.VMEM((C, A_DV), F32)],
        compiler_params=_cparams("arbitrary", "arbitrary"),
        name="hgrn_scan",
    )(q, k, g, v, og, s0, gn.reshape(1, A_DV))


def _cmp_const_kernel(pos_ref, w1_ref, b1_ref, o_ref):
    pos8 = jnp.broadcast_to(pos_ref[...], (SUBLANES, pos_ref.shape[1]))
    o_ref[...] = jnp.dot(pos8, w1_ref[...], precision=HIGHEST, preferred_element_type=F32)[0:1] + b1_ref[...]


def _cmp_const(pos_flat, w1, b1):
    kdim = w1.shape[1]
    return pl.pallas_call(
        _cmp_const_kernel,
        grid=(2,),
        in_specs=[pl.BlockSpec((None, 1, kdim), lambda a: (a, 0, 0)),
                  pl.BlockSpec((None, kdim, CMP_HID), lambda a: (a, 0, 0)),
                  pl.BlockSpec((None, 1, CMP_HID), lambda a: (a, 0, 0))],
        out_specs=pl.BlockSpec((None, 1, CMP_HID), lambda a: (a, 0, 0)),
        out_shape=jax.ShapeDtypeStruct((2, 1, CMP_HID), F32),
        compiler_params=_cparams("arbitrary"),
        name="cmp_const",
    )(pos_flat, w1, b1)


def _compress_kernel(x_ref, w1_ref, cvec_ref, w2_ref, b2_ref, o_ref):
    gb, n, kdim = x_ref.shape
    rows = gb * n
    uv = jnp.dot(x_ref[...].reshape(rows, kdim).astype(BF16), w1_ref[...].astype(BF16),
                 preferred_element_type=F32)
    u = uv[:, :CMP_HID]
    v = uv[:, CMP_HID:]
    v_next = pltpu.roll(v, rows - 1, axis=0)
    hid = u + v_next + cvec_ref[...]
    act = hid * _sigmoid(hid)
    out = jnp.dot(act.astype(BF16), w2_ref[...].astype(BF16), preferred_element_type=F32) + b2_ref[...]
    o_ref[...] = out.reshape(gb, n, LANES)


def _compress(xc, w1cat, cvec, w2dup, b2dup):
    _, BG, nch, kdim = xc.shape
    gb = _largest_divisor(BG, (8, 4, 2, 1))
    return pl.pallas_call(
        _compress_kernel,
        grid=(2, BG // gb),
        in_specs=[pl.BlockSpec((None, gb, nch, kdim), lambda a, i: (a, i, 0, 0)),
                  pl.BlockSpec((None, kdim, 2 * CMP_HID), lambda a, i: (a, 0, 0)),
                  pl.BlockSpec((None, 1, CMP_HID), lambda a, i: (a, 0, 0)),
                  pl.BlockSpec((None, CMP_HID, LANES), lambda a, i: (a, 0, 0)),
                  pl.BlockSpec((None, 1, LANES), lambda a, i: (a, 0, 0))],
        out_specs=pl.BlockSpec((None, gb, nch, LANES), lambda a, i: (a, i, 0, 0)),
        out_shape=jax.ShapeDtypeStruct((2, BG, nch, LANES), F32),
        compiler_params=_cparams("arbitrary", "arbitrary"),
        name="compress_kv",
    )(xc, w1cat, cvec, w2dup, b2dup)


def _dup_tiles(x_ref, col, lo):
    out = []
    for m in range(B_KV_HEADS // 2):
        c0 = col * B_KV_HEADS * B_DH + m * LANES
        pair = x_ref[:, c0:c0 + LANES]
        swapped = pltpu.roll(pair, B_DH, axis=1)
        out.append(jnp.where(lo, pair, swapped))
        out.append(jnp.where(lo, swapped, pair))
    return out


def _pages_kernel(pt_ref, *refs, n_pages):
    del pt_ref
    page_refs = refs[:n_pages]
    newk_ref, newv_ref, kt_ref, v_ref = refs[n_pages:]
    lo = lax.broadcasted_iota(I32, (PAGE_SIZE, LANES), 1) < B_DH
    for p in range(n_pages):
        rows = slice(p * PAGE_SIZE, (p + 1) * PAGE_SIZE)
        kd = _dup_tiles(page_refs[p], 0, lo)
        vd = _dup_tiles(page_refs[p], 1, lo)
        for g in range(B_KV_HEADS):
            kt_ref[0, g, :, rows] = kd[g].T.astype(BF16)
            v_ref[0, rows, g * LANES:(g + 1) * LANES] = vd[g].astype(BF16)
    t_new = newk_ref.shape[1]
    tail = v_ref.shape[1] - n_pages * PAGE_SIZE
    pad = jnp.zeros((tail - t_new, newk_ref.shape[2]), F32)
    newk = jnp.concatenate([newk_ref[0], pad], axis=0)
    newv = jnp.concatenate([newv_ref[0], pad], axis=0)
    v_ref[0, n_pages * PAGE_SIZE:, :] = newv.astype(BF16)
    for g in range(B_KV_HEADS):
        for j in range(tail // PAGE_SIZE):
            blk = newk[j * PAGE_SIZE:(j + 1) * PAGE_SIZE, g * LANES:(g + 1) * LANES]
            c0 = (n_pages + j) * PAGE_SIZE
            kt_ref[0, g, :, c0:c0 + PAGE_SIZE] = blk.T.astype(BF16)


def _pages_to_kv(pool, table, new_dup, k_col, v_col, total_len):
    B, n_pages = table.shape
    T = new_dup.shape[1]
    assert total_len % PAGE_SIZE == 0 and total_len >= n_pages * PAGE_SIZE + T
    width = B_KV_HEADS * LANES
    page_specs = [pl.BlockSpec((None, PAGE_SIZE, width), lambda b, pt, p=p: (pt[b, p], 0, 0))
                  for p in range(n_pages)]
    grid_spec = pltpu.PrefetchScalarGridSpec(
        num_scalar_prefetch=1,
        grid=(B,),
        in_specs=page_specs + [pl.BlockSpec((1, T, width), lambda b, pt, c=k_col: (b, 0, c)),
                               pl.BlockSpec((1, T, width), lambda b, pt, c=v_col: (b, 0, c))],
        out_specs=[pl.BlockSpec((1, B_KV_HEADS, LANES, total_len), lambda b, pt: (b, 0, 0, 0)),
                   pl.BlockSpec((1, total_len, width), lambda b, pt: (b, 0, 0))],
    )
    return pl.pallas_call(
        functools.partial(_pages_kernel, n_pages=n_pages),
        grid_spec=grid_spec,
        out_shape=[jax.ShapeDtypeStruct((B, B_KV_HEADS, LANES, total_len), BF16),
                   jax.ShapeDtypeStruct((B, total_len, width), BF16)],
        compiler_params=_cparams("arbitrary"),
        name="pages_to_kv",
    )(table, *([pool] * n_pages), new_dup, new_dup)


def _cmp_to_slc_map_t(ncp, ns):
    j = np.arange(LANES)[:, None]
    i = np.arange(ncp)[None, :]
    lo = np.maximum(i * CMP_STRIDE, j * SLC_BLOCK)
    hi = np.minimum(i * CMP_STRIDE + CMP_BLOCK, (j + 1) * SLC_BLOCK)
    m = np.maximum(hi - lo, 0) / CMP_STRIDE
    return jnp.asarray(np.where(j < ns, m, 0.0), dtype=F32)


def _aug_table_t(length, pos0, with_blocks):
    pos = np.arange(length) + pos0
    a = np.zeros((LANES, length), np.float32)
    a[0] = (pos // MXU_DIM) * MXU_DIM
    a[1] = pos % MXU_DIM
    a[2] = a[0]
    a[3] = a[1]
    if with_blocks:
        blk = np.arange(length) // SLC_BLOCK
        assert AUG_BLK_LANE0 + blk.max() < LANES
        a[AUG_BLK_LANE0 + blk, np.arange(length)] = 1.0
    return jnp.asarray(a, dtype=BF16)


def _split_bf16(x):
    hi = float(np.float32(x).astype(BF16).astype(np.float32))
    lo = float(np.float32(x - hi).astype(BF16).astype(np.float32))
    return hi, lo


def _nsa_attn_kernel(q_ref, gate_ref, ck_ref, cv_ref, cmapt_ref, kts_ref, vs_ref, ktw_ref, vw_ref,
                     augs_ref, augw_ref, o_ref, qa_ref, m_ref, acc_ref, ocmp_ref, sc_ref,
                     *, tq, gpar, pos0, win_pos0, nc, ns, n_sel):
    qi = pl.program_id(1)
    R = B_GROUP
    grows = R * tq
    rows = gpar * grows
    chunk = min(grows, LANES)
    n_chunk = rows // chunk
    ncp = ck_ref.shape[2]
    nsp = sc_ref.shape[0]
    n_slc_tiles = vs_ref.shape[1] // KEY_TILE
    n_win_tiles = vw_ref.shape[1] // KEY_TILE
    half = B_DH
    lane_q = lax.broadcasted_iota(I32, (tq, LANES), 1)
    lo_q = lane_q < half
    ones_lane = lax.broadcasted_iota(I32, (KEY_TILE, LANES), 1) == half
    t_idx = lax.broadcasted_iota(I32, (tq, 1), 0)
    q_lo = pos0 + qi * tq
    q_hi = q_lo + tq - 1
    qpos_i = q_lo + t_idx
    qpos_g = jnp.concatenate([qpos_i] * R, axis=0)
    qpos_all = jnp.concatenate([qpos_g] * gpar, axis=0) if gpar > 1 else qpos_g
    qpos_f = qpos_g.astype(F32)
    kidx = lax.broadcasted_iota(I32, (1, KEY_TILE), 1)
    blk_t = lax.broadcasted_iota(I32, (nsp, tq), 0)
    qpos_t = q_lo + lax.broadcasted_iota(I32, (nsp, tq), 1)
    place = (lax.broadcasted_iota(I32, (nsp, LANES), 1) ==
             lax.broadcasted_iota(I32, (nsp, LANES), 0) + AUG_BLK_LANE0).astype(BF16)
    sel_lanes = (lane_q >= AUG_BLK_LANE0) & (lane_q < AUG_BLK_LANE0 + ns)

    def flash_init():
        m_ref[...] = jnp.full(m_ref.shape, NEG_INF, F32)
        acc_ref[...] = jnp.zeros(acc_ref.shape, F32)

    def flash_tile(kbts, vbs, mask_fn):
        cs = range(n_chunk)
        rs = [pl.ds(c * chunk, chunk) for c in cs]
        gi = [(c * chunk) // grows for c in cs]
        ss = []
        for c in cs:
            s = jnp.dot(qa_ref[rs[c], :], kbts[gi[c]], preferred_element_type=F32)
            if mask_fn is not None:
                s = jnp.where(mask_fn(qpos_all[c * chunk:(c + 1) * chunk]), s, NEG_INF)
            ss.append(s)
        m_old = [m_ref[rs[c], :] for c in cs]
        m_new = [jnp.maximum(m_old[c], jnp.broadcast_to(jnp.max(ss[c], axis=-1, keepdims=True), (chunk, LANES)))
                 for c in cs]
        es = [jnp.exp(ss[c] - jnp.concatenate([m_new[c]] * (KEY_TILE // LANES), axis=1)) for c in cs]
        for c in cs:
            alpha = jnp.exp(m_old[c] - m_new[c])
            acc_ref[rs[c], :] = alpha * acc_ref[rs[c], :] + jnp.dot(es[c].astype(BF16), vbs[gi[c]],
                                                                   preferred_element_type=F32)
            m_ref[rs[c], :] = m_new[c]

    def flash_done():
        acc = acc_ref[...]
        return acc / acc[:, half:half + 1]

    for g0 in range(0, B_KV_HEADS, gpar):
        groups = list(range(g0, g0 + gpar))
        for gi, g in enumerate(groups):
            c0 = q_ref[0, :, g * 2 * LANES:g * 2 * LANES + LANES]
            c1 = q_ref[0, :, g * 2 * LANES + LANES:(g + 1) * 2 * LANES]
            q_st = jnp.concatenate([jnp.where(lo_q, c0, 0.0), jnp.where(lo_q, 0.0, c0),
                                    jnp.where(lo_q, c1, 0.0), jnp.where(lo_q, 0.0, c1)], axis=0).astype(BF16)
            slopes = [2.0 ** (-8.0 * (g * R + r + 1) / B_HEADS) for r in range(R)]
            slope = jnp.concatenate([jnp.full((tq, 1), s, F32) for s in slopes], axis=0)

            ckb = ck_ref[0, g].astype(BF16)
            cvb = cv_ref[0, g].astype(BF16)
            s = lax.dot_general(q_st, ckb, (((1,), (1,)), ((), ())), preferred_element_type=F32)
            c_iota = lax.broadcasted_iota(I32, (1, ncp), 1)
            c_end = (c_iota * CMP_STRIDE + (CMP_BLOCK - 1)).astype(F32)
            dist = qpos_f - c_end
            mask = (dist >= 0.0) & (c_iota < nc)
            s = jnp.where(mask, s - slope * dist, NEG_INF)
            m = jnp.max(s, axis=-1, keepdims=True)
            e = jnp.where(mask, jnp.exp(s - m), 0.0)
            p = e / jnp.maximum(jnp.sum(e, axis=-1, keepdims=True), 1e-30)
            ocmp_ref[gi * grows:(gi + 1) * grows, :] = jnp.dot(p.astype(BF16), cvb, preferred_element_type=F32)

            p_sum = p[0:tq] + p[tq:2 * tq] + p[2 * tq:3 * tq] + p[3 * tq:4 * tq]
            imp_t = lax.dot_general(cmapt_ref[...], p_sum, (((1,), (1,)), ((), ())), precision=HIGHEST,
                                    preferred_element_type=F32)[0:nsp]
            cur = qpos_t // SLC_BLOCK
            forced = (blk_t == 0) | (blk_t == cur) | (blk_t == cur - 1)
            score = jnp.where(forced, FORCE_SCORE, jnp.where(blk_t * SLC_BLOCK <= qpos_t, imp_t, -1.0))
            score = jnp.where(blk_t < ns, score, -jnp.inf)
            sc_ref[...] = score
            rank = jnp.zeros((nsp, tq), F32)
            for j in range(ns):
                other = sc_ref[pl.ds(j, 1), :]
                ahead = (other > score) | ((other == score) & (j < blk_t))
                rank = rank + ahead.astype(F32)
            sel_t = ((rank < float(n_sel)) & (score >= 0.0)).astype(BF16)
            sel_q = lax.dot_general(sel_t, place, (((0,), (0,)), ((), ())), preferred_element_type=F32)
            sel_term = jnp.where(sel_q > 0.5, 0.0, UNSELECTED)

            q_hi_half = []
            for r in range(R):
                s_hi, s_lo = _split_bf16(slopes[r])
                consts = jnp.where(lane_q < 2, s_hi, jnp.where(lane_q < AUG_POS_LANES, s_lo, 0.0))
                q_hi_half.append(jnp.where(sel_lanes, sel_term, consts))
            qa_ref[gi * grows:(gi + 1) * grows, :] = jnp.concatenate(
                [q_st, jnp.concatenate(q_hi_half, axis=0).astype(BF16)], axis=1)

        def key_tiles(kt_ref, v_ref, aug_ref, t0):
            kbts = [jnp.concatenate([kt_ref[0, g, :, pl.ds(t0, KEY_TILE)], aug_ref[:, pl.ds(t0, KEY_TILE)]], axis=0)
                    for g in groups]
            vbs = [jnp.where(ones_lane, 1.0, v_ref[0, pl.ds(t0, KEY_TILE), g * LANES:(g + 1) * LANES]).astype(BF16)
                   for g in groups]
            return kbts, vbs

        n_kt = jnp.minimum(q_hi // KEY_TILE + 1, n_slc_tiles)
        n_full = jnp.minimum((q_lo + 1) // KEY_TILE, n_kt)

        def slc_tile(kt, masked):
            t0 = pl.multiple_of(kt * KEY_TILE, KEY_TILE)
            kbts, vbs = key_tiles(kts_ref, vs_ref, augs_ref, t0)
            flash_tile(kbts, vbs, (lambda qp: (t0 + kidx) <= qp) if masked else None)

        flash_init()
        lax.fori_loop(0, n_full, lambda kt, c: (slc_tile(kt, False), c)[1], 0)
        lax.fori_loop(n_full, n_kt, lambda kt, c: (slc_tile(kt, True), c)[1], 0)
        o_slc = flash_done()

        w_first = jnp.maximum(q_lo - (WINDOW - 1) - win_pos0, 0) // KEY_TILE
        w_last = jnp.minimum((q_hi - win_pos0) // KEY_TILE + 1, n_win_tiles)

        def win_tile(kt, c):
            t0 = pl.multiple_of(kt * KEY_TILE, KEY_TILE)
            kbts, vbs = key_tiles(ktw_ref, vw_ref, augw_ref, t0)
            kpos = win_pos0 + t0 + kidx
            flash_tile(kbts, vbs, lambda qp: (kpos <= qp) & (kpos > qp - WINDOW))
            return c

        flash_init()
        lax.fori_loop(w_first, w_last, win_tile, 0)
        o_win = flash_done()

        for gi, g in enumerate(groups):
            gr = slice(gi * grows, (gi + 1) * grows)

            def gate_col(j):
                base = j * B_HEADS + g * R
                return jnp.concatenate([gate_ref[0, :, base + r:base + r + 1] for r in range(R)], axis=0)

            o_st = gate_col(0) * ocmp_ref[gr, :] + gate_col(1) * o_slc[gr] + gate_col(2) * o_win[gr]
            o_up = pltpu.roll(o_st, half, axis=1)
            o_ref[0, :, g * 2 * LANES:g * 2 * LANES + LANES] = jnp.where(lo_q, o_st[0:tq], o_up[tq:2 * tq])
            o_ref[0, :, g * 2 * LANES + LANES:(g + 1) * 2 * LANES] = jnp.where(lo_q, o_st[2 * tq:3 * tq], o_up[3 * tq:])


def _nsa_attn(q, gates, ck, cv, kt_slc, v_slc, v_slc_col, kt_win, v_win, v_win_col, *, pos0, win_pos0, nc, ns):
    B, T, _ = q.shape
    tq = min(T, LANES)
    assert T % tq == 0
    gpar = B_KV_HEADS if B_GROUP * tq < LANES else 1
    ncp = ck.shape[2]
    n_sel = min(N_SELECT, ns)
    nsp = -(-ns // (2 * SUBLANES)) * (2 * SUBLANES)
    Ls, Lw = kt_slc.shape[3], kt_win.shape[3]
    assert Ls % KEY_TILE == 0 and Lw % KEY_TILE == 0 and v_slc.shape[1] == Ls and v_win.shape[1] == Lw
    rows = gpar * B_GROUP * tq
    width = B_KV_HEADS * LANES
    kern = functools.partial(_nsa_attn_kernel, tq=tq, gpar=gpar, pos0=pos0, win_pos0=win_pos0, nc=nc, ns=ns,
                             n_sel=n_sel)
    kt_spec = lambda L: pl.BlockSpec((1, B_KV_HEADS, LANES, L), lambda b, i: (b, 0, 0, 0))
    v_spec = lambda L, col: pl.BlockSpec((1, L, width), lambda b, i, c=col: (b, 0, c))
    return pl.pallas_call(
        kern,
        grid=(B, T // tq),
        in_specs=[pl.BlockSpec((1, tq, D_MODEL), lambda b, i: (b, i, 0)),
                  pl.BlockSpec((1, tq, LANES), lambda b, i: (b, i, 0)),
                  pl.BlockSpec((1, B_KV_HEADS, ncp, LANES), lambda b, i: (b, 0, 0, 0)),
                  pl.BlockSpec((1, B_KV_HEADS, ncp, LANES), lambda b, i: (b, 0, 0, 0)),
                  pl.BlockSpec((LANES, ncp), lambda b, i: (0, 0)),
                  kt_spec(Ls), v_spec(Ls, v_slc_col), kt_spec(Lw), v_spec(Lw, v_win_col),
                  pl.BlockSpec((LANES, Ls), lambda b, i: (0, 0)),
                  pl.BlockSpec((LANES, Lw), lambda b, i: (0, 0))],
        out_specs=pl.BlockSpec((1, tq, D_MODEL), lambda b, i: (b, i, 0)),
        out_shape=jax.ShapeDtypeStruct((B, T, D_MODEL), F32),
        scratch_shapes=[pltpu.VMEM((rows, 2 * LANES), BF16), pltpu.VMEM((rows, LANES), F32),
                        pltpu.VMEM((rows, LANES), F32), pltpu.VMEM((rows, LANES), F32),
                        pltpu.VMEM((nsp, tq), F32)],
        compiler_params=_cparams("arbitrary", "arbitrary"),
        name="nsa_attn",
    )(q, gates, ck, cv, _cmp_to_slc_map_t(ncp, ns), kt_slc, v_slc, kt_win, v_win,
      _aug_table_t(Ls, 0, True), _aug_table_t(Lw, win_pos0, False))


def _deinterleave_kernel(w_ref, perm_ref, glu_ref, lin_ref):
    for c in range(w_ref.shape[1] // MXU_DIM):
        t = jnp.dot(w_ref[:, c * MXU_DIM:(c + 1) * MXU_DIM].astype(BF16), perm_ref[...],
                    preferred_element_type=F32)
        glu_ref[:, c * LANES:(c + 1) * LANES] = t[:, :LANES].astype(BF16)
        lin_ref[:, c * LANES:(c + 1) * LANES] = t[:, LANES:].astype(BF16)


def _deinterleave_up(w_up):
    E, K, N2 = w_up.shape
    rt = 512
    i = np.arange(MXU_DIM)[:, None]
    j = np.arange(MXU_DIM)[None, :]
    perm = jnp.asarray(((i % 2) * LANES + i // 2) == j, dtype=BF16)
    return pl.pallas_call(
        _deinterleave_kernel,
        grid=(E, K // rt),
        in_specs=[pl.BlockSpec((None, rt, N2), lambda e, r: (e, r, 0)),
                  pl.BlockSpec((MXU_DIM, MXU_DIM), lambda e, r: (0, 0))],
        out_specs=[pl.BlockSpec((None, rt, N2 // 2), lambda e, r: (e, r, 0))] * 2,
        out_shape=[jax.ShapeDtypeStruct((E, K, N2 // 2), BF16)] * 2,
        compiler_params=_cparams("arbitrary", "arbitrary"),
        name="deinterleave_up",
    )(w_up, perm)


def _route_kernel(te_ref, start_ref, pos_ref, cnt_ref):
    rt = te_ref.shape[0]

    @pl.when(pl.program_id(0) == 0)
    def _():
        cnt_ref[...] = start_ref[...]

    te = te_ref[...]
    lane = lax.broadcasted_iota(I32, (rt, LANES), 1)
    onehots = [(lane == te[:, k:k + 1]).astype(F32) for k in range(TOP_K)]
    total = onehots[0] + onehots[1] + onehots[2] + onehots[3]
    r = lax.broadcasted_iota(I32, (rt, rt), 0)
    c = lax.broadcasted_iota(I32, (rt, rt), 1)
    earlier = (c < r).astype(BF16)
    base = cnt_ref[...] + jnp.dot(earlier, total.astype(BF16), preferred_element_type=F32)
    pos = jnp.zeros((rt, LANES), I32)
    for k in range(TOP_K):
        pk = jnp.sum(onehots[k] * base, axis=-1, keepdims=True)
        pos = jnp.where(lane == k, pk.astype(I32), pos)
    pos_ref[...] = pos
    cnt_ref[...] = cnt_ref[...] + jnp.sum(total, axis=0, keepdims=True)


def _route(te, start):
    N = te.shape[0]
    rt = _largest_divisor(N, (256, 128, 64, 32, 16, 8))
    return pl.pallas_call(
        _route_kernel,
        grid=(N // rt,),
        in_specs=[pl.BlockSpec((rt, LANES), lambda i: (i, 0)), pl.BlockSpec((1, LANES), lambda i: (0, 0))],
        out_specs=pl.BlockSpec((rt, LANES), lambda i: (i, 0)),
        out_shape=jax.ShapeDtypeStruct((N, LANES), I32),
        scratch_shapes=[pltpu.VMEM((1, LANES), F32)],
        compiler_params=_cparams("arbitrary"),
        name="moe_route",
    )(te, start)


def _row_copy_wait(src_rows, dst_rows, sem):
    pltpu.make_async_copy(src_rows, dst_rows, sem).wait()


def _dispatch_kernel(pos_ref, x_ref, xr_in_ref, xr_ref, sem):
    del xr_in_ref
    rt = x_ref.shape[0]

    def body(i, carry):
        for k in range(TOP_K):
            dst = pos_ref[i * TOP_K + k]
            pltpu.make_async_copy(x_ref.at[pl.ds(i, 1)], xr_ref.at[pl.ds(dst, 1)], sem).start(priority=k % 2)
        return carry

    lax.fori_loop(0, rt, body, 0)
    for k in range(TOP_K):
        _row_copy_wait(x_ref, xr_ref.at[pl.ds(0, rt)], sem)


def _dispatch(x, pos_flat, m_pad):
    N, D = x.shape
    rt = _largest_divisor(N, (256, 128, 64, 32, 16, 8))
    return pl.pallas_call(
        _dispatch_kernel,
        grid=(N // rt,),
        in_specs=[pl.BlockSpec((rt * TOP_K,), lambda i: (i,), memory_space=pltpu.SMEM),
                  pl.BlockSpec((rt, D), lambda i: (i, 0)),
                  pl.BlockSpec(memory_space=pl.ANY)],
        out_specs=pl.BlockSpec(memory_space=pl.ANY),
        out_shape=jax.ShapeDtypeStruct((m_pad, D), F32),
        scratch_shapes=[pltpu.SemaphoreType.DMA(())],
        input_output_aliases={2: 0},
        compiler_params=_cparams("arbitrary"),
        name="moe_dispatch",
    )(pos_flat, x, jnp.zeros((m_pad, D), F32))


def _moe_kernel(be_ref, nv_ref, x_ref, wg_ref, wl_ref, bg_ref, bl_ref, wd_ref, bd_ref, o_ref):
    i = pl.program_id(0)

    @pl.when(i < nv_ref[0])
    def _():
        x = x_ref[...].astype(BF16)
        glu = jnp.dot(x, wg_ref[...], preferred_element_type=F32) + bg_ref[...]
        lin = jnp.dot(x, wl_ref[...], preferred_element_type=F32) + bl_ref[...]
        glu = jnp.minimum(glu, SWIGLU_LIMIT)
        lin = jnp.clip(lin, -SWIGLU_LIMIT, SWIGLU_LIMIT)
        act = glu * _sigmoid(SWIGLU_ALPHA * glu) * (lin + 1.0)
        o_ref[...] = jnp.dot(act.astype(BF16), wd_ref[...], preferred_element_type=F32) + bd_ref[...]

    @pl.when(i >= nv_ref[0])
    def _():
        o_ref[...] = jnp.zeros_like(o_ref)


def _moe_ffn_blocks(xr, blk_exp, nvalid, wg, wl, bg, bl, wd, bd):
    m_pad = xr.shape[0]
    n_blk = m_pad // MOE_ROWS
    wspec = lambda cols: pl.BlockSpec((None, D_MODEL, cols), lambda i, be, nv: (be[i], 0, 0))
    bspec = lambda cols: pl.BlockSpec((None, 1, cols), lambda i, be, nv: (be[i], 0, 0))
    grid_spec = pltpu.PrefetchScalarGridSpec(
        num_scalar_prefetch=2,
        grid=(n_blk,),
        in_specs=[pl.BlockSpec((MOE_ROWS, D_MODEL), lambda i, be, nv: (i, 0)),
                  wspec(D_FF), wspec(D_FF), bspec(D_FF), bspec(D_FF), wspec(D_MODEL), bspec(D_MODEL)],
        out_specs=pl.BlockSpec((MOE_ROWS, D_MODEL), lambda i, be, nv: (i, 0)),
    )
    return pl.pallas_call(
        _moe_kernel,
        grid_spec=grid_spec,
        out_shape=jax.ShapeDtypeStruct((m_pad, D_MODEL), F32),
        compiler_params=_cparams("arbitrary"),
        name="moe_ffn",
    )(blk_exp, nvalid, xr, wg, wl, bg, bl, wd, bd)


def _combine_kernel(pos_ref, x_ref, tg_ref, gf_ref, lng_ref, lnb_ref, yr_ref, xn_ref, ybuf, sem):
    sb, tt, d = x_ref.shape
    rows = sb * tt

    def body(i, carry):
        for k in range(TOP_K):
            src = pos_ref[i * TOP_K + k]
            pltpu.make_async_copy(yr_ref.at[pl.ds(src, 1)], ybuf.at[k, pl.ds(i, 1)], sem).start(priority=k % 2)
        return carry

    lax.fori_loop(0, rows, body, 0)
    for k in range(TOP_K):
        _row_copy_wait(yr_ref.at[pl.ds(0, rows)], ybuf.at[k], sem)
    tg = tg_ref[...].reshape(rows, LANES)
    y = tg[:, 0:1] * ybuf[0]
    for k in range(1, TOP_K):
        y = y + tg[:, k:k + 1] * ybuf[k]
    z = DN_ALPHA * x_ref[...] + (1.0 + gf_ref[...]) * y.reshape(sb, tt, d)
    xn_ref[...] = _layer_norm_rows(z, lng_ref[...], lnb_ref[...])


def _combine(x, yr, pos_flat, tg, mod, ln_g, ln_b):
    B, T, _ = x.shape
    sb, tt = _row_tiles(B, T)
    rows = sb * tt
    nj = T // tt
    vec = pl.BlockSpec((1, 1, D_MODEL), lambda i, j: (0, 0, 0))
    return pl.pallas_call(
        _combine_kernel,
        grid=(B // sb, nj),
        in_specs=[pl.BlockSpec((rows * TOP_K,), lambda i, j: (i * nj + j,), memory_space=pltpu.SMEM),
                  _tok_spec(sb, tt), _tok_spec(sb, tt, LANES), _mod_spec(sb, 5), vec, vec,
                  pl.BlockSpec(memory_space=pl.ANY)],
        out_specs=_tok_spec(sb, tt),
        out_shape=jax.ShapeDtypeStruct((B, T, D_MODEL), F32),
        scratch_shapes=[pltpu.VMEM((TOP_K, rows, D_MODEL), F32), pltpu.SemaphoreType.DMA(())],
        compiler_params=_cparams("arbitrary", "arbitrary"),
        name="moe_combine",
    )(pos_flat, x, tg, mod, ln_g.reshape(1, 1, D_MODEL), ln_b.reshape(1, 1, D_MODEL), yr)


def _moe_plan(te_all):
    N = te_all.shape[0]
    M = N * TOP_K
    onehot = (te_all[:, :TOP_K, None] == jnp.arange(N_EXPERTS, dtype=I32)[None, None, :])
    counts = jnp.sum(onehot.astype(I32), axis=(0, 1))
    padded = (counts + MOE_ROWS - 1) // MOE_ROWS * MOE_ROWS
    pend = jnp.cumsum(padded)
    pstart = pend - padded
    n_blk = -(-(M + N_EXPERTS * (MOE_ROWS - 1)) // MOE_ROWS)
    blk_first = jnp.arange(n_blk, dtype=I32)[:, None] * MOE_ROWS
    blk_exp = jnp.minimum(jnp.sum((pend[None, :] <= blk_first).astype(I32), axis=1), N_EXPERTS - 1).astype(I32)
    nvalid = (pend[-1] // MOE_ROWS).astype(I32).reshape(1)
    start = jnp.pad(pstart.astype(F32), (0, LANES - N_EXPERTS)).reshape(1, LANES)
    return start, blk_exp, nvalid, n_blk * MOE_ROWS


def _compress_inputs(cmp_raw):
    B, L, _ = cmp_raw.shape
    nch = L // CMP_STRIDE
    x = cmp_raw[:, :nch * CMP_STRIDE].reshape(B, nch, CMP_STRIDE, 2, B_KV_HEADS, B_DH)
    x = x.transpose(3, 0, 4, 1, 2, 5)
    return x.reshape(2, B * B_KV_HEADS, nch, CMP_STRIDE * B_DH), nch


def _pad_rows(a, mult, axis=1):
    n = a.shape[axis]
    pad = (-n) % mult
    if pad == 0:
        return a
    widths = [(0, 0)] * a.ndim
    widths[axis] = (0, pad)
    return jnp.pad(a, widths)


def kernel(x_prompt, x_sample, state_hgrn, cache_cmp_kv, cache_slc_kv, state_win_kv, page_table, c_prompt, c_sample,
           ada_w, ada_b, ln_g, ln_b, a_w_in, a_lb, a_onorm_g, a_w_o, kv_ada_w, kv_ada_b, kv_w, cmp_pos, cmp_w1, cmp_b1,
           cmp_w2, cmp_b2, b_w_in, b_w_o, moe_wr, moe_br, moe_w_up, moe_b_up, moe_w_down, moe_b_down):
    Bp, Tp, _ = x_prompt.shape
    Bs, Ts, _ = x_sample.shape
    n_pages = page_table.shape[1]
    past_len = n_pages * PAGE_SIZE
    keep_s = state_win_kv.shape[1]
    G, DH = B_KV_HEADS, B_DH
    GD = G * DH
    Np, Ns = Bp * Tp, Bs * Ts
    assert keep_s % PAGE_SIZE == 0

    a_w_in_bf = a_w_in.astype(BF16)
    a_w_o_bf = a_w_o.astype(BF16)
    b_w_o_bf = b_w_o.astype(BF16)
    dup_cols = [jnp.concatenate([kv_w[:, c * GD + g * DH:c * GD + (g + 1) * DH]] * 2, axis=1)
                for c in range(2, 6) for g in range(G)]
    kv_w_bf = jnp.concatenate([kv_w] + dup_cols, axis=1).astype(BF16)
    wq_bf = b_w_in[:, :, :D_MODEL].astype(BF16)
    n_l = b_w_in.shape[0]
    wg = b_w_in[:, :, D_MODEL:].reshape(n_l, D_MODEL, B_HEADS, 3).transpose(0, 1, 3, 2).reshape(n_l, D_MODEL, 3 * B_HEADS)
    wg_bf = jnp.pad(wg, ((0, 0), (0, 0), (0, LANES - 3 * B_HEADS))).astype(BF16)
    wr_pad = jnp.pad(moe_wr, ((0, 0), (0, 0), (0, LANES - N_EXPERTS)))
    br_pad = jnp.pad(moe_br, ((0, 0), (0, LANES - N_EXPERTS)), constant_values=NEG_INF).reshape(DEPTH, 1, LANES)
    w_glu, w_lin = _deinterleave_up(moe_w_up.reshape(DEPTH * N_EXPERTS, D_MODEL, 2 * D_FF))
    w_glu = w_glu.reshape(DEPTH, N_EXPERTS, D_MODEL, D_FF)
    w_lin = w_lin.reshape(DEPTH, N_EXPERTS, D_MODEL, D_FF)
    b_glu = moe_b_up[..., 0::2].reshape(DEPTH, N_EXPERTS, 1, D_FF)
    b_lin = moe_b_up[..., 1::2].reshape(DEPTH, N_EXPERTS, 1, D_FF)
    w_dn = moe_w_down.astype(BF16)
    b_dn = moe_b_down.reshape(DEPTH, N_EXPERTS, 1, D_MODEL)
    lb_p = jax.nn.softmax(a_lb.astype(F32), axis=0)
    lb_all = jnp.maximum(jnp.cumsum(lb_p, axis=0) - lb_p[0], 0.0)
    w1cat = jnp.concatenate([cmp_w1[:, :CMP_STRIDE * DH], cmp_w1[:, CMP_STRIDE * DH:]], axis=-1)
    cvec = _cmp_const(cmp_pos.reshape(2, 1, CMP_BLOCK * DH), cmp_w1, cmp_b1.reshape(2, 1, CMP_HID))
    w2dup = jnp.concatenate([cmp_w2, cmp_w2], axis=-1)
    b2dup = jnp.concatenate([cmp_b2, cmp_b2], axis=-1).reshape(2, 1, LANES)

    n_c = Bp + Bs
    c_all = _pad_rows(jnp.concatenate([c_prompt, c_sample], axis=0), SUBLANES, axis=0)
    mod_all = _silu_linear(c_all, ada_w, ada_b)
    kvmod_all = _silu_linear(c_all, kv_ada_w[None], kv_ada_b[None])[0]
    mods = [(mod_all[l, :Bp].reshape(Bp, 1, 6 * D_MODEL), mod_all[l, Bp:n_c].reshape(Bs, 1, 6 * D_MODEL))
            for l in range(DEPTH)]
    kvmods = (kvmod_all[:Bp].reshape(Bp, 1, 2 * D_MODEL), kvmod_all[Bp:n_c].reshape(Bs, 1, 2 * D_MODEL))

    xs = [x_prompt, x_sample]
    s0s = [jnp.zeros((N_A_LAYERS, Bp, A_HEADS, A_DK, A_DV), F32), state_hgrn]
    hgrn_out = [[], []]
    ctx = [None, None]
    new_kv = [None, None]

    def keys_t(dup, col):
        b, length, _ = dup.shape
        return dup[:, :, col * G * LANES:(col + 1) * G * LANES].reshape(b, length, G, LANES).transpose(0, 2, 3, 1)

    def build_ctx(t, x):
        B, T, _ = x.shape
        if t == 0:
            kv, dup = _kv_proj(x, kvmods[t], kv_w_bf, BF16)
            cmp_raw = kv[:, :, :2 * GD]
            if T % KEY_TILE:
                dup = _pad_rows(dup, KEY_TILE)
            attn_kv = (keys_t(dup, 0), dup, 1, keys_t(dup, 2), dup, 3)
            L, pos0, win_pos0 = T, 0, 0
            win_out = kv[:, T - min(WINDOW, T):, 4 * GD:]
        else:
            kv, dup = _kv_proj(x, kvmods[t], kv_w_bf, F32)
            L = past_len + T
            pool_cmp = cache_cmp_kv.reshape(-1, PAGE_SIZE, 2 * GD)
            cmp_raw = pool_cmp[page_table].reshape(B, past_len, 2 * GD)
            if L // CMP_STRIDE * CMP_STRIDE > past_len:
                cmp_raw = jnp.concatenate([cmp_raw, kv[:, :, :2 * GD]], axis=1)
            ls_pad = -(-L // KEY_TILE) * KEY_TILE
            lw_pad = -(-(keep_s + T) // KEY_TILE) * KEY_TILE
            kt_slc, v_slc = _pages_to_kv(cache_slc_kv.reshape(-1, PAGE_SIZE, 2 * GD), page_table, dup, 0, 1, ls_pad)
            n_wp = keep_s // PAGE_SIZE
            win_table = jnp.arange(B * n_wp, dtype=I32).reshape(B, n_wp)
            kt_win, v_win = _pages_to_kv(state_win_kv.reshape(B * n_wp, PAGE_SIZE, 2 * GD), win_table, dup, 2, 3, lw_pad)
            attn_kv = (kt_slc, v_slc, 0, kt_win, v_win, 0)
            pos0 = past_len
            win_pos0 = past_len - keep_s
            win_all = jnp.concatenate([state_win_kv.reshape(B, keep_s, 2 * GD), kv[:, :, 4 * GD:]], axis=1)
            win_out = win_all[:, -keep_s:]
        xc, nch = _compress_inputs(cmp_raw)
        nc = nch - CMP_RATIO + 1
        ns = -(-L // SLC_BLOCK)
        ckv = _compress(xc, w1cat, cvec, w2dup, b2dup)
        ckv = _pad_rows(ckv.reshape(2, B, G, nch, LANES), LANES, axis=3)
        info = dict(ck=ckv[0], cv=ckv[1], attn_kv=attn_kv, pos0=pos0, win_pos0=win_pos0, nc=nc, ns=ns)
        outs = (kv[:, :, :2 * GD].reshape(B, T, 2, G, DH), kv[:, :, 2 * GD:4 * GD].reshape(B, T, 2, G, DH),
                win_out.reshape(B, -1, 2, G, DH))
        return info, outs

    for l in range(DEPTH):
        o_mix = []
        for t in range(2):
            x = xs[t]
            mod = mods[l][t]
            if l < N_A_LAYERS:
                q, k, g, v, og = _hgrn_in(x, mod, a_w_in_bf[l], lb_all[l])
                o, S = _hgrn_scan(q, k, g, v, og, s0s[t][l], a_onorm_g[l])
                hgrn_out[t].append(S)
            else:
                j = l - N_A_LAYERS
                if ctx[t] is None:
                    ctx[t], new_kv[t] = build_ctx(t, x)
                c = ctx[t]
                q, gates = _nsa_in(x, mod, wq_bf[j], wg_bf[j])
                o = _nsa_attn(q, gates, c['ck'], c['cv'], *c['attn_kv'],
                              pos0=c['pos0'], win_pos0=c['win_pos0'], nc=c['nc'], ns=c['ns'])
            o_mix.append(o)
        w_o = a_w_o_bf[l] if l < N_A_LAYERS else b_w_o_bf[l - N_A_LAYERS]
        hf, te, tg = [], [], []
        for t in range(2):
            xn, h, e, gt = _mixer_out(o_mix[t], w_o, xs[t], mods[l][t], ln_g[l, 0], ln_b[l, 0], wr_pad[l], br_pad[l])
            xs[t] = xn
            hf.append(h.reshape(-1, D_MODEL))
            te.append(e.reshape(-1, LANES))
            tg.append(gt)
        te_all = jnp.concatenate(te, axis=0)
        start, blk_exp, nvalid, m_pad = _moe_plan(te_all)
        pos_flat = _route(te_all, start)[:, :TOP_K].reshape(-1)
        xr = _dispatch(jnp.concatenate(hf, axis=0), pos_flat, m_pad)
        yr = _moe_ffn_blocks(xr, blk_exp, nvalid, w_glu[l], w_lin[l], b_glu[l], b_lin[l], w_dn[l], b_dn[l])
        pos_t = [pos_flat[:Np * TOP_K], pos_flat[Np * TOP_K:]]
        for t in range(2):
            xs[t] = _combine(xs[t], yr, pos_t[t], tg[t], mods[l][t], ln_g[l, 1], ln_b[l, 1])

    return (xs[0], xs[1], jnp.stack(hgrn_out[0], axis=0), jnp.stack(hgrn_out[1], axis=0),
            new_kv[0][0], new_kv[1][0], new_kv[0][1], new_kv[1][1], new_kv[0][2], new_kv[1][2])
```

```python
import functools

import numpy as np
import jax
import jax.numpy as jnp
from jax import lax
from jax.experimental import pallas as pl
from jax.experimental.pallas import tpu as pltpu

F32 = jnp.float32
BF16 = jnp.bfloat16
I32 = jnp.int32
HIGHEST = lax.Precision.HIGHEST

D_MODEL = 1024
DEPTH = 4
PAGE_SIZE = 128
N_A_LAYERS = DEPTH // 2
A_HEADS = 8
A_DK = 128
A_DV = D_MODEL // A_HEADS
A_DF = A_HEADS * A_DK
HGRN_CHUNK = 32
B_HEADS = 16
B_DH = D_MODEL // B_HEADS
B_KV_HEADS = 4
B_GROUP = B_HEADS // B_KV_HEADS
CMP_BLOCK = 32
CMP_STRIDE = 16
CMP_RATIO = CMP_BLOCK // CMP_STRIDE
CMP_HID = 2 * B_DH
SLC_BLOCK = 64
N_SELECT = 16
WINDOW = 512
FORCE_SCORE = 1e4
N_EXPERTS = 32
TOP_K = 4
D_FF = D_MODEL
SWIGLU_LIMIT = 7.0
SWIGLU_ALPHA = 1.702
DN_ALPHA = (2 * DEPTH) ** 0.25
LN_EPS = 1e-5
RMS_EPS = 1e-6
NEG_INF = -1e30

LANES = 128
SUBLANES = 8
MXU_DIM = 256
ROW_TILE = 256
MOE_ROWS = 256
KEY_TILE = 512
VMEM_LIMIT = 56 * 1024 * 1024

AUG_POS_LANES = 4
AUG_BLK_LANE0 = 8
UNSELECTED = -(2.0 ** 30)


def _cparams(*sem):
    return pltpu.CompilerParams(dimension_semantics=sem, vmem_limit_bytes=VMEM_LIMIT)


def _row_tiles(B, T):
    if T >= ROW_TILE:
        assert T % ROW_TILE == 0
        return 1, ROW_TILE
    sb = max(1, ROW_TILE // T)
    while B % sb:
        sb //= 2
    return sb, T


def _largest_divisor(n, candidates):
    for c in candidates:
        if n % c == 0:
            return c
    raise ValueError(n)


def _sigmoid(x):
    return 1.0 / (1.0 + jnp.exp(-x))


def _linear_kernel(x_ref, w_ref, b_ref, o_ref):
    x = x_ref[...]
    x = x * _sigmoid(x)
    o_ref[...] = jnp.dot(x.astype(BF16), w_ref[...].astype(BF16), preferred_element_type=F32) + b_ref[...]


def _silu_linear(x, w, b):
    L, K, N = w.shape
    M = x.shape[0]
    tn = 1024
    return pl.pallas_call(
        _linear_kernel,
        grid=(L, N // tn),
        in_specs=[pl.BlockSpec((M, K), lambda l, j: (0, 0)),
                  pl.BlockSpec((None, K, tn), lambda l, j: (l, 0, j)),
                  pl.BlockSpec((None, 1, tn), lambda l, j: (l, 0, j))],
        out_specs=pl.BlockSpec((None, M, tn), lambda l, j: (l, 0, j)),
        out_shape=jax.ShapeDtypeStruct((L, M, N), F32),
        compiler_params=_cparams("arbitrary", "arbitrary"),
        name="silu_linear",
    )(x, w, b.reshape(L, 1, N))


def _modulate(x_ref, sc_ref, sh_ref):
    sb, tt, d = x_ref.shape
    h = x_ref[...] * (1.0 + sc_ref[...]) + sh_ref[...]
    return h.reshape(sb * tt, d)


def _hgrn_in_kernel(x_ref, sh_ref, sc_ref, w_ref, lb_ref, q_ref, k_ref, g_ref, v_ref, og_ref):
    sb, tt, d = x_ref.shape
    h = _modulate(x_ref, sc_ref, sh_ref).astype(BF16)
    proj = jnp.dot(h, w_ref[...], preferred_element_type=F32)
    qz = proj[:, :A_DF]
    fz = proj[:, A_DF:2 * A_DF]
    lb = lb_ref[...]
    log_sig = jnp.minimum(fz, 0.0) - jnp.log1p(jnp.exp(-jnp.abs(fz)))
    pos = lb > 0.0
    log_lb = jnp.where(pos, jnp.log(jnp.where(pos, lb, 1.0)), NEG_INF)
    other = jnp.log1p(-lb) + log_sig
    logf = jnp.maximum(log_lb, other) + jnp.log1p(jnp.exp(-jnp.abs(log_lb - other)))
    q_ref[...] = (qz * _sigmoid(qz)).reshape(sb, tt, A_DF)
    g_ref[...] = logf.reshape(sb, tt, A_DF)
    k_ref[...] = ((1.0 - lb) * _sigmoid(-fz)).reshape(sb, tt, A_DF)
    v_ref[...] = proj[:, 2 * A_DF:2 * A_DF + D_MODEL].reshape(sb, tt, D_MODEL)
    og_ref[...] = proj[:, 2 * A_DF + D_MODEL:].reshape(sb, tt, D_MODEL)


def _mod_spec(sb, comp):
    return pl.BlockSpec((sb, 1, D_MODEL), lambda i, j, c=comp: (i, 0, c))


def _tok_spec(sb, tt, width=D_MODEL):
    return pl.BlockSpec((sb, tt, width), lambda i, j: (i, j, 0))


def _hgrn_in(x, mod, w_bf, lb):
    B, T, _ = x.shape
    sb, tt = _row_tiles(B, T)
    n_out = w_bf.shape[1]
    shp = jax.ShapeDtypeStruct((B, T, D_MODEL), F32)
    return pl.pallas_call(
        _hgrn_in_kernel,
        grid=(B // sb, T // tt),
        in_specs=[_tok_spec(sb, tt), _mod_spec(sb, 0), _mod_spec(sb, 1),
                  pl.BlockSpec((D_MODEL, n_out), lambda i, j: (0, 0)),
                  pl.BlockSpec((1, A_DF), lambda i, j: (0, 0))],
        out_specs=[_tok_spec(sb, tt)] * 5,
        out_shape=[shp] * 5,
        compiler_params=_cparams("arbitrary", "arbitrary"),
        name="hgrn_in",
    )(x, mod, mod, w_bf, lb.reshape(1, A_DF))


def _nsa_in_kernel(x_ref, sh_ref, sc_ref, wq_ref, wg_ref, q_ref, gate_ref):
    sb, tt, d = x_ref.shape
    h = _modulate(x_ref, sc_ref, sh_ref).astype(BF16)
    q = jnp.dot(h, wq_ref[...], preferred_element_type=F32) * (B_DH ** -0.5)
    gz = jnp.dot(h, wg_ref[...], preferred_element_type=F32)
    q_ref[...] = q.reshape(sb, tt, D_MODEL)
    gate_ref[...] = _sigmoid(gz).reshape(sb, tt, LANES)


def _nsa_in(x, mod, wq_bf, wg_bf):
    B, T, _ = x.shape
    sb, tt = _row_tiles(B, T)
    return pl.pallas_call(
        _nsa_in_kernel,
        grid=(B // sb, T // tt),
        in_specs=[_tok_spec(sb, tt), _mod_spec(sb, 0), _mod_spec(sb, 1),
                  pl.BlockSpec((D_MODEL, D_MODEL), lambda i, j: (0, 0)),
                  pl.BlockSpec((D_MODEL, LANES), lambda i, j: (0, 0))],
        out_specs=[_tok_spec(sb, tt), _tok_spec(sb, tt, LANES)],
        out_shape=[jax.ShapeDtypeStruct((B, T, D_MODEL), F32), jax.ShapeDtypeStruct((B, T, LANES), F32)],
        compiler_params=_cparams("arbitrary", "arbitrary"),
        name="nsa_in",
    )(x, mod, mod, wq_bf, wg_bf)


KV_RAW = 6 * B_KV_HEADS * B_DH
KV_DUP = 4 * B_KV_HEADS * LANES


def _kv_proj_kernel(x_ref, sh_ref, sc_ref, w_ref, kv_ref, dup_ref):
    sb, tt, d = x_ref.shape
    h = _modulate(x_ref, sc_ref, sh_ref).astype(BF16)
    kv = jnp.dot(h, w_ref[...], preferred_element_type=F32)
    kv_ref[...] = kv[:, :KV_RAW].reshape(sb, tt, KV_RAW)
    dup_ref[...] = kv[:, KV_RAW:].reshape(sb, tt, KV_DUP).astype(dup_ref.dtype)


def _kv_proj(x, kvmod, w_bf, dup_dtype):
    B, T, _ = x.shape
    sb, tt = _row_tiles(B, T)
    n_out = w_bf.shape[1]
    return pl.pallas_call(
        _kv_proj_kernel,
        grid=(B // sb, T // tt),
        in_specs=[_tok_spec(sb, tt), _mod_spec(sb, 0), _mod_spec(sb, 1),
                  pl.BlockSpec((D_MODEL, n_out), lambda i, j: (0, 0))],
        out_specs=[_tok_spec(sb, tt, KV_RAW), _tok_spec(sb, tt, KV_DUP)],
        out_shape=[jax.ShapeDtypeStruct((B, T, KV_RAW), F32), jax.ShapeDtypeStruct((B, T, KV_DUP), dup_dtype)],
        compiler_params=_cparams("arbitrary", "arbitrary"),
        name="kv_proj",
    )(x, kvmod, kvmod, w_bf)


def _layer_norm_rows(z, g, b):
    mu = jnp.mean(z, axis=-1, keepdims=True)
    zc = z - mu
    var = jnp.mean(zc * zc, axis=-1, keepdims=True)
    return zc * lax.rsqrt(var + LN_EPS) * g + b


def _mixer_out_kernel(o_ref, w_ref, x_ref, gm_ref, shf_ref, scf_ref, lng_ref, lnb_ref, wr_ref, br_ref,
                      xn_ref, hf_ref, te_ref, tg_ref):
    sb, tt, d = x_ref.shape
    rows = sb * tt
    o = o_ref[...].reshape(rows, d).astype(BF16)
    y = jnp.dot(o, w_ref[...], preferred_element_type=F32).reshape(sb, tt, d)
    z = DN_ALPHA * x_ref[...] + (1.0 + gm_ref[...]) * y
    xn = _layer_norm_rows(z, lng_ref[...], lnb_ref[...])
    xn_ref[...] = xn
    hf = xn * (1.0 + scf_ref[...]) + shf_ref[...]
    hf_ref[...] = hf
    logits = jnp.dot(hf.reshape(rows, d), wr_ref[...], precision=HIGHEST,
                     preferred_element_type=F32) + br_ref[...]
    lane = lax.broadcasted_iota(I32, (rows, LANES), 1)
    te = jnp.zeros((rows, LANES), I32)
    tv = jnp.zeros((rows, LANES), F32)
    v0 = None
    denom = None
    for k in range(TOP_K):
        m = jnp.max(logits, axis=-1, keepdims=True)
        idx = jnp.min(jnp.where(logits == m, lane, LANES), axis=-1, keepdims=True)
        if k == 0:
            v0 = m
            e = jnp.ones_like(m)
            denom = e
        else:
            e = jnp.exp(m - v0)
            denom = denom + e
        te = jnp.where(lane == k, idx, te)
        tv = jnp.where(lane == k, e, tv)
        logits = jnp.where(lane == idx, -jnp.inf, logits)
    te_ref[...] = te.reshape(sb, tt, LANES)
    tg_ref[...] = (tv / denom).reshape(sb, tt, LANES)


def _mixer_out(o, w_bf, x, mod, ln_g, ln_b, wr_pad, br_pad):
    B, T, _ = x.shape
    sb, tt = _row_tiles(B, T)
    vec = pl.BlockSpec((1, 1, D_MODEL), lambda i, j: (0, 0, 0))
    return pl.pallas_call(
        _mixer_out_kernel,
        grid=(B // sb, T // tt),
        in_specs=[_tok_spec(sb, tt), pl.BlockSpec((D_MODEL, D_MODEL), lambda i, j: (0, 0)), _tok_spec(sb, tt),
                  _mod_spec(sb, 2), _mod_spec(sb, 3), _mod_spec(sb, 4), vec, vec,
                  pl.BlockSpec((D_MODEL, LANES), lambda i, j: (0, 0)),
                  pl.BlockSpec((1, LANES), lambda i, j: (0, 0))],
        out_specs=[_tok_spec(sb, tt), _tok_spec(sb, tt), _tok_spec(sb, tt, LANES), _tok_spec(sb, tt, LANES)],
        out_shape=[jax.ShapeDtypeStruct((B, T, D_MODEL), F32), jax.ShapeDtypeStruct((B, T, D_MODEL), F32),
                   jax.ShapeDtypeStruct((B, T, LANES), I32), jax.ShapeDtypeStruct((B, T, LANES), F32)],
        compiler_params=_cparams("arbitrary", "arbitrary"),
        name="mixer_out",
    )(o, w_bf, x, mod, mod, mod, ln_g.reshape(1, 1, D_MODEL), ln_b.reshape(1, 1, D_MODEL), wr_pad, br_pad)


def _hgrn_scan_kernel(q_ref, k_ref, g_ref, v_ref, og_ref, s0_ref, gn_ref, o_ref, sout_ref,
                      st_ref, b_ref, k_scr, v_scr, *, chunk, n_chunks):
    tb = pl.program_id(1)
    C = chunk
    nsub = C // SUBLANES

    @pl.when(tb == 0)
    def _():
        for h in range(A_HEADS):
            st_ref[h] = s0_ref[0, h].T

    row = lax.broadcasted_iota(I32, (C, C), 0)
    col = lax.broadcasted_iota(I32, (C, C), 1)
    tri = (row >= col).astype(F32)
    sub_row = lax.broadcasted_iota(I32, (SUBLANES, A_DK), 0)
    gn = gn_ref[...]

    def chunk_body(c, carry):
        r0 = pl.multiple_of(c * C, C)
        for h in range(A_HEADS):
            ls = slice(h * A_DK, (h + 1) * A_DK)
            g = g_ref[0, pl.ds(r0, C), ls]
            b = jnp.dot(tri, g, precision=HIGHEST, preferred_element_type=F32)
            b_ref[...] = b
            q = q_ref[0, pl.ds(r0, C), ls]
            k = k_ref[0, pl.ds(r0, C), ls]
            v = v_ref[0, pl.ds(r0, C), ls]
            k_scr[...] = k
            v_scr[...] = v
            st = st_ref[h]
            o_inter = lax.dot_general((q * jnp.exp(b)).astype(BF16), st.astype(BF16),
                                      (((1,), (1,)), ((), ())), preferred_element_type=F32)
            acc = [jnp.zeros((SUBLANES, A_DV), F32) for _ in range(nsub)]
            for s in range(C):
                j = s // SUBLANES
                b_s = b_ref[pl.ds(s, 1), :]
                k_s = k_scr[pl.ds(s, 1), :]
                v_s = v_scr[pl.ds(s, 1), :]
                for i in range(j, nsub):
                    rs = slice(i * SUBLANES, (i + 1) * SUBLANES)
                    dlt = b[rs] - b_s
                    if i == j:
                        causal = sub_row >= (s % SUBLANES)
                        e = jnp.where(causal, jnp.exp(jnp.where(causal, dlt, 0.0)), 0.0)
                    else:
                        e = jnp.exp(dlt)
                    w = jnp.sum(q[rs] * e * k_s, axis=-1, keepdims=True)
                    acc[i] = acc[i] + w * v_s
            o = o_inter + (jnp.concatenate(acc, axis=0) if nsub > 1 else acc[0])
            b_last = b_ref[pl.ds(C - 1, 1), :]
            kd = k * jnp.exp(b_last - b)
            st_new = st * jnp.exp(b_last) + lax.dot_general(
                v.astype(BF16), kd.astype(BF16), (((0,), (0,)), ((), ())), preferred_element_type=F32)
            st_ref[h] = st_new
            og = og_ref[0, pl.ds(r0, C), ls]
            on = o * lax.rsqrt(jnp.mean(o * o, axis=-1, keepdims=True) + RMS_EPS) * gn
            o_ref[0, pl.ds(r0, C), ls] = on * (og * _sigmoid(og))
        return carry

    lax.fori_loop(0, n_chunks, chunk_body, 0)

    @pl.when(tb == pl.num_programs(1) - 1)
    def _():
        for h in range(A_HEADS):
            sout_ref[0, h] = st_ref[h].T


def _hgrn_scan(q, k, g, v, og, s0, gn):
    B, T, _ = q.shape
    C = HGRN_CHUNK if T % HGRN_CHUNK == 0 else T
    assert C % SUBLANES == 0
    tblk = min(T, ROW_TILE)
    assert T % tblk == 0 and tblk % C == 0
    tok = pl.BlockSpec((1, tblk, D_MODEL), lambda b, t: (b, t, 0))
    st = pl.BlockSpec((1, A_HEADS, A_DK, A_DV), lambda b, t: (b, 0, 0, 0))
    return pl.pallas_call(
        functools.partial(_hgrn_scan_kernel, chunk=C, n_chunks=tblk // C),
        grid=(B, T // tblk),
        in_specs=[tok, tok, tok, tok, tok, st, pl.BlockSpec((1, A_DV), lambda b, t: (0, 0))],
        out_specs=[tok, st],
        out_shape=[jax.ShapeDtypeStruct((B, T, D_MODEL), F32),
                   jax.ShapeDtypeStruct((B, A_HEADS, A_DK, A_DV), F32)],
        scratch_shapes=[pltpu.VMEM((A_HEADS, A_DV, A_DK), F32), pltpu.VMEM((C, A_DK), F32),
                        pltpu.VMEM((C, A_DK), F32), pltpu.VMEM((C, A_DV), F32)],
        compiler_params=_cparams("arbitrary", "arbitrary"),
        name="hgrn_scan",
    )(q, k, g, v, og, s0, gn.reshape(1, A_DV))


def _cmp_const_kernel(pos_ref, w1_ref, b1_ref, o_ref):
    pos8 = jnp.broadcast_to(pos_ref[...], (SUBLANES, pos_ref.shape[1]))
    o_ref[...] = jnp.dot(pos8, w1_ref[...], precision=HIGHEST, preferred_element_type=F32)[0:1] + b1_ref[...]


def _cmp_const(pos_flat, w1, b1):
    kdim = w1.shape[1]
    return pl.pallas_call(
        _cmp_const_kernel,
        grid=(2,),
        in_specs=[pl.BlockSpec((None, 1, kdim), lambda a: (a, 0, 0)),
                  pl.BlockSpec((None, kdim, CMP_HID), lambda a: (a, 0, 0)),
                  pl.BlockSpec((None, 1, CMP_HID), lambda a: (a, 0, 0))],
        out_specs=pl.BlockSpec((None, 1, CMP_HID), lambda a: (a, 0, 0)),
        out_shape=jax.ShapeDtypeStruct((2, 1, CMP_HID), F32),
        compiler_params=_cparams("arbitrary"),
        name="cmp_const",
    )(pos_flat, w1, b1)


def _compress_kernel(x_ref, w1_ref, cvec_ref, w2_ref, b2_ref, o_ref):
    gb, n, kdim = x_ref.shape
    rows = gb * n
    uv = jnp.dot(x_ref[...].reshape(rows, kdim).astype(BF16), w1_ref[...].astype(BF16),
                 preferred_element_type=F32)
    u = uv[:, :CMP_HID]
    v = uv[:, CMP_HID:]
    v_next = pltpu.roll(v, rows - 1, axis=0)
    hid = u + v_next + cvec_ref[...]
    act = hid * _sigmoid(hid)
    out = jnp.dot(act.astype(BF16), w2_ref[...].astype(BF16), preferred_element_type=F32) + b2_ref[...]
    o_ref[...] = out.reshape(gb, n, LANES)


def _compress(xc, w1cat, cvec, w2dup, b2dup):
    _, BG, nch, kdim = xc.shape
    gb = _largest_divisor(BG, (8, 4, 2, 1))
    return pl.pallas_call(
        _compress_kernel,
        grid=(2, BG // gb),
        in_specs=[pl.BlockSpec((None, gb, nch, kdim), lambda a, i: (a, i, 0, 0)),
                  pl.BlockSpec((None, kdim, 2 * CMP_HID), lambda a, i: (a, 0, 0)),
                  pl.BlockSpec((None, 1, CMP_HID), lambda a, i: (a, 0, 0)),
                  pl.BlockSpec((None, CMP_HID, LANES), lambda a, i: (a, 0, 0)),
                  pl.BlockSpec((None, 1, LANES), lambda a, i: (a, 0, 0))],
        out_specs=pl.BlockSpec((None, gb, nch, LANES), lambda a, i: (a, i, 0, 0)),
        out_shape=jax.ShapeDtypeStruct((2, BG, nch, LANES), F32),
        compiler_params=_cparams("arbitrary", "arbitrary"),
        name="compress_kv",
    )(xc, w1cat, cvec, w2dup, b2dup)


def _compress_paged_kernel(pt_ref, *refs, n_seq_pages):
    del pt_ref
    page_refs = refs[:n_seq_pages]
    w1_ref, cvec_ref, w2_ref, b2_ref, o_ref = refs[n_seq_pages:]
    x = jnp.concatenate([r[...] for r in page_refs], axis=0)
    rows = x.shape[0]
    uv = jnp.dot(x.astype(BF16), w1_ref[...].astype(BF16), preferred_element_type=F32)
    v_next = pltpu.roll(uv[:, CMP_HID:], rows - 1, axis=0)
    hid = uv[:, :CMP_HID] + v_next + cvec_ref[...]
    act = hid * _sigmoid(hid)
    out = jnp.dot(act.astype(BF16), w2_ref[...].astype(BF16), preferred_element_type=F32) + b2_ref[...]
    o_ref[...] = out.reshape(o_ref.shape)


def _compress_paged(pool_t, table, w1cat, cvec, w2dup, b2dup):
    _, G, _, cpp, kdim = pool_t.shape
    B, n_pages = table.shape
    nb = _largest_divisor(B, (2, 1))
    nch = n_pages * cpp
    page_specs = [pl.BlockSpec((None, None, None, cpp, kdim),
                               lambda a, g, i, pt, s=s, p=p: (a, g, pt[i * nb + s, p], 0, 0))
                  for s in range(nb) for p in range(n_pages)]
    grid_spec = pltpu.PrefetchScalarGridSpec(
        num_scalar_prefetch=1,
        grid=(2, G, B // nb),
        in_specs=page_specs + [pl.BlockSpec((None, kdim, 2 * CMP_HID), lambda a, g, i, pt: (a, 0, 0)),
                               pl.BlockSpec((None, 1, CMP_HID), lambda a, g, i, pt: (a, 0, 0)),
                               pl.BlockSpec((None, CMP_HID, LANES), lambda a, g, i, pt: (a, 0, 0)),
                               pl.BlockSpec((None, 1, LANES), lambda a, g, i, pt: (a, 0, 0))],
        out_specs=pl.BlockSpec((None, nb, None, nch, LANES), lambda a, g, i, pt: (a, i, g, 0, 0)),
    )
    return pl.pallas_call(
        functools.partial(_compress_paged_kernel, n_seq_pages=nb * n_pages),
        grid_spec=grid_spec,
        out_shape=jax.ShapeDtypeStruct((2, B, G, nch, LANES), F32),
        compiler_params=_cparams("arbitrary", "arbitrary", "arbitrary"),
        name="compress_paged",
    )(table, *([pool_t] * (nb * n_pages)), w1cat, cvec, w2dup, b2dup)


def _dup_tiles(x_ref, col, lo):
    out = []
    for m in range(B_KV_HEADS // 2):
        c0 = col * B_KV_HEADS * B_DH + m * LANES
        pair = x_ref[:, c0:c0 + LANES].astype(F32)
        swapped = pltpu.roll(pair, B_DH, axis=1)
        out.append(jnp.where(lo, pair, swapped))
        out.append(jnp.where(lo, swapped, pair))
    return out


def _pages_kernel(pt_ref, *refs, n_pages):
    del pt_ref
    page_refs = refs[:n_pages]
    newk_ref, newv_ref, kt_ref, v_ref = refs[n_pages:]
    lo = lax.broadcasted_iota(I32, (PAGE_SIZE, LANES), 1) < B_DH
    for p in range(n_pages):
        rows = slice(p * PAGE_SIZE, (p + 1) * PAGE_SIZE)
        kd = _dup_tiles(page_refs[p], 0, lo)
        vd = _dup_tiles(page_refs[p], 1, lo)
        for g in range(B_KV_HEADS):
            kt_ref[0, g, :, rows] = kd[g].T.astype(BF16)
            v_ref[0, rows, g * LANES:(g + 1) * LANES] = vd[g].astype(BF16)
    t_new = newk_ref.shape[1]
    tail = v_ref.shape[1] - n_pages * PAGE_SIZE
    pad = jnp.zeros((tail - t_new, newk_ref.shape[2]), F32)
    newk = jnp.concatenate([newk_ref[0], pad], axis=0)
    newv = jnp.concatenate([newv_ref[0], pad], axis=0)
    v_ref[0, n_pages * PAGE_SIZE:, :] = newv.astype(BF16)
    for g in range(B_KV_HEADS):
        for j in range(tail // PAGE_SIZE):
            blk = newk[j * PAGE_SIZE:(j + 1) * PAGE_SIZE, g * LANES:(g + 1) * LANES]
            c0 = (n_pages + j) * PAGE_SIZE
            kt_ref[0, g, :, c0:c0 + PAGE_SIZE] = blk.T.astype(BF16)


def _pages_to_kv(pool, table, new_dup, k_col, v_col, total_len):
    B, n_pages = table.shape
    T = new_dup.shape[1]
    assert total_len % PAGE_SIZE == 0 and total_len >= n_pages * PAGE_SIZE + T
    width = B_KV_HEADS * LANES
    page_specs = [pl.BlockSpec((None, PAGE_SIZE, width), lambda b, pt, p=p: (pt[b, p], 0, 0))
                  for p in range(n_pages)]
    grid_spec = pltpu.PrefetchScalarGridSpec(
        num_scalar_prefetch=1,
        grid=(B,),
        in_specs=page_specs + [pl.BlockSpec((1, T, width), lambda b, pt, c=k_col: (b, 0, c)),
                               pl.BlockSpec((1, T, width), lambda b, pt, c=v_col: (b, 0, c))],
        out_specs=[pl.BlockSpec((1, B_KV_HEADS, LANES, total_len), lambda b, pt: (b, 0, 0, 0)),
                   pl.BlockSpec((1, total_len, width), lambda b, pt: (b, 0, 0))],
    )
    return pl.pallas_call(
        functools.partial(_pages_kernel, n_pages=n_pages),
        grid_spec=grid_spec,
        out_shape=[jax.ShapeDtypeStruct((B, B_KV_HEADS, LANES, total_len), BF16),
                   jax.ShapeDtypeStruct((B, total_len, width), BF16)],
        compiler_params=_cparams("arbitrary"),
        name="pages_to_kv",
    )(table, *([pool] * n_pages), new_dup, new_dup)


def _cmp_to_slc_map_t(ncp, ns):
    j = np.arange(LANES)[:, None]
    i = np.arange(ncp)[None, :]
    lo = np.maximum(i * CMP_STRIDE, j * SLC_BLOCK)
    hi = np.minimum(i * CMP_STRIDE + CMP_BLOCK, (j + 1) * SLC_BLOCK)
    m = np.maximum(hi - lo, 0) / CMP_STRIDE
    return jnp.asarray(np.where(j < ns, m, 0.0), dtype=F32)


def _aug_table_t(length, pos0, with_blocks):
    pos = np.arange(length) + pos0
    a = np.zeros((LANES, length), np.float32)
    a[0] = (pos // MXU_DIM) * MXU_DIM
    a[1] = pos % MXU_DIM
    a[2] = a[0]
    a[3] = a[1]
    if with_blocks:
        blk = np.arange(length) // SLC_BLOCK
        assert AUG_BLK_LANE0 + blk.max() < LANES
        a[AUG_BLK_LANE0 + blk, np.arange(length)] = 1.0
    return jnp.asarray(a, dtype=BF16)


def _split_bf16(x):
    hi = float(np.float32(x).astype(BF16).astype(np.float32))
    lo = float(np.float32(x - hi).astype(BF16).astype(np.float32))
    return hi, lo


def _nsa_attn_kernel(q_ref, gate_ref, ck_ref, cv_ref, cmapt_ref, kts_ref, vs_ref, ktw_ref, vw_ref,
                     augs_ref, augw_ref, o_ref, qa_ref, m_ref, acc_ref, ocmp_ref, sc_ref,
                     *, tq, gpar, pos0, win_pos0, nc, ns, n_sel):
    qi = pl.program_id(1)
    R = B_GROUP
    grows = R * tq
    rows = gpar * grows
    chunk = min(grows, LANES)
    n_chunk = rows // chunk
    ncp = ck_ref.shape[2]
    nsp = sc_ref.shape[0]
    n_slc_tiles = vs_ref.shape[1] // KEY_TILE
    n_win_tiles = vw_ref.shape[1] // KEY_TILE
    half = B_DH
    lane_q = lax.broadcasted_iota(I32, (tq, LANES), 1)
    lo_q = lane_q < half
    ones_lane = lax.broadcasted_iota(I32, (KEY_TILE, LANES), 1) == half
    t_idx = lax.broadcasted_iota(I32, (tq, 1), 0)
    q_lo = pos0 + qi * tq
    q_hi = q_lo + tq - 1
    qpos_i = q_lo + t_idx
    qpos_g = jnp.concatenate([qpos_i] * R, axis=0)
    qpos_all = jnp.concatenate([qpos_g] * gpar, axis=0) if gpar > 1 else qpos_g
    qpos_f = qpos_g.astype(F32)
    kidx = lax.broadcasted_iota(I32, (1, KEY_TILE), 1)
    blk_t = lax.broadcasted_iota(I32, (nsp, tq), 0)
    qpos_t = q_lo + lax.broadcasted_iota(I32, (nsp, tq), 1)
    place = (lax.broadcasted_iota(I32, (nsp, LANES), 1) ==
             lax.broadcasted_iota(I32, (nsp, LANES), 0) + AUG_BLK_LANE0).astype(BF16)
    sel_lanes = (lane_q >= AUG_BLK_LANE0) & (lane_q < AUG_BLK_LANE0 + ns)

    def flash_init():
        m_ref[...] = jnp.full(m_ref.shape, NEG_INF, F32)
        acc_ref[...] = jnp.zeros(acc_ref.shape, F32)

    def flash_tile(kbts, vbs, mask_fn):
        cs = range(n_chunk)
        rs = [pl.ds(c * chunk, chunk) for c in cs]
        gi = [(c * chunk) // grows for c in cs]
        ss = []
        for c in cs:
            s = jnp.dot(qa_ref[rs[c], :], kbts[gi[c]], preferred_element_type=F32)
            if mask_fn is not None:
                s = jnp.where(mask_fn(qpos_all[c * chunk:(c + 1) * chunk]), s, NEG_INF)
            ss.append(s)
        m_old = [m_ref[rs[c], :] for c in cs]
        m_new = [jnp.maximum(m_old[c], jnp.broadcast_to(jnp.max(ss[c], axis=-1, keepdims=True), (chunk, LANES)))
                 for c in cs]
        es = [jnp.exp(ss[c] - jnp.concatenate([m_new[c]] * (KEY_TILE // LANES), axis=1)) for c in cs]
        for c in cs:
            alpha = jnp.exp(m_old[c] - m_new[c])
            acc_ref[rs[c], :] = alpha * acc_ref[rs[c], :] + jnp.dot(es[c].astype(BF16), vbs[gi[c]],
                                                                   preferred_element_type=F32)
            m_ref[rs[c], :] = m_new[c]

    def flash_done():
        acc = acc_ref[...]
        return acc / acc[:, half:half + 1]

    for g0 in range(0, B_KV_HEADS, gpar):
        groups = list(range(g0, g0 + gpar))
        for gi, g in enumerate(groups):
            c0 = q_ref[0, :, g * 2 * LANES:g * 2 * LANES + LANES]
            c1 = q_ref[0, :, g * 2 * LANES + LANES:(g + 1) * 2 * LANES]
            q_st = jnp.concatenate([jnp.where(lo_q, c0, 0.0), jnp.where(lo_q, 0.0, c0),
                                    jnp.where(lo_q, c1, 0.0), jnp.where(lo_q, 0.0, c1)], axis=0).astype(BF16)
            slopes = [2.0 ** (-8.0 * (g * R + r + 1) / B_HEADS) for r in range(R)]
            slope = jnp.concatenate([jnp.full((tq, 1), s, F32) for s in slopes], axis=0)

            ckb = ck_ref[0, g].astype(BF16)
            cvb = cv_ref[0, g].astype(BF16)
            s = lax.dot_general(q_st, ckb, (((1,), (1,)), ((), ())), preferred_element_type=F32)
            c_iota = lax.broadcasted_iota(I32, (1, ncp), 1)
            c_end = (c_iota * CMP_STRIDE + (CMP_BLOCK - 1)).astype(F32)
            dist = qpos_f - c_end
            mask = (dist >= 0.0) & (c_iota < nc)
            s = jnp.where(mask, s - slope * dist, NEG_INF)
            m = jnp.max(s, axis=-1, keepdims=True)
            e = jnp.where(mask, jnp.exp(s - m), 0.0)
            p = e / jnp.maximum(jnp.sum(e, axis=-1, keepdims=True), 1e-30)
            ocmp_ref[gi * grows:(gi + 1) * grows, :] = jnp.dot(p.astype(BF16), cvb, preferred_element_type=F32)

            p_sum = p[0:tq] + p[tq:2 * tq] + p[2 * tq:3 * tq] + p[3 * tq:4 * tq]
            imp_t = lax.dot_general(cmapt_ref[...], p_sum, (((1,), (1,)), ((), ())), precision=HIGHEST,
                                    preferred_element_type=F32)[0:nsp]
            cur = qpos_t // SLC_BLOCK
            forced = (blk_t == 0) | (blk_t == cur) | (blk_t == cur - 1)
            score = jnp.where(forced, FORCE_SCORE, jnp.where(blk_t * SLC_BLOCK <= qpos_t, imp_t, -1.0))
            score = jnp.where(blk_t < ns, score, -jnp.inf)
            sc_ref[...] = score
            rank = jnp.zeros((nsp, tq), F32)
            for j in range(ns):
                other = sc_ref[pl.ds(j, 1), :]
                ahead = (other > score) | ((other == score) & (j < blk_t))
                rank = rank + ahead.astype(F32)
            sel_t = ((rank < float(n_sel)) & (score >= 0.0)).astype(BF16)
            sel_q = lax.dot_general(sel_t, place, (((0,), (0,)), ((), ())), preferred_element_type=F32)
            sel_term = jnp.where(sel_q > 0.5, 0.0, UNSELECTED)

            q_hi_half = []
            for r in range(R):
                s_hi, s_lo = _split_bf16(slopes[r])
                consts = jnp.where(lane_q < 2, s_hi, jnp.where(lane_q < AUG_POS_LANES, s_lo, 0.0))
                q_hi_half.append(jnp.where(sel_lanes, sel_term, consts))
            qa_ref[gi * grows:(gi + 1) * grows, :] = jnp.concatenate(
                [q_st, jnp.concatenate(q_hi_half, axis=0).astype(BF16)], axis=1)

        def key_tiles(kt_ref, v_ref, aug_ref, t0):
            kbts = [jnp.concatenate([kt_ref[0, g, :, pl.ds(t0, KEY_TILE)], aug_ref[:, pl.ds(t0, KEY_TILE)]], axis=0)
                    for g in groups]
            vbs = [jnp.where(ones_lane, 1.0, v_ref[0, pl.ds(t0, KEY_TILE), g * LANES:(g + 1) * LANES]).astype(BF16)
                   for g in groups]
            return kbts, vbs

        n_kt = jnp.minimum(q_hi // KEY_TILE + 1, n_slc_tiles)
        n_full = jnp.minimum((q_lo + 1) // KEY_TILE, n_kt)

        def slc_tile(kt, masked):
            t0 = pl.multiple_of(kt * KEY_TILE, KEY_TILE)
            kbts, vbs = key_tiles(kts_ref, vs_ref, augs_ref, t0)
            flash_tile(kbts, vbs, (lambda qp: (t0 + kidx) <= qp) if masked else None)

        flash_init()
        lax.fori_loop(0, n_full, lambda kt, c: (slc_tile(kt, False), c)[1], 0)
        lax.fori_loop(n_full, n_kt, lambda kt, c: (slc_tile(kt, True), c)[1], 0)
        o_slc = flash_done()

        w_first = jnp.maximum(q_lo - (WINDOW - 1) - win_pos0, 0) // KEY_TILE
        w_last = jnp.minimum((q_hi - win_pos0) // KEY_TILE + 1, n_win_tiles)

        def win_tile(kt, c):
            t0 = pl.multiple_of(kt * KEY_TILE, KEY_TILE)
            kbts, vbs = key_tiles(ktw_ref, vw_ref, augw_ref, t0)
            kpos = win_pos0 + t0 + kidx
            flash_tile(kbts, vbs, lambda qp: (kpos <= qp) & (kpos > qp - WINDOW))
            return c

        flash_init()
        lax.fori_loop(w_first, w_last, win_tile, 0)
        o_win = flash_done()

        for gi, g in enumerate(groups):
            gr = slice(gi * grows, (gi + 1) * grows)

            def gate_col(j):
                base = j * B_HEADS + g * R
                return jnp.concatenate([gate_ref[0, :, base + r:base + r + 1] for r in range(R)], axis=0)

            o_st = gate_col(0) * ocmp_ref[gr, :] + gate_col(1) * o_slc[gr] + gate_col(2) * o_win[gr]
            o_up = pltpu.roll(o_st, half, axis=1)
            o_ref[0, :, g * 2 * LANES:g * 2 * LANES + LANES] = jnp.where(lo_q, o_st[0:tq], o_up[tq:2 * tq])
            o_ref[0, :, g * 2 * LANES + LANES:(g + 1) * 2 * LANES] = jnp.where(lo_q, o_st[2 * tq:3 * tq], o_up[3 * tq:])


def _nsa_attn(q, gates, ck, cv, kt_slc, v_slc, v_slc_col, kt_win, v_win, v_win_col, *, pos0, win_pos0, nc, ns):
    B, T, _ = q.shape
    tq = min(T, LANES)
    assert T % tq == 0
    gpar = B_KV_HEADS if B_GROUP * tq < LANES else 1
    ncp = ck.shape[2]
    n_sel = min(N_SELECT, ns)
    nsp = -(-ns // (2 * SUBLANES)) * (2 * SUBLANES)
    Ls, Lw = kt_slc.shape[3], kt_win.shape[3]
    assert Ls % KEY_TILE == 0 and Lw % KEY_TILE == 0 and v_slc.shape[1] == Ls and v_win.shape[1] == Lw
    rows = gpar * B_GROUP * tq
    width = B_KV_HEADS * LANES
    kern = functools.partial(_nsa_attn_kernel, tq=tq, gpar=gpar, pos0=pos0, win_pos0=win_pos0, nc=nc, ns=ns,
                             n_sel=n_sel)
    kt_spec = lambda L: pl.BlockSpec((1, B_KV_HEADS, LANES, L), lambda b, i: (b, 0, 0, 0))
    v_spec = lambda L, col: pl.BlockSpec((1, L, width), lambda b, i, c=col: (b, 0, c))
    return pl.pallas_call(
        kern,
        grid=(B, T // tq),
        in_specs=[pl.BlockSpec((1, tq, D_MODEL), lambda b, i: (b, i, 0)),
                  pl.BlockSpec((1, tq, LANES), lambda b, i: (b, i, 0)),
                  pl.BlockSpec((1, B_KV_HEADS, ncp, LANES), lambda b, i: (b, 0, 0, 0)),
                  pl.BlockSpec((1, B_KV_HEADS, ncp, LANES), lambda b, i: (b, 0, 0, 0)),
                  pl.BlockSpec((LANES, ncp), lambda b, i: (0, 0)),
                  kt_spec(Ls), v_spec(Ls, v_slc_col), kt_spec(Lw), v_spec(Lw, v_win_col),
                  pl.BlockSpec((LANES, Ls), lambda b, i: (0, 0)),
                  pl.BlockSpec((LANES, Lw), lambda b, i: (0, 0))],
        out_specs=pl.BlockSpec((1, tq, D_MODEL), lambda b, i: (b, i, 0)),
        out_shape=jax.ShapeDtypeStruct((B, T, D_MODEL), F32),
        scratch_shapes=[pltpu.VMEM((rows, 2 * LANES), BF16), pltpu.VMEM((rows, LANES), F32),
                        pltpu.VMEM((rows, LANES), F32), pltpu.VMEM((rows, LANES), F32),
                        pltpu.VMEM((nsp, tq), F32)],
        compiler_params=_cparams("arbitrary", "arbitrary"),
        name="nsa_attn",
    )(q, gates, ck, cv, _cmp_to_slc_map_t(ncp, ns), kt_slc, v_slc, kt_win, v_win,
      _aug_table_t(Ls, 0, True), _aug_table_t(Lw, win_pos0, False))


def _deinterleave_kernel(w_ref, perm_ref, glu_ref, lin_ref):
    for c in range(w_ref.shape[1] // MXU_DIM):
        t = jnp.dot(w_ref[:, c * MXU_DIM:(c + 1) * MXU_DIM].astype(BF16), perm_ref[...],
                    preferred_element_type=F32)
        glu_ref[:, c * LANES:(c + 1) * LANES] = t[:, :LANES].astype(BF16)
        lin_ref[:, c * LANES:(c + 1) * LANES] = t[:, LANES:].astype(BF16)


def _deinterleave_up(w_up):
    E, K, N2 = w_up.shape
    rt = 512
    i = np.arange(MXU_DIM)[:, None]
    j = np.arange(MXU_DIM)[None, :]
    perm = jnp.asarray(((i % 2) * LANES + i // 2) == j, dtype=BF16)
    return pl.pallas_call(
        _deinterleave_kernel,
        grid=(E, K // rt),
        in_specs=[pl.BlockSpec((None, rt, N2), lambda e, r: (e, r, 0)),
                  pl.BlockSpec((MXU_DIM, MXU_DIM), lambda e, r: (0, 0))],
        out_specs=[pl.BlockSpec((None, rt, N2 // 2), lambda e, r: (e, r, 0))] * 2,
        out_shape=[jax.ShapeDtypeStruct((E, K, N2 // 2), BF16)] * 2,
        compiler_params=_cparams("arbitrary", "arbitrary"),
        name="deinterleave_up",
    )(w_up, perm)


def _route_kernel(te_ref, start_ref, pos_ref, cnt_ref):
    rt = te_ref.shape[0]

    @pl.when(pl.program_id(0) == 0)
    def _():
        cnt_ref[...] = start_ref[...]

    te = te_ref[...]
    lane = lax.broadcasted_iota(I32, (rt, LANES), 1)
    onehots = [(lane == te[:, k:k + 1]).astype(F32) for k in range(TOP_K)]
    total = onehots[0] + onehots[1] + onehots[2] + onehots[3]
    r = lax.broadcasted_iota(I32, (rt, rt), 0)
    c = lax.broadcasted_iota(I32, (rt, rt), 1)
    earlier = (c < r).astype(BF16)
    base = cnt_ref[...] + jnp.dot(earlier, total.astype(BF16), preferred_element_type=F32)
    pos = jnp.zeros((rt, LANES), I32)
    for k in range(TOP_K):
        pk = jnp.sum(onehots[k] * base, axis=-1, keepdims=True)
        pos = jnp.where(lane == k, pk.astype(I32), pos)
    pos_ref[...] = pos
    cnt_ref[...] = cnt_ref[...] + jnp.sum(total, axis=0, keepdims=True)


def _route(te, start):
    N = te.shape[0]
    rt = _largest_divisor(N, (256, 128, 64, 32, 16, 8))
    return pl.pallas_call(
        _route_kernel,
        grid=(N // rt,),
        in_specs=[pl.BlockSpec((rt, LANES), lambda i: (i, 0)), pl.BlockSpec((1, LANES), lambda i: (0, 0))],
        out_specs=pl.BlockSpec((rt, LANES), lambda i: (i, 0)),
        out_shape=jax.ShapeDtypeStruct((N, LANES), I32),
        scratch_shapes=[pltpu.VMEM((1, LANES), F32)],
        compiler_params=_cparams("arbitrary"),
        name="moe_route",
    )(te, start)


def _row_copy_wait(src_rows, dst_rows, sem):
    pltpu.make_async_copy(src_rows, dst_rows, sem).wait()


def _dispatch_kernel(pos_ref, x_ref, xr_in_ref, xr_ref, sem):
    del xr_in_ref
    rt = x_ref.shape[0]

    def body(i, carry):
        for k in range(TOP_K):
            dst = pos_ref[i * TOP_K + k]
            pltpu.make_async_copy(x_ref.at[pl.ds(i, 1)], xr_ref.at[pl.ds(dst, 1)], sem).start(priority=k % 2)
        return carry

    lax.fori_loop(0, rt, body, 0)
    for k in range(TOP_K):
        _row_copy_wait(x_ref, xr_ref.at[pl.ds(0, rt)], sem)


def _dispatch(x, pos_flat, m_pad):
    N, D = x.shape
    rt = _largest_divisor(N, (256, 128, 64, 32, 16, 8))
    return pl.pallas_call(
        _dispatch_kernel,
        grid=(N // rt,),
        in_specs=[pl.BlockSpec((rt * TOP_K,), lambda i: (i,), memory_space=pltpu.SMEM),
                  pl.BlockSpec((rt, D), lambda i: (i, 0)),
                  pl.BlockSpec(memory_space=pl.ANY)],
        out_specs=pl.BlockSpec(memory_space=pl.ANY),
        out_shape=jax.ShapeDtypeStruct((m_pad, D), F32),
        scratch_shapes=[pltpu.SemaphoreType.DMA(())],
        input_output_aliases={2: 0},
        compiler_params=_cparams("arbitrary"),
        name="moe_dispatch",
    )(pos_flat, x, jnp.zeros((m_pad, D), F32))


def _moe_kernel(be_ref, nv_ref, x_ref, wg_ref, wl_ref, bg_ref, bl_ref, wd_ref, bd_ref, o_ref):
    i = pl.program_id(0)

    @pl.when(i < nv_ref[0])
    def _():
        x = x_ref[...].astype(BF16)
        glu = jnp.dot(x, wg_ref[...], preferred_element_type=F32) + bg_ref[...]
        lin = jnp.dot(x, wl_ref[...], preferred_element_type=F32) + bl_ref[...]
        glu = jnp.minimum(glu, SWIGLU_LIMIT)
        lin = jnp.clip(lin, -SWIGLU_LIMIT, SWIGLU_LIMIT)
        act = glu * _sigmoid(SWIGLU_ALPHA * glu) * (lin + 1.0)
        o_ref[...] = jnp.dot(act.astype(BF16), wd_ref[...], preferred_element_type=F32) + bd_ref[...]

    @pl.when(i >= nv_ref[0])
    def _():
        o_ref[...] = jnp.zeros_like(o_ref)


def _moe_ffn_blocks(xr, blk_exp, nvalid, wg, wl, bg, bl, wd, bd):
    m_pad = xr.shape[0]
    n_blk = m_pad // MOE_ROWS
    wspec = lambda cols: pl.BlockSpec((None, D_MODEL, cols), lambda i, be, nv: (be[i], 0, 0))
    bspec = lambda cols: pl.BlockSpec((None, 1, cols), lambda i, be, nv: (be[i], 0, 0))
    grid_spec = pltpu.PrefetchScalarGridSpec(
        num_scalar_prefetch=2,
        grid=(n_blk,),
        in_specs=[pl.BlockSpec((MOE_ROWS, D_MODEL), lambda i, be, nv: (i, 0)),
                  wspec(D_FF), wspec(D_FF), bspec(D_FF), bspec(D_FF), wspec(D_MODEL), bspec(D_MODEL)],
        out_specs=pl.BlockSpec((MOE_ROWS, D_MODEL), lambda i, be, nv: (i, 0)),
    )
    return pl.pallas_call(
        _moe_kernel,
        grid_spec=grid_spec,
        out_shape=jax.ShapeDtypeStruct((m_pad, D_MODEL), F32),
        compiler_params=_cparams("arbitrary"),
        name="moe_ffn",
    )(blk_exp, nvalid, xr, wg, wl, bg, bl, wd, bd)


def _combine_kernel(pos_ref, x_ref, tg_ref, gf_ref, lng_ref, lnb_ref, yr_ref, xn_ref, ybuf, sem):
    sb, tt, d = x_ref.shape
    rows = sb * tt

    def body(i, carry):
        for k in range(TOP_K):
            src = pos_ref[i * TOP_K + k]
            pltpu.make_async_copy(yr_ref.at[pl.ds(src, 1)], ybuf.at[k, pl.ds(i, 1)], sem).start(priority=k % 2)
        return carry

    lax.fori_loop(0, rows, body, 0)
    for k in range(TOP_K):
        _row_copy_wait(yr_ref.at[pl.ds(0, rows)], ybuf.at[k], sem)
    tg = tg_ref[...].reshape(rows, LANES)
    y = tg[:, 0:1] * ybuf[0]
    for k in range(1, TOP_K):
        y = y + tg[:, k:k + 1] * ybuf[k]
    z = DN_ALPHA * x_ref[...] + (1.0 + gf_ref[...]) * y.reshape(sb, tt, d)
    xn_ref[...] = _layer_norm_rows(z, lng_ref[...], lnb_ref[...])


def _combine(x, yr, pos_flat, tg, mod, ln_g, ln_b):
    B, T, _ = x.shape
    sb, tt = _row_tiles(B, T)
    rows = sb * tt
    nj = T // tt
    vec = pl.BlockSpec((1, 1, D_MODEL), lambda i, j: (0, 0, 0))
    return pl.pallas_call(
        _combine_kernel,
        grid=(B // sb, nj),
        in_specs=[pl.BlockSpec((rows * TOP_K,), lambda i, j: (i * nj + j,), memory_space=pltpu.SMEM),
                  _tok_spec(sb, tt), _tok_spec(sb, tt, LANES), _mod_spec(sb, 5), vec, vec,
                  pl.BlockSpec(memory_space=pl.ANY)],
        out_specs=_tok_spec(sb, tt),
        out_shape=jax.ShapeDtypeStruct((B, T, D_MODEL), F32),
        scratch_shapes=[pltpu.VMEM((TOP_K, rows, D_MODEL), F32), pltpu.SemaphoreType.DMA(())],
        compiler_params=_cparams("arbitrary", "arbitrary"),
        name="moe_combine",
    )(pos_flat, x, tg, mod, ln_g.reshape(1, 1, D_MODEL), ln_b.reshape(1, 1, D_MODEL), yr)


def _moe_plan(te_all):
    N = te_all.shape[0]
    M = N * TOP_K
    onehot = (te_all[:, :TOP_K, None] == jnp.arange(N_EXPERTS, dtype=I32)[None, None, :])
    counts = jnp.sum(onehot.astype(I32), axis=(0, 1))
    padded = (counts + MOE_ROWS - 1) // MOE_ROWS * MOE_ROWS
    pend = jnp.cumsum(padded)
    pstart = pend - padded
    n_blk = -(-(M + N_EXPERTS * (MOE_ROWS - 1)) // MOE_ROWS)
    blk_first = jnp.arange(n_blk, dtype=I32)[:, None] * MOE_ROWS
    blk_exp = jnp.minimum(jnp.sum((pend[None, :] <= blk_first).astype(I32), axis=1), N_EXPERTS - 1).astype(I32)
    nvalid = (pend[-1] // MOE_ROWS).astype(I32).reshape(1)
    start = jnp.pad(pstart.astype(F32), (0, LANES - N_EXPERTS)).reshape(1, LANES)
    return start, blk_exp, nvalid, n_blk * MOE_ROWS


def _compress_inputs(cmp_raw):
    B, L, _ = cmp_raw.shape
    nch = L // CMP_STRIDE
    x = cmp_raw[:, :nch * CMP_STRIDE].reshape(B, nch, CMP_STRIDE, 2, B_KV_HEADS, B_DH)
    x = x.transpose(3, 0, 4, 1, 2, 5)
    return x.reshape(2, B * B_KV_HEADS, nch, CMP_STRIDE * B_DH), nch


def _pad_rows(a, mult, axis=1):
    n = a.shape[axis]
    pad = (-n) % mult
    if pad == 0:
        return a
    widths = [(0, 0)] * a.ndim
    widths[axis] = (0, pad)
    return jnp.pad(a, widths)


def kernel(x_prompt, x_sample, state_hgrn, cache_cmp_kv, cache_slc_kv, state_win_kv, page_table, c_prompt, c_sample,
           ada_w, ada_b, ln_g, ln_b, a_w_in, a_lb, a_onorm_g, a_w_o, kv_ada_w, kv_ada_b, kv_w, cmp_pos, cmp_w1, cmp_b1,
           cmp_w2, cmp_b2, b_w_in, b_w_o, moe_wr, moe_br, moe_w_up, moe_b_up, moe_w_down, moe_b_down):
    Bp, Tp, _ = x_prompt.shape
    Bs, Ts, _ = x_sample.shape
    n_pages = page_table.shape[1]
    past_len = n_pages * PAGE_SIZE
    keep_s = state_win_kv.shape[1]
    G, DH = B_KV_HEADS, B_DH
    GD = G * DH
    Np, Ns = Bp * Tp, Bs * Ts
    assert keep_s % PAGE_SIZE == 0

    a_w_in_bf = a_w_in.astype(BF16)
    a_w_o_bf = a_w_o.astype(BF16)
    b_w_o_bf = b_w_o.astype(BF16)
    dup_cols = [jnp.concatenate([kv_w[:, c * GD + g * DH:c * GD + (g + 1) * DH]] * 2, axis=1)
                for c in range(2, 6) for g in range(G)]
    kv_w_bf = jnp.concatenate([kv_w] + dup_cols, axis=1).astype(BF16)
    wq_bf = b_w_in[:, :, :D_MODEL].astype(BF16)
    n_l = b_w_in.shape[0]
    wg = b_w_in[:, :, D_MODEL:].reshape(n_l, D_MODEL, B_HEADS, 3).transpose(0, 1, 3, 2).reshape(n_l, D_MODEL, 3 * B_HEADS)
    wg_bf = jnp.pad(wg, ((0, 0), (0, 0), (0, LANES - 3 * B_HEADS))).astype(BF16)
    wr_pad = jnp.pad(moe_wr, ((0, 0), (0, 0), (0, LANES - N_EXPERTS)))
    br_pad = jnp.pad(moe_br, ((0, 0), (0, LANES - N_EXPERTS)), constant_values=NEG_INF).reshape(DEPTH, 1, LANES)
    w_glu, w_lin = _deinterleave_up(moe_w_up.reshape(DEPTH * N_EXPERTS, D_MODEL, 2 * D_FF))
    w_glu = w_glu.reshape(DEPTH, N_EXPERTS, D_MODEL, D_FF)
    w_lin = w_lin.reshape(DEPTH, N_EXPERTS, D_MODEL, D_FF)
    b_glu = moe_b_up[..., 0::2].reshape(DEPTH, N_EXPERTS, 1, D_FF)
    b_lin = moe_b_up[..., 1::2].reshape(DEPTH, N_EXPERTS, 1, D_FF)
    w_dn = moe_w_down.astype(BF16)
    b_dn = moe_b_down.reshape(DEPTH, N_EXPERTS, 1, D_MODEL)
    lb_p = jax.nn.softmax(a_lb.astype(F32), axis=0)
    lb_all = jnp.maximum(jnp.cumsum(lb_p, axis=0) - lb_p[0], 0.0)
    w1cat = jnp.concatenate([cmp_w1[:, :CMP_STRIDE * DH], cmp_w1[:, CMP_STRIDE * DH:]], axis=-1)
    cvec = _cmp_const(cmp_pos.reshape(2, 1, CMP_BLOCK * DH), cmp_w1, cmp_b1.reshape(2, 1, CMP_HID))
    w2dup = jnp.concatenate([cmp_w2, cmp_w2], axis=-1)
    b2dup = jnp.concatenate([cmp_b2, cmp_b2], axis=-1).reshape(2, 1, LANES)

    n_c = Bp + Bs
    c_all = _pad_rows(jnp.concatenate([c_prompt, c_sample], axis=0), SUBLANES, axis=0)
    mod_all = _silu_linear(c_all, ada_w, ada_b)
    kvmod_all = _silu_linear(c_all, kv_ada_w[None], kv_ada_b[None])[0]
    mods = [(mod_all[l, :Bp].reshape(Bp, 1, 6 * D_MODEL), mod_all[l, Bp:n_c].reshape(Bs, 1, 6 * D_MODEL))
            for l in range(DEPTH)]
    kvmods = (kvmod_all[:Bp].reshape(Bp, 1, 2 * D_MODEL), kvmod_all[Bp:n_c].reshape(Bs, 1, 2 * D_MODEL))

    xs = [x_prompt, x_sample]
    s0s = [jnp.zeros((N_A_LAYERS, Bp, A_HEADS, A_DK, A_DV), F32), state_hgrn]
    hgrn_out = [[], []]
    ctx = [None, None]
    new_kv = [None, None]

    def keys_t(dup, col):
        b, length, _ = dup.shape
        return dup[:, :, col * G * LANES:(col + 1) * G * LANES].reshape(b, length, G, LANES).transpose(0, 2, 3, 1)

    def build_ctx(t, x):
        B, T, _ = x.shape
        if t == 0:
            kv, dup = _kv_proj(x, kvmods[t], kv_w_bf, BF16)
            cmp_raw = kv[:, :, :2 * GD]
            if T % KEY_TILE:
                dup = _pad_rows(dup, KEY_TILE)
            attn_kv = (keys_t(dup, 0), dup, 1, keys_t(dup, 2), dup, 3)
            L, pos0, win_pos0 = T, 0, 0
            win_out = kv[:, T - min(WINDOW, T):, 4 * GD:].reshape(B, -1, 2, G, DH)
            xc, nch = _compress_inputs(cmp_raw)
            ckv = _compress(xc, w1cat, cvec, w2dup, b2dup).reshape(2, B, G, nch, LANES)
        else:
            kv, dup = _kv_proj(x, kvmods[t], kv_w_bf, F32)
            L = past_len + T
            nch = L // CMP_STRIDE
            assert nch * CMP_STRIDE <= past_len and T <= keep_s
            cpp = PAGE_SIZE // CMP_STRIDE
            pool_t = cache_cmp_kv.reshape(-1, cpp, CMP_STRIDE, 2, G, DH).transpose(3, 4, 0, 1, 2, 5)
            pool_t = pool_t.reshape(2, G, -1, cpp, CMP_STRIDE * DH)
            ckv = _compress_paged(pool_t, page_table, w1cat, cvec, w2dup, b2dup)[:, :, :, :nch]
            ls_pad = -(-L // KEY_TILE) * KEY_TILE
            lw_pad = -(-(keep_s + T) // KEY_TILE) * KEY_TILE
            kt_slc, v_slc = _pages_to_kv(cache_slc_kv.reshape(-1, PAGE_SIZE, 2 * GD).astype(BF16), page_table, dup, 0, 1,
                                         ls_pad)
            n_wp = keep_s // PAGE_SIZE
            win_table = jnp.arange(B * n_wp, dtype=I32).reshape(B, n_wp)
            kt_win, v_win = _pages_to_kv(state_win_kv.reshape(B * n_wp, PAGE_SIZE, 2 * GD).astype(BF16), win_table, dup, 2, 3,
                                         lw_pad)
            attn_kv = (kt_slc, v_slc, 0, kt_win, v_win, 0)
            pos0 = past_len
            win_pos0 = past_len - keep_s
            win_out = jnp.concatenate([state_win_kv[:, T:], kv[:, :, 4 * GD:].reshape(B, T, 2, G, DH)], axis=1)
        nc = nch - CMP_RATIO + 1
        ns = -(-L // SLC_BLOCK)
        ckv = _pad_rows(ckv, LANES, axis=3)
        info = dict(ck=ckv[0], cv=ckv[1], attn_kv=attn_kv, pos0=pos0, win_pos0=win_pos0, nc=nc, ns=ns)
        outs = (kv[:, :, :2 * GD].reshape(B, T, 2, G, DH), kv[:, :, 2 * GD:4 * GD].reshape(B, T, 2, G, DH),
                win_out)
        return info, outs

    for l in range(DEPTH):
        o_mix = []
        for t in range(2):
            x = xs[t]
            mod = mods[l][t]
            if l < N_A_LAYERS:
                q, k, g, v, og = _hgrn_in(x, mod, a_w_in_bf[l], lb_all[l])
                o, S = _hgrn_scan(q, k, g, v, og, s0s[t][l], a_onorm_g[l])
                hgrn_out[t].append(S)
            else:
                j = l - N_A_LAYERS
                if ctx[t] is None:
                    ctx[t], new_kv[t] = build_ctx(t, x)
                c = ctx[t]
                q, gates = _nsa_in(x, mod, wq_bf[j], wg_bf[j])
                o = _nsa_attn(q, gates, c['ck'], c['cv'], *c['attn_kv'],
                              pos0=c['pos0'], win_pos0=c['win_pos0'], nc=c['nc'], ns=c['ns'])
            o_mix.append(o)
        w_o = a_w_o_bf[l] if l < N_A_LAYERS else b_w_o_bf[l - N_A_LAYERS]
        hf, te, tg = [], [], []
        for t in range(2):
            xn, h, e, gt = _mixer_out(o_mix[t], w_o, xs[t], mods[l][t], ln_g[l, 0], ln_b[l, 0], wr_pad[l], br_pad[l])
            xs[t] = xn
            hf.append(h.reshape(-1, D_MODEL))
            te.append(e.reshape(-1, LANES))
            tg.append(gt)
        te_all = jnp.concatenate(te, axis=0)
        start, blk_exp, nvalid, m_pad = _moe_plan(te_all)
        pos_flat = _route(te_all, start)[:, :TOP_K].reshape(-1)
        xr = _dispatch(jnp.concatenate(hf, axis=0), pos_flat, m_pad)
        yr = _moe_ffn_blocks(xr, blk_exp, nvalid, w_glu[l], w_lin[l], b_glu[l], b_lin[l], w_dn[l], b_dn[l])
        pos_t = [pos_flat[:Np * TOP_K], pos_flat[Np * TOP_K:]]
        for t in range(2):
            xs[t] = _combine(xs[t], yr, pos_t[t], tg[t], mods[l][t], ln_g[l, 1], ln_b[l, 1])

    return (xs[0], xs[1], jnp.stack(hgrn_out[0], axis=0), jnp.stack(hgrn_out[1], axis=0),
            new_kv[0][0], new_kv[1][0], new_kv[0][1], new_kv[1][1], new_kv[0][2], new_kv[1][2])
```

```python
import functools

import numpy as np
import jax
import jax.numpy as jnp
from jax import lax
from jax.experimental import pallas as pl
from jax.experimental.pallas import tpu as pltpu

F32 = jnp.float32
BF16 = jnp.bfloat16
I32 = jnp.int32
HIGHEST = lax.Precision.HIGHEST

D_MODEL = 1024
DEPTH = 4
PAGE_SIZE = 128
N_A_LAYERS = DEPTH // 2
A_HEADS = 8
A_DK = 128
A_DV = D_MODEL // A_HEADS
A_DF = A_HEADS * A_DK
HGRN_CHUNK = 32
B_HEADS = 16
B_DH = D_MODEL // B_HEADS
B_KV_HEADS = 4
B_GROUP = B_HEADS // B_KV_HEADS
CMP_BLOCK = 32
CMP_STRIDE = 16
CMP_RATIO = CMP_BLOCK // CMP_STRIDE
CMP_HID = 2 * B_DH
SLC_BLOCK = 64
N_SELECT = 16
WINDOW = 512
FORCE_SCORE = 1e4
N_EXPERTS = 32
TOP_K = 4
D_FF = D_MODEL
SWIGLU_LIMIT = 7.0
SWIGLU_ALPHA = 1.702
DN_ALPHA = (2 * DEPTH) ** 0.25
LN_EPS = 1e-5
RMS_EPS = 1e-6
NEG_INF = -1e30

LANES = 128
SUBLANES = 8
MXU_DIM = 256
ROW_TILE = 256
MOE_ROWS = 256
KEY_TILE = 512
VMEM_LIMIT = 56 * 1024 * 1024

AUG_POS_LANES = 4
AUG_BLK_LANE0 = 8
UNSELECTED = -(2.0 ** 30)


def _cparams(*sem):
    return pltpu.CompilerParams(dimension_semantics=sem, vmem_limit_bytes=VMEM_LIMIT)


def _row_tiles(B, T):
    if T >= ROW_TILE:
        assert T % ROW_TILE == 0
        return 1, ROW_TILE
    sb = max(1, ROW_TILE // T)
    while B % sb:
        sb //= 2
    return sb, T


def _largest_divisor(n, candidates):
    for c in candidates:
        if n % c == 0:
            return c
    raise ValueError(n)


def _sigmoid(x):
    return 1.0 / (1.0 + jnp.exp(-x))


def _linear_kernel(x_ref, w_ref, b_ref, o_ref):
    x = x_ref[...]
    x = x * _sigmoid(x)
    o_ref[...] = jnp.dot(x.astype(BF16), w_ref[...].astype(BF16), preferred_element_type=F32) + b_ref[...]


def _silu_linear(x, w, b):
    L, K, N = w.shape
    M = x.shape[0]
    tn = 1024
    return pl.pallas_call(
        _linear_kernel,
        grid=(L, N // tn),
        in_specs=[pl.BlockSpec((M, K), lambda l, j: (0, 0)),
                  pl.BlockSpec((None, K, tn), lambda l, j: (l, 0, j)),
                  pl.BlockSpec((None, 1, tn), lambda l, j: (l, 0, j))],
        out_specs=pl.BlockSpec((None, M, tn), lambda l, j: (l, 0, j)),
        out_shape=jax.ShapeDtypeStruct((L, M, N), F32),
        compiler_params=_cparams("arbitrary", "arbitrary"),
        name="silu_linear",
    )(x, w, b.reshape(L, 1, N))


def _modulate(x_ref, sc_ref, sh_ref):
    sb, tt, d = x_ref.shape
    h = x_ref[...] * (1.0 + sc_ref[...]) + sh_ref[...]
    return h.reshape(sb * tt, d)


def _hgrn_in_kernel(x_ref, sh_ref, sc_ref, w_ref, lb_ref, q_ref, k_ref, g_ref, v_ref, og_ref):
    sb, tt, d = x_ref.shape
    h = _modulate(x_ref, sc_ref, sh_ref).astype(BF16)
    proj = jnp.dot(h, w_ref[...], preferred_element_type=F32)
    qz = proj[:, :A_DF]
    fz = proj[:, A_DF:2 * A_DF]
    lb = lb_ref[...]
    log_sig = jnp.minimum(fz, 0.0) - jnp.log1p(jnp.exp(-jnp.abs(fz)))
    pos = lb > 0.0
    log_lb = jnp.where(pos, jnp.log(jnp.where(pos, lb, 1.0)), NEG_INF)
    other = jnp.log1p(-lb) + log_sig
    logf = jnp.maximum(log_lb, other) + jnp.log1p(jnp.exp(-jnp.abs(log_lb - other)))
    q_ref[...] = (qz * _sigmoid(qz)).reshape(sb, tt, A_DF)
    g_ref[...] = logf.reshape(sb, tt, A_DF)
    k_ref[...] = ((1.0 - lb) * _sigmoid(-fz)).reshape(sb, tt, A_DF)
    v_ref[...] = proj[:, 2 * A_DF:2 * A_DF + D_MODEL].reshape(sb, tt, D_MODEL)
    og_ref[...] = proj[:, 2 * A_DF + D_MODEL:].reshape(sb, tt, D_MODEL)


def _mod_spec(sb, comp):
    return pl.BlockSpec((sb, 1, D_MODEL), lambda i, j, c=comp: (i, 0, c))


def _tok_spec(sb, tt, width=D_MODEL):
    return pl.BlockSpec((sb, tt, width), lambda i, j: (i, j, 0))


def _hgrn_in(x, mod, w_bf, lb):
    B, T, _ = x.shape
    sb, tt = _row_tiles(B, T)
    n_out = w_bf.shape[1]
    shp = jax.ShapeDtypeStruct((B, T, D_MODEL), F32)
    return pl.pallas_call(
        _hgrn_in_kernel,
        grid=(B // sb, T // tt),
        in_specs=[_tok_spec(sb, tt), _mod_spec(sb, 0), _mod_spec(sb, 1),
                  pl.BlockSpec((D_MODEL, n_out), lambda i, j: (0, 0)),
                  pl.BlockSpec((1, A_DF), lambda i, j: (0, 0))],
        out_specs=[_tok_spec(sb, tt)] * 5,
        out_shape=[shp] * 5,
        compiler_params=_cparams("arbitrary", "arbitrary"),
        name="hgrn_in",
    )(x, mod, mod, w_bf, lb.reshape(1, A_DF))


def _nsa_in_kernel(x_ref, sh_ref, sc_ref, wq_ref, wg_ref, q_ref, gate_ref):
    sb, tt, d = x_ref.shape
    h = _modulate(x_ref, sc_ref, sh_ref).astype(BF16)
    q = jnp.dot(h, wq_ref[...], preferred_element_type=F32) * (B_DH ** -0.5)
    gz = jnp.dot(h, wg_ref[...], preferred_element_type=F32)
    q_ref[...] = q.reshape(sb, tt, D_MODEL)
    gate_ref[...] = _sigmoid(gz).reshape(sb, tt, LANES)


def _nsa_in(x, mod, wq_bf, wg_bf):
    B, T, _ = x.shape
    sb, tt = _row_tiles(B, T)
    return pl.pallas_call(
        _nsa_in_kernel,
        grid=(B // sb, T // tt),
        in_specs=[_tok_spec(sb, tt), _mod_spec(sb, 0), _mod_spec(sb, 1),
                  pl.BlockSpec((D_MODEL, D_MODEL), lambda i, j: (0, 0)),
                  pl.BlockSpec((D_MODEL, LANES), lambda i, j: (0, 0))],
        out_specs=[_tok_spec(sb, tt), _tok_spec(sb, tt, LANES)],
        out_shape=[jax.ShapeDtypeStruct((B, T, D_MODEL), F32), jax.ShapeDtypeStruct((B, T, LANES), F32)],
        compiler_params=_cparams("arbitrary", "arbitrary"),
        name="nsa_in",
    )(x, mod, mod, wq_bf, wg_bf)


KV_RAW = 6 * B_KV_HEADS * B_DH
KV_DUP = 4 * B_KV_HEADS * LANES


def _kv_proj_kernel(x_ref, sh_ref, sc_ref, w_ref, kv_ref, dup_ref):
    sb, tt, d = x_ref.shape
    h = _modulate(x_ref, sc_ref, sh_ref).astype(BF16)
    kv = jnp.dot(h, w_ref[...], preferred_element_type=F32)
    kv_ref[...] = kv[:, :KV_RAW].reshape(sb, tt, KV_RAW)
    dup_ref[...] = kv[:, KV_RAW:].reshape(sb, tt, KV_DUP).astype(dup_ref.dtype)


def _kv_proj(x, kvmod, w_bf, dup_dtype):
    B, T, _ = x.shape
    sb, tt = _row_tiles(B, T)
    n_out = w_bf.shape[1]
    return pl.pallas_call(
        _kv_proj_kernel,
        grid=(B // sb, T // tt),
        in_specs=[_tok_spec(sb, tt), _mod_spec(sb, 0), _mod_spec(sb, 1),
                  pl.BlockSpec((D_MODEL, n_out), lambda i, j: (0, 0))],
        out_specs=[_tok_spec(sb, tt, KV_RAW), _tok_spec(sb, tt, KV_DUP)],
        out_shape=[jax.ShapeDtypeStruct((B, T, KV_RAW), F32), jax.ShapeDtypeStruct((B, T, KV_DUP), dup_dtype)],
        compiler_params=_cparams("arbitrary", "arbitrary"),
        name="kv_proj",
    )(x, kvmod, kvmod, w_bf)


def _layer_norm_rows(z, g, b):
    mu = jnp.mean(z, axis=-1, keepdims=True)
    zc = z - mu
    var = jnp.mean(zc * zc, axis=-1, keepdims=True)
    return zc * lax.rsqrt(var + LN_EPS) * g + b


def _mixer_out_kernel(o_ref, w_ref, x_ref, gm_ref, shf_ref, scf_ref, lng_ref, lnb_ref, wr_ref, br_ref,
                      xn_ref, hf_ref, te_ref, tg_ref):
    sb, tt, d = x_ref.shape
    rows = sb * tt
    o = o_ref[...].reshape(rows, d).astype(BF16)
    y = jnp.dot(o, w_ref[...], preferred_element_type=F32).reshape(sb, tt, d)
    z = DN_ALPHA * x_ref[...] + (1.0 + gm_ref[...]) * y
    xn = _layer_norm_rows(z, lng_ref[...], lnb_ref[...])
    xn_ref[...] = xn
    hf = xn * (1.0 + scf_ref[...]) + shf_ref[...]
    bits = pltpu.bitcast(hf.astype(BF16).astype(F32).reshape(rows, d), jnp.uint32)
    hf_ref[...] = (bits[:, :d // 2] | (bits[:, d // 2:] >> 16)).reshape(sb, tt, d // 2)
    logits = jnp.dot(hf.reshape(rows, d), wr_ref[...], precision=HIGHEST,
                     preferred_element_type=F32) + br_ref[...]
    lane = lax.broadcasted_iota(I32, (rows, LANES), 1)
    te = jnp.zeros((rows, LANES), I32)
    tv = jnp.zeros((rows, LANES), F32)
    v0 = None
    denom = None
    for k in range(TOP_K):
        m = jnp.max(logits, axis=-1, keepdims=True)
        idx = jnp.min(jnp.where(logits == m, lane, LANES), axis=-1, keepdims=True)
        if k == 0:
            v0 = m
            e = jnp.ones_like(m)
            denom = e
        else:
            e = jnp.exp(m - v0)
            denom = denom + e
        te = jnp.where(lane == k, idx, te)
        tv = jnp.where(lane == k, e, tv)
        logits = jnp.where(lane == idx, -jnp.inf, logits)
    te_ref[...] = te.reshape(sb, tt, LANES)
    tg_ref[...] = (tv / denom).reshape(sb, tt, LANES)


def _mixer_out(o, w_bf, x, mod, ln_g, ln_b, wr_pad, br_pad):
    B, T, _ = x.shape
    sb, tt = _row_tiles(B, T)
    vec = pl.BlockSpec((1, 1, D_MODEL), lambda i, j: (0, 0, 0))
    return pl.pallas_call(
        _mixer_out_kernel,
        grid=(B // sb, T // tt),
        in_specs=[_tok_spec(sb, tt), pl.BlockSpec((D_MODEL, D_MODEL), lambda i, j: (0, 0)), _tok_spec(sb, tt),
                  _mod_spec(sb, 2), _mod_spec(sb, 3), _mod_spec(sb, 4), vec, vec,
                  pl.BlockSpec((D_MODEL, LANES), lambda i, j: (0, 0)),
                  pl.BlockSpec((1, LANES), lambda i, j: (0, 0))],
        out_specs=[_tok_spec(sb, tt), _tok_spec(sb, tt, D_MODEL // 2), _tok_spec(sb, tt, LANES), _tok_spec(sb, tt, LANES)],
        out_shape=[jax.ShapeDtypeStruct((B, T, D_MODEL), F32), jax.ShapeDtypeStruct((B, T, D_MODEL // 2), jnp.uint32),
                   jax.ShapeDtypeStruct((B, T, LANES), I32), jax.ShapeDtypeStruct((B, T, LANES), F32)],
        compiler_params=_cparams("arbitrary", "arbitrary"),
        name="mixer_out",
    )(o, w_bf, x, mod, mod, mod, ln_g.reshape(1, 1, D_MODEL), ln_b.reshape(1, 1, D_MODEL), wr_pad, br_pad)


def _hgrn_scan_kernel(q_ref, k_ref, g_ref, v_ref, og_ref, s0_ref, gn_ref, o_ref, sout_ref,
                      st_ref, b_ref, k_scr, v_scr, *, chunk, n_chunks):
    tb = pl.program_id(1)
    C = chunk
    nsub = C // SUBLANES

    @pl.when(tb == 0)
    def _():
        for h in range(A_HEADS):
            st_ref[h] = s0_ref[0, h].T

    row = lax.broadcasted_iota(I32, (C, C), 0)
    col = lax.broadcasted_iota(I32, (C, C), 1)
    tri = (row >= col).astype(F32)
    sub_row = lax.broadcasted_iota(I32, (SUBLANES, A_DK), 0)
    gn = gn_ref[...]

    def chunk_body(c, carry):
        r0 = pl.multiple_of(c * C, C)
        for h in range(A_HEADS):
            ls = slice(h * A_DK, (h + 1) * A_DK)
            g = g_ref[0, pl.ds(r0, C), ls]
            b = jnp.dot(tri, g, precision=HIGHEST, preferred_element_type=F32)
            b_ref[...] = b
            q = q_ref[0, pl.ds(r0, C), ls]
            k = k_ref[0, pl.ds(r0, C), ls]
            v = v_ref[0, pl.ds(r0, C), ls]
            k_scr[...] = k
            v_scr[...] = v
            st = st_ref[h]
            o_inter = lax.dot_general((q * jnp.exp(b)).astype(BF16), st.astype(BF16),
                                      (((1,), (1,)), ((), ())), preferred_element_type=F32)
            acc = [jnp.zeros((SUBLANES, A_DV), F32) for _ in range(nsub)]
            for s in range(C):
                j = s // SUBLANES
                b_s = b_ref[pl.ds(s, 1), :]
                k_s = k_scr[pl.ds(s, 1), :]
                v_s = v_scr[pl.ds(s, 1), :]
                for i in range(j, nsub):
                    rs = slice(i * SUBLANES, (i + 1) * SUBLANES)
                    dlt = b[rs] - b_s
                    if i == j:
                        causal = sub_row >= (s % SUBLANES)
                        e = jnp.where(causal, jnp.exp(jnp.where(causal, dlt, 0.0)), 0.0)
                    else:
                        e = jnp.exp(dlt)
                    w = jnp.sum(q[rs] * e * k_s, axis=-1, keepdims=True)
                    acc[i] = acc[i] + w * v_s
            o = o_inter + (jnp.concatenate(acc, axis=0) if nsub > 1 else acc[0])
            b_last = b_ref[pl.ds(C - 1, 1), :]
            kd = k * jnp.exp(b_last - b)
            st_new = st * jnp.exp(b_last) + lax.dot_general(
                v.astype(BF16), kd.astype(BF16), (((0,), (0,)), ((), ())), preferred_element_type=F32)
            st_ref[h] = st_new
            og = og_ref[0, pl.ds(r0, C), ls]
            on = o * lax.rsqrt(jnp.mean(o * o, axis=-1, keepdims=True) + RMS_EPS) * gn
            o_ref[0, pl.ds(r0, C), ls] = on * (og * _sigmoid(og))
        return carry

    lax.fori_loop(0, n_chunks, chunk_body, 0)

    @pl.when(tb == pl.num_programs(1) - 1)
    def _():
        for h in range(A_HEADS):
            sout_ref[0, h] = st_ref[h].T


def _hgrn_scan(q, k, g, v, og, s0_all, layer, gn):
    B, T, _ = q.shape
    C = HGRN_CHUNK if T % HGRN_CHUNK == 0 else T
    assert C % SUBLANES == 0
    tblk = min(T, ROW_TILE)
    assert T % tblk == 0 and tblk % C == 0
    tok = pl.BlockSpec((1, tblk, D_MODEL), lambda b, t: (b, t, 0))
    st = pl.BlockSpec((1, A_HEADS, A_DK, A_DV), lambda b, t: (b, 0, 0, 0))
    st_in = pl.BlockSpec((None, 1, A_HEADS, A_DK, A_DV), lambda b, t: (layer, b, 0, 0, 0))
    return pl.pallas_call(
        functools.partial(_hgrn_scan_kernel, chunk=C, n_chunks=tblk // C),
        grid=(B, T // tblk),
        in_specs=[tok, tok, tok, tok, tok, st_in, pl.BlockSpec((1, A_DV), lambda b, t: (0, 0))],
        out_specs=[tok, st],
        out_shape=[jax.ShapeDtypeStruct((B, T, D_MODEL), F32),
                   jax.ShapeDtypeStruct((B, A_HEADS, A_DK, A_DV), F32)],
        scratch_shapes=[pltpu.VMEM((A_HEADS, A_DV, A_DK), F32), pltpu.VMEM((C, A_DK), F32),
                        pltpu.VMEM((C, A_DK), F32), pltpu.VMEM((C, A_DV), F32)],
        compiler_params=_cparams("arbitrary", "arbitrary"),
        name="hgrn_scan",
    )(q, k, g, v, og, s0_all, gn.reshape(1, A_DV))


def _cmp_const_kernel(pos_ref, w1_ref, b1_ref, o_ref):
    pos8 = jnp.broadcast_to(pos_ref[...], (SUBLANES, pos_ref.shape[1]))
    o_ref[...] = jnp.dot(pos8, w1_ref[...], precision=HIGHEST, preferred_element_type=F32)[0:1] + b1_ref[...]


def _cmp_const(pos_flat, w1, b1):
    kdim = w1.shape[1]
    return pl.pallas_call(
        _cmp_const_kernel,
        grid=(2,),
        in_specs=[pl.BlockSpec((None, 1, kdim), lambda a: (a, 0, 0)),
                  pl.BlockSpec((None, kdim, CMP_HID), lambda a: (a, 0, 0)),
                  pl.BlockSpec((None, 1, CMP_HID), lambda a: (a, 0, 0))],
        out_specs=pl.BlockSpec((None, 1, CMP_HID), lambda a: (a, 0, 0)),
        out_shape=jax.ShapeDtypeStruct((2, 1, CMP_HID), F32),
        compiler_params=_cparams("arbitrary"),
        name="cmp_const",
    )(pos_flat, w1, b1)


def _compress_kernel(x_ref, w1_ref, cvec_ref, w2_ref, b2_ref, o_ref):
    gb, n, kdim = x_ref.shape
    rows = gb * n
    uv = jnp.dot(x_ref[...].reshape(rows, kdim).astype(BF16), w1_ref[...].astype(BF16),
                 preferred_element_type=F32)
    u = uv[:, :CMP_HID]
    v = uv[:, CMP_HID:]
    v_next = pltpu.roll(v, rows - 1, axis=0)
    hid = u + v_next + cvec_ref[...]
    act = hid * _sigmoid(hid)
    out = jnp.dot(act.astype(BF16), w2_ref[...].astype(BF16), preferred_element_type=F32) + b2_ref[...]
    o_ref[...] = out.reshape(gb, n, LANES)


def _compress(xc, w1cat, cvec, w2dup, b2dup):
    _, BG, nch, kdim = xc.shape
    gb = _largest_divisor(BG, (8, 4, 2, 1))
    return pl.pallas_call(
        _compress_kernel,
        grid=(2, BG // gb),
        in_specs=[pl.BlockSpec((None, gb, nch, kdim), lambda a, i: (a, i, 0, 0)),
                  pl.BlockSpec((None, kdim, 2 * CMP_HID), lambda a, i: (a, 0, 0)),
                  pl.BlockSpec((None, 1, CMP_HID), lambda a, i: (a, 0, 0)),
                  pl.BlockSpec((None, CMP_HID, LANES), lambda a, i: (a, 0, 0)),
                  pl.BlockSpec((None, 1, LANES), lambda a, i: (a, 0, 0))],
        out_specs=pl.BlockSpec((None, gb, nch, LANES), lambda a, i: (a, i, 0, 0)),
        out_shape=jax.ShapeDtypeStruct((2, BG, nch, LANES), F32),
        compiler_params=_cparams("arbitrary", "arbitrary"),
        name="compress_kv",
    )(xc, w1cat, cvec, w2dup, b2dup)


def _compress_paged_kernel(pt_ref, *refs, n_seq_pages):
    del pt_ref
    page_refs = refs[:n_seq_pages]
    w1_ref, cvec_ref, w2_ref, b2_ref, o_ref = refs[n_seq_pages:]
    x = jnp.concatenate([r[...] for r in page_refs], axis=0)
    rows = x.shape[0]
    uv = jnp.dot(x.astype(BF16), w1_ref[...].astype(BF16), preferred_element_type=F32)
    v_next = pltpu.roll(uv[:, CMP_HID:], rows - 1, axis=0)
    hid = uv[:, :CMP_HID] + v_next + cvec_ref[...]
    act = hid * _sigmoid(hid)
    out = jnp.dot(act.astype(BF16), w2_ref[...].astype(BF16), preferred_element_type=F32) + b2_ref[...]
    o_ref[...] = out.reshape(o_ref.shape)


def _compress_paged(pool_t, table, w1cat, cvec, w2dup, b2dup):
    _, G, _, cpp, kdim = pool_t.shape
    B, n_pages = table.shape
    nb = _largest_divisor(B, (2, 1))
    nch = n_pages * cpp
    page_specs = [pl.BlockSpec((None, None, None, cpp, kdim),
                               lambda a, g, i, pt, s=s, p=p: (a, g, pt[i * nb + s, p], 0, 0))
                  for s in range(nb) for p in range(n_pages)]
    grid_spec = pltpu.PrefetchScalarGridSpec(
        num_scalar_prefetch=1,
        grid=(2, G, B // nb),
        in_specs=page_specs + [pl.BlockSpec((None, kdim, 2 * CMP_HID), lambda a, g, i, pt: (a, 0, 0)),
                               pl.BlockSpec((None, 1, CMP_HID), lambda a, g, i, pt: (a, 0, 0)),
                               pl.BlockSpec((None, CMP_HID, LANES), lambda a, g, i, pt: (a, 0, 0)),
                               pl.BlockSpec((None, 1, LANES), lambda a, g, i, pt: (a, 0, 0))],
        out_specs=pl.BlockSpec((None, nb, None, nch, LANES), lambda a, g, i, pt: (a, i, g, 0, 0)),
    )
    return pl.pallas_call(
        functools.partial(_compress_paged_kernel, n_seq_pages=nb * n_pages),
        grid_spec=grid_spec,
        out_shape=jax.ShapeDtypeStruct((2, B, G, nch, LANES), F32),
        compiler_params=_cparams("arbitrary", "arbitrary", "arbitrary"),
        name="compress_paged",
    )(table, *([pool_t] * (nb * n_pages)), w1cat, cvec, w2dup, b2dup)


def _pages_kernel(pt_ref, *refs, n_pages):
    del pt_ref
    page_refs = refs[:n_pages]
    newk_ref, newv_ref, kt_ref, v_ref = refs[n_pages:]
    for p in range(n_pages):
        rows = slice(p * PAGE_SIZE, (p + 1) * PAGE_SIZE)
        page = page_refs[p]
        for g in range(B_KV_HEADS):
            kt = page[0, g]
            kt_ref[0, g, :, rows] = jnp.concatenate([kt, kt], axis=0)
            vt = jnp.concatenate([page[1, g], page[1, g ^ 1]], axis=0).astype(F32)
            v_ref[0, rows, g * LANES:(g + 1) * LANES] = vt.T.astype(BF16)
    t_new = newk_ref.shape[1]
    tail = v_ref.shape[1] - n_pages * PAGE_SIZE
    pad = jnp.zeros((tail - t_new, newk_ref.shape[2]), F32)
    newk = jnp.concatenate([newk_ref[0], pad], axis=0)
    newv = jnp.concatenate([newv_ref[0], pad], axis=0)
    v_ref[0, n_pages * PAGE_SIZE:, :] = newv.astype(BF16)
    for g in range(B_KV_HEADS):
        for j in range(tail // PAGE_SIZE):
            blk = newk[j * PAGE_SIZE:(j + 1) * PAGE_SIZE, g * LANES:(g + 1) * LANES]
            c0 = (n_pages + j) * PAGE_SIZE
            kt_ref[0, g, :, c0:c0 + PAGE_SIZE] = blk.T.astype(BF16)


def _pages_to_kv(pool, table, new_dup, k_col, v_col, total_len):
    B, n_pages = table.shape
    T = new_dup.shape[1]
    assert total_len % PAGE_SIZE == 0 and total_len >= n_pages * PAGE_SIZE + T
    width = B_KV_HEADS * LANES
    page_specs = [pl.BlockSpec((None, 2, B_KV_HEADS, B_DH, PAGE_SIZE), lambda b, pt, p=p: (pt[b, p], 0, 0, 0, 0))
                  for p in range(n_pages)]
    grid_spec = pltpu.PrefetchScalarGridSpec(
        num_scalar_prefetch=1,
        grid=(B,),
        in_specs=page_specs + [pl.BlockSpec((1, T, width), lambda b, pt, c=k_col: (b, 0, c)),
                               pl.BlockSpec((1, T, width), lambda b, pt, c=v_col: (b, 0, c))],
        out_specs=[pl.BlockSpec((1, B_KV_HEADS, LANES, total_len), lambda b, pt: (b, 0, 0, 0)),
                   pl.BlockSpec((1, total_len, width), lambda b, pt: (b, 0, 0))],
    )
    return pl.pallas_call(
        functools.partial(_pages_kernel, n_pages=n_pages),
        grid_spec=grid_spec,
        out_shape=[jax.ShapeDtypeStruct((B, B_KV_HEADS, LANES, total_len), BF16),
                   jax.ShapeDtypeStruct((B, total_len, width), BF16)],
        compiler_params=_cparams("arbitrary"),
        name="pages_to_kv",
    )(table, *([pool] * n_pages), new_dup, new_dup)


def _cmp_to_slc_map_t(ncp, ns):
    j = np.arange(LANES)[:, None]
    i = np.arange(ncp)[None, :]
    lo = np.maximum(i * CMP_STRIDE, j * SLC_BLOCK)
    hi = np.minimum(i * CMP_STRIDE + CMP_BLOCK, (j + 1) * SLC_BLOCK)
    m = np.maximum(hi - lo, 0) / CMP_STRIDE
    return jnp.asarray(np.where(j < ns, m, 0.0), dtype=F32)


def _aug_table_t(length, pos0, with_blocks):
    pos = np.arange(length) + pos0
    a = np.zeros((LANES, length), np.float32)
    a[0] = (pos // MXU_DIM) * MXU_DIM
    a[1] = pos % MXU_DIM
    a[2] = a[0]
    a[3] = a[1]
    if with_blocks:
        blk = np.arange(length) // SLC_BLOCK
        assert AUG_BLK_LANE0 + blk.max() < LANES
        a[AUG_BLK_LANE0 + blk, np.arange(length)] = 1.0
    return jnp.asarray(a, dtype=BF16)


def _split_bf16(x):
    hi = float(np.float32(x).astype(BF16).astype(np.float32))
    lo = float(np.float32(x - hi).astype(BF16).astype(np.float32))
    return hi, lo


def _nsa_attn_kernel(q_ref, gate_ref, ck_ref, cv_ref, cmapt_ref, kts_ref, vs_ref, ktw_ref, vw_ref,
                     augs_ref, augw_ref, o_ref, qa_ref, m_ref, acc_ref, ocmp_ref, sc_ref,
                     *, tq, gpar, pos0, win_pos0, nc, ns, n_sel):
    qi = pl.program_id(1)
    R = B_GROUP
    grows = R * tq
    rows = gpar * grows
    chunk = min(grows, LANES)
    n_chunk = rows // chunk
    ncp = ck_ref.shape[2]
    nsp = sc_ref.shape[0]
    n_slc_tiles = vs_ref.shape[1] // KEY_TILE
    n_win_tiles = vw_ref.shape[1] // KEY_TILE
    half = B_DH
    lane_q = lax.broadcasted_iota(I32, (tq, LANES), 1)
    lo_q = lane_q < half
    ones_lane = lax.broadcasted_iota(I32, (KEY_TILE, LANES), 1) == half
    t_idx = lax.broadcasted_iota(I32, (tq, 1), 0)
    q_lo = pos0 + qi * tq
    q_hi = q_lo + tq - 1
    qpos_i = q_lo + t_idx
    qpos_g = jnp.concatenate([qpos_i] * R, axis=0)
    qpos_all = jnp.concatenate([qpos_g] * gpar, axis=0) if gpar > 1 else qpos_g
    qpos_f = qpos_g.astype(F32)
    kidx = lax.broadcasted_iota(I32, (1, KEY_TILE), 1)
    blk_t = lax.broadcasted_iota(I32, (nsp, tq), 0)
    qpos_t = q_lo + lax.broadcasted_iota(I32, (nsp, tq), 1)
    place = (lax.broadcasted_iota(I32, (nsp, LANES), 1) ==
             lax.broadcasted_iota(I32, (nsp, LANES), 0) + AUG_BLK_LANE0).astype(BF16)
    sel_lanes = (lane_q >= AUG_BLK_LANE0) & (lane_q < AUG_BLK_LANE0 + ns)

    def flash_init():
        m_ref[...] = jnp.full(m_ref.shape, NEG_INF, F32)
        acc_ref[...] = jnp.zeros(acc_ref.shape, F32)

    def flash_tile(kbts, vbs, mask_fn):
        cs = range(n_chunk)
        rs = [pl.ds(c * chunk, chunk) for c in cs]
        gi = [(c * chunk) // grows for c in cs]
        ss = []
        for c in cs:
            s = jnp.dot(qa_ref[rs[c], :], kbts[gi[c]], preferred_element_type=F32)
            if mask_fn is not None:
                s = jnp.where(mask_fn(qpos_all[c * chunk:(c + 1) * chunk]), s, NEG_INF)
            ss.append(s)
        m_old = [m_ref[rs[c], :] for c in cs]
        m_new = [jnp.maximum(m_old[c], jnp.broadcast_to(jnp.max(ss[c], axis=-1, keepdims=True), (chunk, LANES)))
                 for c in cs]
        es = [jnp.exp(ss[c] - jnp.concatenate([m_new[c]] * (KEY_TILE // LANES), axis=1)) for c in cs]
        for c in cs:
            alpha = jnp.exp(m_old[c] - m_new[c])
            acc_ref[rs[c], :] = alpha * acc_ref[rs[c], :] + jnp.dot(es[c].astype(BF16), vbs[gi[c]],
                                                                   preferred_element_type=F32)
            m_ref[rs[c], :] = m_new[c]

    def flash_done():
        acc = acc_ref[...]
        return acc / acc[:, half:half + 1]

    for g0 in range(0, B_KV_HEADS, gpar):
        groups = list(range(g0, g0 + gpar))
        for gi, g in enumerate(groups):
            c0 = q_ref[0, :, g * 2 * LANES:g * 2 * LANES + LANES]
            c1 = q_ref[0, :, g * 2 * LANES + LANES:(g + 1) * 2 * LANES]
            q_st = jnp.concatenate([jnp.where(lo_q, c0, 0.0), jnp.where(lo_q, 0.0, c0),
                                    jnp.where(lo_q, c1, 0.0), jnp.where(lo_q, 0.0, c1)], axis=0).astype(BF16)
            slopes = [2.0 ** (-8.0 * (g * R + r + 1) / B_HEADS) for r in range(R)]
            slope = jnp.concatenate([jnp.full((tq, 1), s, F32) for s in slopes], axis=0)

            ckb = ck_ref[0, g].astype(BF16)
            cvb = cv_ref[0, g].astype(BF16)
            s = lax.dot_general(q_st, ckb, (((1,), (1,)), ((), ())), preferred_element_type=F32)
            c_iota = lax.broadcasted_iota(I32, (1, ncp), 1)
            c_end = (c_iota * CMP_STRIDE + (CMP_BLOCK - 1)).astype(F32)
            dist = qpos_f - c_end
            mask = (dist >= 0.0) & (c_iota < nc)
            s = jnp.where(mask, s - slope * dist, NEG_INF)
            m = jnp.max(s, axis=-1, keepdims=True)
            e = jnp.where(mask, jnp.exp(s - m), 0.0)
            p = e / jnp.maximum(jnp.sum(e, axis=-1, keepdims=True), 1e-30)
            ocmp_ref[gi * grows:(gi + 1) * grows, :] = jnp.dot(p.astype(BF16), cvb, preferred_element_type=F32)

            p_sum = p[0:tq] + p[tq:2 * tq] + p[2 * tq:3 * tq] + p[3 * tq:4 * tq]
            imp_t = lax.dot_general(cmapt_ref[...], p_sum, (((1,), (1,)), ((), ())), precision=HIGHEST,
                                    preferred_element_type=F32)[0:nsp]
            cur = qpos_t // SLC_BLOCK
            forced = (blk_t == 0) | (blk_t == cur) | (blk_t == cur - 1)
            score = jnp.where(forced, FORCE_SCORE, jnp.where(blk_t * SLC_BLOCK <= qpos_t, imp_t, -1.0))
            score = jnp.where(blk_t < ns, score, -jnp.inf)
            sc_ref[...] = score
            rank = jnp.zeros((nsp, tq), F32)
            for j in range(ns):
                other = sc_ref[pl.ds(j, 1), :]
                ahead = (other > score) | ((other == score) & (j < blk_t))
                rank = rank + ahead.astype(F32)
            sel_t = ((rank < float(n_sel)) & (score >= 0.0)).astype(BF16)
            sel_q = lax.dot_general(sel_t, place, (((0,), (0,)), ((), ())), preferred_element_type=F32)
            sel_term = jnp.where(sel_q > 0.5, 0.0, UNSELECTED)

            q_hi_half = []
            for r in range(R):
                s_hi, s_lo = _split_bf16(slopes[r])
                consts = jnp.where(lane_q < 2, s_hi, jnp.where(lane_q < AUG_POS_LANES, s_lo, 0.0))
                q_hi_half.append(jnp.where(sel_lanes, sel_term, consts))
            qa_ref[gi * grows:(gi + 1) * grows, :] = jnp.concatenate(
                [q_st, jnp.concatenate(q_hi_half, axis=0).astype(BF16)], axis=1)

        def key_tiles(kt_ref, v_ref, aug_ref, t0):
            kbts = [jnp.concatenate([kt_ref[0, g, :, pl.ds(t0, KEY_TILE)], aug_ref[:, pl.ds(t0, KEY_TILE)]], axis=0)
                    for g in groups]
            vbs = [jnp.where(ones_lane, 1.0, v_ref[0, pl.ds(t0, KEY_TILE), g * LANES:(g + 1) * LANES]).astype(BF16)
                   for g in groups]
            return kbts, vbs

        n_kt = jnp.minimum(q_hi // KEY_TILE + 1, n_slc_tiles)
        n_full = jnp.minimum((q_lo + 1) // KEY_TILE, n_kt)

        def slc_tile(kt, masked):
            t0 = pl.multiple_of(kt * KEY_TILE, KEY_TILE)
            kbts, vbs = key_tiles(kts_ref, vs_ref, augs_ref, t0)
            flash_tile(kbts, vbs, (lambda qp: (t0 + kidx) <= qp) if masked else None)

        flash_init()
        lax.fori_loop(0, n_full, lambda kt, c: (slc_tile(kt, False), c)[1], 0)
        lax.fori_loop(n_full, n_kt, lambda kt, c: (slc_tile(kt, True), c)[1], 0)
        o_slc = flash_done()

        w_first = jnp.maximum(q_lo - (WINDOW - 1) - win_pos0, 0) // KEY_TILE
        w_last = jnp.minimum((q_hi - win_pos0) // KEY_TILE + 1, n_win_tiles)

        def win_tile(kt, c):
            t0 = pl.multiple_of(kt * KEY_TILE, KEY_TILE)
            kbts, vbs = key_tiles(ktw_ref, vw_ref, augw_ref, t0)
            kpos = win_pos0 + t0 + kidx
            flash_tile(kbts, vbs, lambda qp: (kpos <= qp) & (kpos > qp - WINDOW))
            return c

        flash_init()
        lax.fori_loop(w_first, w_last, win_tile, 0)
        o_win = flash_done()

        for gi, g in enumerate(groups):
            gr = slice(gi * grows, (gi + 1) * grows)

            def gate_col(j):
                base = j * B_HEADS + g * R
                return jnp.concatenate([gate_ref[0, :, base + r:base + r + 1] for r in range(R)], axis=0)

            o_st = gate_col(0) * ocmp_ref[gr, :] + gate_col(1) * o_slc[gr] + gate_col(2) * o_win[gr]
            o_up = pltpu.roll(o_st, half, axis=1)
            o_ref[0, :, g * 2 * LANES:g * 2 * LANES + LANES] = jnp.where(lo_q, o_st[0:tq], o_up[tq:2 * tq])
            o_ref[0, :, g * 2 * LANES + LANES:(g + 1) * 2 * LANES] = jnp.where(lo_q, o_st[2 * tq:3 * tq], o_up[3 * tq:])


def _nsa_attn(q, gates, ck, cv, kt_slc, v_slc, v_slc_col, kt_win, v_win, v_win_col, *, pos0, win_pos0, nc, ns):
    B, T, _ = q.shape
    tq = min(T, 2 * LANES)
    assert T % tq == 0
    gpar = B_KV_HEADS if B_GROUP * tq < LANES else 1
    ncp = ck.shape[2]
    n_sel = min(N_SELECT, ns)
    nsp = -(-ns // (2 * SUBLANES)) * (2 * SUBLANES)
    Ls, Lw = kt_slc.shape[3], kt_win.shape[3]
    assert Ls % KEY_TILE == 0 and Lw % KEY_TILE == 0 and v_slc.shape[1] == Ls and v_win.shape[1] == Lw
    rows = gpar * B_GROUP * tq
    width = B_KV_HEADS * LANES
    kern = functools.partial(_nsa_attn_kernel, tq=tq, gpar=gpar, pos0=pos0, win_pos0=win_pos0, nc=nc, ns=ns,
                             n_sel=n_sel)
    kt_spec = lambda L: pl.BlockSpec((1, B_KV_HEADS, LANES, L), lambda b, i: (b, 0, 0, 0))
    v_spec = lambda L, col: pl.BlockSpec((1, L, width), lambda b, i, c=col: (b, 0, c))
    return pl.pallas_call(
        kern,
        grid=(B, T // tq),
        in_specs=[pl.BlockSpec((1, tq, D_MODEL), lambda b, i: (b, i, 0)),
                  pl.BlockSpec((1, tq, LANES), lambda b, i: (b, i, 0)),
                  pl.BlockSpec((1, B_KV_HEADS, ncp, LANES), lambda b, i: (b, 0, 0, 0)),
                  pl.BlockSpec((1, B_KV_HEADS, ncp, LANES), lambda b, i: (b, 0, 0, 0)),
                  pl.BlockSpec((LANES, ncp), lambda b, i: (0, 0)),
                  kt_spec(Ls), v_spec(Ls, v_slc_col), kt_spec(Lw), v_spec(Lw, v_win_col),
                  pl.BlockSpec((LANES, Ls), lambda b, i: (0, 0)),
                  pl.BlockSpec((LANES, Lw), lambda b, i: (0, 0))],
        out_specs=pl.BlockSpec((1, tq, D_MODEL), lambda b, i: (b, i, 0)),
        out_shape=jax.ShapeDtypeStruct((B, T, D_MODEL), F32),
        scratch_shapes=[pltpu.VMEM((rows, 2 * LANES), BF16), pltpu.VMEM((rows, LANES), F32),
                        pltpu.VMEM((rows, LANES), F32), pltpu.VMEM((rows, LANES), F32),
                        pltpu.VMEM((nsp, tq), F32)],
        compiler_params=_cparams("arbitrary", "arbitrary"),
        name="nsa_attn",
    )(q, gates, ck, cv, _cmp_to_slc_map_t(ncp, ns), kt_slc, v_slc, kt_win, v_win,
      _aug_table_t(Ls, 0, True), _aug_table_t(Lw, win_pos0, False))


def _deinterleave_kernel(w_ref, perm_ref, glu_ref, lin_ref):
    for c in range(w_ref.shape[1] // MXU_DIM):
        t = jnp.dot(w_ref[:, c * MXU_DIM:(c + 1) * MXU_DIM].astype(BF16), perm_ref[...],
                    preferred_element_type=F32)
        glu_ref[:, c * LANES:(c + 1) * LANES] = t[:, :LANES].astype(BF16)
        lin_ref[:, c * LANES:(c + 1) * LANES] = t[:, LANES:].astype(BF16)


def _deinterleave_up(w_up):
    E, K, N2 = w_up.shape
    rt = 512
    i = np.arange(MXU_DIM)[:, None]
    j = np.arange(MXU_DIM)[None, :]
    perm = jnp.asarray(((i % 2) * LANES + i // 2) == j, dtype=BF16)
    return pl.pallas_call(
        _deinterleave_kernel,
        grid=(E, K // rt),
        in_specs=[pl.BlockSpec((None, rt, N2), lambda e, r: (e, r, 0)),
                  pl.BlockSpec((MXU_DIM, MXU_DIM), lambda e, r: (0, 0))],
        out_specs=[pl.BlockSpec((None, rt, N2 // 2), lambda e, r: (e, r, 0))] * 2,
        out_shape=[jax.ShapeDtypeStruct((E, K, N2 // 2), BF16)] * 2,
        compiler_params=_cparams("arbitrary", "arbitrary"),
        name="deinterleave_up",
    )(w_up, perm)


def _route_kernel(te_ref, start_ref, pos_ref, cnt_ref):
    rt = te_ref.shape[0]

    @pl.when(pl.program_id(0) == 0)
    def _():
        cnt_ref[...] = start_ref[...]

    te = te_ref[...]
    lane = lax.broadcasted_iota(I32, (rt, LANES), 1)
    onehots = [(lane == te[:, k:k + 1]).astype(F32) for k in range(TOP_K)]
    total = onehots[0] + onehots[1] + onehots[2] + onehots[3]
    r = lax.broadcasted_iota(I32, (rt, rt), 0)
    c = lax.broadcasted_iota(I32, (rt, rt), 1)
    earlier = (c < r).astype(BF16)
    base = cnt_ref[...] + jnp.dot(earlier, total.astype(BF16), preferred_element_type=F32)
    pos = jnp.zeros((rt, LANES), I32)
    for k in range(TOP_K):
        pk = jnp.sum(onehots[k] * base, axis=-1, keepdims=True)
        pos = jnp.where(lane == k, pk.astype(I32), pos)
    pos_ref[...] = pos
    cnt_ref[...] = cnt_ref[...] + jnp.sum(total, axis=0, keepdims=True)


def _route(te, start):
    N = te.shape[0]
    rt = _largest_divisor(N, (256, 128, 64, 32, 16, 8))
    return pl.pallas_call(
        _route_kernel,
        grid=(N // rt,),
        in_specs=[pl.BlockSpec((rt, LANES), lambda i: (i, 0)), pl.BlockSpec((1, LANES), lambda i: (0, 0))],
        out_specs=pl.BlockSpec((rt, LANES), lambda i: (i, 0)),
        out_shape=jax.ShapeDtypeStruct((N, LANES), I32),
        scratch_shapes=[pltpu.VMEM((1, LANES), F32)],
        compiler_params=_cparams("arbitrary"),
        name="moe_route",
    )(te, start)


def _row_copy_wait(src_rows, dst_rows, sem):
    pltpu.make_async_copy(src_rows, dst_rows, sem).wait()


def _dispatch_kernel(pos_ref, x_ref, xr_in_ref, xr_ref, sem):
    del xr_in_ref
    rt = x_ref.shape[0]

    def body(i, carry):
        for k in range(TOP_K):
            dst = pos_ref[i * TOP_K + k]
            pltpu.make_async_copy(x_ref.at[pl.ds(i, 1)], xr_ref.at[pl.ds(dst, 1)], sem).start(priority=k % 2)
        return carry

    lax.fori_loop(0, rt, body, 0)
    for k in range(TOP_K):
        _row_copy_wait(x_ref, xr_ref.at[pl.ds(0, rt)], sem)


def _dispatch(x, pos_flat, xr_init):
    N, D = x.shape
    rt = _largest_divisor(N, (256, 128, 64, 32, 16, 8))
    return pl.pallas_call(
        _dispatch_kernel,
        grid=(N // rt,),
        in_specs=[pl.BlockSpec((rt * TOP_K,), lambda i: (i,), memory_space=pltpu.SMEM),
                  pl.BlockSpec((rt, D), lambda i: (i, 0)),
                  pl.BlockSpec(memory_space=pl.ANY)],
        out_specs=pl.BlockSpec(memory_space=pl.ANY),
        out_shape=jax.ShapeDtypeStruct(xr_init.shape, xr_init.dtype),
        scratch_shapes=[pltpu.SemaphoreType.DMA(())],
        input_output_aliases={2: 0},
        compiler_params=_cparams("arbitrary"),
        name="moe_dispatch",
    )(pos_flat, x, xr_init)


def _moe_kernel(be_ref, nv_ref, x_ref, wg_ref, wl_ref, bg_ref, bl_ref, wd_ref, bd_ref, o_ref):
    i = pl.program_id(0)

    @pl.when(i < nv_ref[0])
    def _():
        packed = x_ref[...]
        x = jnp.concatenate([pltpu.bitcast(packed & jnp.uint32(0xFFFF0000), F32),
                             pltpu.bitcast(packed << 16, F32)], axis=1).astype(BF16)
        glu = jnp.dot(x, wg_ref[...], preferred_element_type=F32) + bg_ref[...]
        lin = jnp.dot(x, wl_ref[...], preferred_element_type=F32) + bl_ref[...]
        glu = jnp.minimum(glu, SWIGLU_LIMIT)
        lin = jnp.clip(lin, -SWIGLU_LIMIT, SWIGLU_LIMIT)
        act = glu * _sigmoid(SWIGLU_ALPHA * glu) * (lin + 1.0)
        o_ref[...] = jnp.dot(act.astype(BF16), wd_ref[...], preferred_element_type=F32) + bd_ref[...]

    @pl.when(i >= nv_ref[0])
    def _():
        o_ref[...] = jnp.zeros_like(o_ref)


def _moe_ffn_blocks(xr, blk_exp, nvalid, wg, wl, bg, bl, wd, bd):
    m_pad = xr.shape[0]
    n_blk = m_pad // MOE_ROWS
    wspec = lambda cols: pl.BlockSpec((None, D_MODEL, cols), lambda i, be, nv: (be[i], 0, 0))
    bspec = lambda cols: pl.BlockSpec((None, 1, cols), lambda i, be, nv: (be[i], 0, 0))
    grid_spec = pltpu.PrefetchScalarGridSpec(
        num_scalar_prefetch=2,
        grid=(n_blk,),
        in_specs=[pl.BlockSpec((MOE_ROWS, D_MODEL // 2), lambda i, be, nv: (i, 0)),
                  wspec(D_FF), wspec(D_FF), bspec(D_FF), bspec(D_FF), wspec(D_MODEL), bspec(D_MODEL)],
        out_specs=pl.BlockSpec((MOE_ROWS, D_MODEL), lambda i, be, nv: (i, 0)),
    )
    return pl.pallas_call(
        _moe_kernel,
        grid_spec=grid_spec,
        out_shape=jax.ShapeDtypeStruct((m_pad, D_MODEL), F32),
        compiler_params=_cparams("arbitrary"),
        name="moe_ffn",
    )(blk_exp, nvalid, xr, wg, wl, bg, bl, wd, bd)


def _combine_kernel(pos_ref, x_ref, tg_ref, gf_ref, lng_ref, lnb_ref, yr_ref, xn_ref, ybuf, sem):
    sb, tt, d = x_ref.shape
    rows = sb * tt

    def body(i, carry):
        for k in range(TOP_K):
            src = pos_ref[i * TOP_K + k]
            pltpu.make_async_copy(yr_ref.at[pl.ds(src, 1)], ybuf.at[k, pl.ds(i, 1)], sem).start(priority=k % 2)
        return carry

    lax.fori_loop(0, rows, body, 0)
    for k in range(TOP_K):
        _row_copy_wait(yr_ref.at[pl.ds(0, rows)], ybuf.at[k], sem)
    tg = tg_ref[...].reshape(rows, LANES)
    y = tg[:, 0:1] * ybuf[0]
    for k in range(1, TOP_K):
        y = y + tg[:, k:k + 1] * ybuf[k]
    z = DN_ALPHA * x_ref[...] + (1.0 + gf_ref[...]) * y.reshape(sb, tt, d)
    xn_ref[...] = _layer_norm_rows(z, lng_ref[...], lnb_ref[...])


def _combine(x, yr, pos_flat, tg, mod, ln_g, ln_b):
    B, T, _ = x.shape
    sb, tt = _row_tiles(B, T)
    rows = sb * tt
    nj = T // tt
    vec = pl.BlockSpec((1, 1, D_MODEL), lambda i, j: (0, 0, 0))
    return pl.pallas_call(
        _combine_kernel,
        grid=(B // sb, nj),
        in_specs=[pl.BlockSpec((rows * TOP_K,), lambda i, j: (i * nj + j,), memory_space=pltpu.SMEM),
                  _tok_spec(sb, tt), _tok_spec(sb, tt, LANES), _mod_spec(sb, 5), vec, vec,
                  pl.BlockSpec(memory_space=pl.ANY)],
        out_specs=_tok_spec(sb, tt),
        out_shape=jax.ShapeDtypeStruct((B, T, D_MODEL), F32),
        scratch_shapes=[pltpu.VMEM((TOP_K, rows, D_MODEL), F32), pltpu.SemaphoreType.DMA(())],
        compiler_params=_cparams("arbitrary", "arbitrary"),
        name="moe_combine",
    )(pos_flat, x, tg, mod, ln_g.reshape(1, 1, D_MODEL), ln_b.reshape(1, 1, D_MODEL), yr)


def _moe_plan(te_all):
    N = te_all.shape[0]
    M = N * TOP_K
    onehot = (te_all[:, :TOP_K, None] == jnp.arange(N_EXPERTS, dtype=I32)[None, None, :])
    counts = jnp.sum(onehot.astype(I32), axis=(0, 1))
    padded = (counts + MOE_ROWS - 1) // MOE_ROWS * MOE_ROWS
    pend = jnp.cumsum(padded)
    pstart = pend - padded
    n_blk = -(-(M + N_EXPERTS * (MOE_ROWS - 1)) // MOE_ROWS)
    blk_first = jnp.arange(n_blk, dtype=I32)[:, None] * MOE_ROWS
    blk_exp = jnp.minimum(jnp.sum((pend[None, :] <= blk_first).astype(I32), axis=1), N_EXPERTS - 1).astype(I32)
    nvalid = (pend[-1] // MOE_ROWS).astype(I32).reshape(1)
    start = jnp.pad(pstart.astype(F32), (0, LANES - N_EXPERTS)).reshape(1, LANES)
    return start, blk_exp, nvalid, n_blk * MOE_ROWS


def _compress_inputs(cmp_raw):
    B, L, _ = cmp_raw.shape
    nch = L // CMP_STRIDE
    x = cmp_raw[:, :nch * CMP_STRIDE].reshape(B, nch, CMP_STRIDE, 2, B_KV_HEADS, B_DH)
    x = x.transpose(3, 0, 4, 1, 2, 5)
    return x.reshape(2, B * B_KV_HEADS, nch, CMP_STRIDE * B_DH), nch


def _pad_rows(a, mult, axis=1):
    n = a.shape[axis]
    pad = (-n) % mult
    if pad == 0:
        return a
    widths = [(0, 0)] * a.ndim
    widths[axis] = (0, pad)
    return jnp.pad(a, widths)


def kernel(x_prompt, x_sample, state_hgrn, cache_cmp_kv, cache_slc_kv, state_win_kv, page_table, c_prompt, c_sample,
           ada_w, ada_b, ln_g, ln_b, a_w_in, a_lb, a_onorm_g, a_w_o, kv_ada_w, kv_ada_b, kv_w, cmp_pos, cmp_w1, cmp_b1,
           cmp_w2, cmp_b2, b_w_in, b_w_o, moe_wr, moe_br, moe_w_up, moe_b_up, moe_w_down, moe_b_down):
    Bp, Tp, _ = x_prompt.shape
    Bs, Ts, _ = x_sample.shape
    n_pages = page_table.shape[1]
    past_len = n_pages * PAGE_SIZE
    keep_s = state_win_kv.shape[1]
    G, DH = B_KV_HEADS, B_DH
    GD = G * DH
    Np, Ns = Bp * Tp, Bs * Ts
    assert keep_s % PAGE_SIZE == 0

    a_w_in_bf = a_w_in.astype(BF16)
    a_w_o_bf = a_w_o.astype(BF16)
    b_w_o_bf = b_w_o.astype(BF16)
    dup_cols = [jnp.concatenate([kv_w[:, c * GD + g * DH:c * GD + (g + 1) * DH]] * 2, axis=1)
                for c in range(2, 6) for g in range(G)]
    kv_w_bf = jnp.concatenate([kv_w] + dup_cols, axis=1).astype(BF16)
    wq_bf = b_w_in[:, :, :D_MODEL].astype(BF16)
    n_l = b_w_in.shape[0]
    wg = b_w_in[:, :, D_MODEL:].reshape(n_l, D_MODEL, B_HEADS, 3).transpose(0, 1, 3, 2).reshape(n_l, D_MODEL, 3 * B_HEADS)
    wg_bf = jnp.pad(wg, ((0, 0), (0, 0), (0, LANES - 3 * B_HEADS))).astype(BF16)
    wr_pad = jnp.pad(moe_wr, ((0, 0), (0, 0), (0, LANES - N_EXPERTS)))
    br_pad = jnp.pad(moe_br, ((0, 0), (0, LANES - N_EXPERTS)), constant_values=NEG_INF).reshape(DEPTH, 1, LANES)
    w_glu, w_lin = _deinterleave_up(moe_w_up.reshape(DEPTH * N_EXPERTS, D_MODEL, 2 * D_FF))
    b_glu = moe_b_up[..., 0::2].reshape(DEPTH * N_EXPERTS, 1, D_FF)
    b_lin = moe_b_up[..., 1::2].reshape(DEPTH * N_EXPERTS, 1, D_FF)
    w_dn = moe_w_down.astype(BF16).reshape(DEPTH * N_EXPERTS, D_FF, D_MODEL)
    b_dn = moe_b_down.reshape(DEPTH * N_EXPERTS, 1, D_MODEL)
    lb_p = jax.nn.softmax(a_lb.astype(F32), axis=0)
    lb_all = jnp.maximum(jnp.cumsum(lb_p, axis=0) - lb_p[0], 0.0)
    w1cat = jnp.concatenate([cmp_w1[:, :CMP_STRIDE * DH], cmp_w1[:, CMP_STRIDE * DH:]], axis=-1)
    cvec = _cmp_const(cmp_pos.reshape(2, 1, CMP_BLOCK * DH), cmp_w1, cmp_b1.reshape(2, 1, CMP_HID))
    w2dup = jnp.concatenate([cmp_w2, cmp_w2], axis=-1)
    b2dup = jnp.concatenate([cmp_b2, cmp_b2], axis=-1).reshape(2, 1, LANES)

    n_c = Bp + Bs
    c_all = _pad_rows(jnp.concatenate([c_prompt, c_sample], axis=0), SUBLANES, axis=0)
    mod_all = _silu_linear(c_all, ada_w, ada_b)
    kvmod_all = _silu_linear(c_all, kv_ada_w[None], kv_ada_b[None])[0]
    mods = [(mod_all[l, :Bp].reshape(Bp, 1, 6 * D_MODEL), mod_all[l, Bp:n_c].reshape(Bs, 1, 6 * D_MODEL))
            for l in range(DEPTH)]
    kvmods = (kvmod_all[:Bp].reshape(Bp, 1, 2 * D_MODEL), kvmod_all[Bp:n_c].reshape(Bs, 1, 2 * D_MODEL))

    xs = [x_prompt, x_sample]
    s0s = [jnp.zeros((1, Bp, A_HEADS, A_DK, A_DV), F32), state_hgrn]
    hgrn_out = [[], []]
    ctx = [None, None]
    new_kv = [None, None]
    xr = None

    def keys_t(dup, col):
        b, length, _ = dup.shape
        return dup[:, :, col * G * LANES:(col + 1) * G * LANES].reshape(b, length, G, LANES).transpose(0, 2, 3, 1)

    def build_ctx(t, x):
        B, T, _ = x.shape
        if t == 0:
            kv, dup = _kv_proj(x, kvmods[t], kv_w_bf, BF16)
            cmp_raw = kv[:, :, :2 * GD]
            if T % KEY_TILE:
                dup = _pad_rows(dup, KEY_TILE)
            attn_kv = (keys_t(dup, 0), dup, 1, keys_t(dup, 2), dup, 3)
            L, pos0, win_pos0 = T, 0, 0
            win_out = kv[:, T - min(WINDOW, T):, 4 * GD:].reshape(B, -1, 2, G, DH)
            xc, nch = _compress_inputs(cmp_raw)
            ckv = _compress(xc, w1cat, cvec, w2dup, b2dup).reshape(2, B, G, nch, LANES)
        else:
            kv, dup = _kv_proj(x, kvmods[t], kv_w_bf, F32)
            L = past_len + T
            nch = L // CMP_STRIDE
            assert nch * CMP_STRIDE <= past_len and T <= keep_s
            cpp = PAGE_SIZE // CMP_STRIDE
            pool_t = cache_cmp_kv.reshape(-1, cpp, CMP_STRIDE, 2, G, DH).transpose(3, 4, 0, 1, 2, 5)
            pool_t = pool_t.reshape(2, G, -1, cpp, CMP_STRIDE * DH)
            ckv = _compress_paged(pool_t, page_table, w1cat, cvec, w2dup, b2dup)[:, :, :, :nch]
            ls_pad = -(-L // KEY_TILE) * KEY_TILE
            lw_pad = -(-(keep_s + T) // KEY_TILE) * KEY_TILE
            kt_slc, v_slc = _pages_to_kv(cache_slc_kv.transpose(0, 2, 3, 4, 1).astype(BF16), page_table, dup, 0, 1, ls_pad)
            n_wp = keep_s // PAGE_SIZE
            win_table = jnp.arange(B * n_wp, dtype=I32).reshape(B, n_wp)
            win_pool = state_win_kv.reshape(B * n_wp, PAGE_SIZE, 2, G, DH).transpose(0, 2, 3, 4, 1).astype(BF16)
            kt_win, v_win = _pages_to_kv(win_pool, win_table, dup, 2, 3, lw_pad)
            attn_kv = (kt_slc, v_slc, 0, kt_win, v_win, 0)
            pos0 = past_len
            win_pos0 = past_len - keep_s
            win_out = jnp.concatenate([state_win_kv[:, T:], kv[:, :, 4 * GD:].reshape(B, T, 2, G, DH)], axis=1)
        nc = nch - CMP_RATIO + 1
        ns = -(-L // SLC_BLOCK)
        ckv = _pad_rows(ckv, LANES, axis=3)
        info = dict(ck=ckv[0], cv=ckv[1], attn_kv=attn_kv, pos0=pos0, win_pos0=win_pos0, nc=nc, ns=ns)
        outs = (kv[:, :, :2 * GD].reshape(B, T, 2, G, DH), kv[:, :, 2 * GD:4 * GD].reshape(B, T, 2, G, DH),
                win_out)
        return info, outs

    for l in range(DEPTH):
        o_mix = []
        for t in range(2):
            x = xs[t]
            mod = mods[l][t]
            if l < N_A_LAYERS:
                q, k, g, v, og = _hgrn_in(x, mod, a_w_in_bf[l], lb_all[l])
                o, S = _hgrn_scan(q, k, g, v, og, s0s[t], l if t else 0, a_onorm_g[l])
                hgrn_out[t].append(S)
            else:
                j = l - N_A_LAYERS
                if ctx[t] is None:
                    ctx[t], new_kv[t] = build_ctx(t, x)
                c = ctx[t]
                q, gates = _nsa_in(x, mod, wq_bf[j], wg_bf[j])
                o = _nsa_attn(q, gates, c['ck'], c['cv'], *c['attn_kv'],
                              pos0=c['pos0'], win_pos0=c['win_pos0'], nc=c['nc'], ns=c['ns'])
            o_mix.append(o)
        w_o = a_w_o_bf[l] if l < N_A_LAYERS else b_w_o_bf[l - N_A_LAYERS]
        hf, te, tg = [], [], []
        for t in range(2):
            xn, h, e, gt = _mixer_out(o_mix[t], w_o, xs[t], mods[l][t], ln_g[l, 0], ln_b[l, 0], wr_pad[l], br_pad[l])
            xs[t] = xn
            hf.append(h.reshape(-1, D_MODEL // 2))
            te.append(e.reshape(-1, LANES))
            tg.append(gt)
        te_all = jnp.concatenate(te, axis=0)
        start, blk_exp, nvalid, m_pad = _moe_plan(te_all)
        pos_flat = _route(te_all, start)[:, :TOP_K].reshape(-1)
        if xr is None:
            xr = jnp.zeros((m_pad, D_MODEL // 2), jnp.uint32)
        xr = _dispatch(jnp.concatenate(hf, axis=0), pos_flat, xr)
        yr = _moe_ffn_blocks(xr, blk_exp + l * N_EXPERTS, nvalid, w_glu, w_lin, b_glu, b_lin, w_dn, b_dn)
        pos_t = [pos_flat[:Np * TOP_K], pos_flat[Np * TOP_K:]]
        for t in range(2):
            xs[t] = _combine(xs[t], yr, pos_t[t], tg[t], mods[l][t], ln_g[l, 1], ln_b[l, 1])

    return (xs[0], xs[1], jnp.stack(hgrn_out[0], axis=0), jnp.stack(hgrn_out[1], axis=0),
            new_kv[0][0], new_kv[1][0], new_kv[0][1], new_kv[1][1], new_kv[0][2], new_kv[1][2])
```

```python
import functools

import numpy as np
import jax
import jax.numpy as jnp
from jax import lax
from jax.experimental import pallas as pl
from jax.experimental.pallas import tpu as pltpu

F32 = jnp.float32
BF16 = jnp.bfloat16
I32 = jnp.int32
HIGHEST = lax.Precision.HIGHEST

D_MODEL = 1024
DEPTH = 4
PAGE_SIZE = 128
N_A_LAYERS = DEPTH // 2
A_HEADS = 8
A_DK = 128
A_DV = D_MODEL // A_HEADS
A_DF = A_HEADS * A_DK
HGRN_CHUNK = 32
B_HEADS = 16
B_DH = D_MODEL // B_HEADS
B_KV_HEADS = 4
B_GROUP = B_HEADS // B_KV_HEADS
CMP_BLOCK = 32
CMP_STRIDE = 16
CMP_RATIO = CMP_BLOCK // CMP_STRIDE
CMP_HID = 2 * B_DH
SLC_BLOCK = 64
N_SELECT = 16
WINDOW = 512
FORCE_SCORE = 1e4
N_EXPERTS = 32
TOP_K = 4
D_FF = D_MODEL
SWIGLU_LIMIT = 7.0
SWIGLU_ALPHA = 1.702
DN_ALPHA = (2 * DEPTH) ** 0.25
LN_EPS = 1e-5
RMS_EPS = 1e-6
NEG_INF = -1e30

LANES = 128
SUBLANES = 8
MXU_DIM = 256
ROW_TILE = 256
MOE_ROWS = 256
KEY_TILE = 512
VMEM_LIMIT = 56 * 1024 * 1024

AUG_POS_LANES = 4
AUG_BLK_LANE0 = 8
UNSELECTED = -(2.0 ** 30)


def _cparams(*sem):
    return pltpu.CompilerParams(dimension_semantics=sem, vmem_limit_bytes=VMEM_LIMIT)


def _row_tiles(B, T):
    if T >= ROW_TILE:
        assert T % ROW_TILE == 0
        return 1, ROW_TILE
    sb = max(1, ROW_TILE // T)
    while B % sb:
        sb //= 2
    return sb, T


def _largest_divisor(n, candidates):
    for c in candidates:
        if n % c == 0:
            return c
    raise ValueError(n)


def _sigmoid(x):
    return 1.0 / (1.0 + jnp.exp(-x))


def _linear_kernel(x_ref, w_ref, b_ref, o_ref):
    x = x_ref[...]
    x = x * _sigmoid(x)
    o_ref[...] = jnp.dot(x.astype(BF16), w_ref[...].astype(BF16), preferred_element_type=F32) + b_ref[...]


def _silu_linear(x, w, b):
    L, K, N = w.shape
    M = x.shape[0]
    tn = 1024
    return pl.pallas_call(
        _linear_kernel,
        grid=(L, N // tn),
        in_specs=[pl.BlockSpec((M, K), lambda l, j: (0, 0)),
                  pl.BlockSpec((None, K, tn), lambda l, j: (l, 0, j)),
                  pl.BlockSpec((None, 1, tn), lambda l, j: (l, 0, j))],
        out_specs=pl.BlockSpec((None, M, tn), lambda l, j: (l, 0, j)),
        out_shape=jax.ShapeDtypeStruct((L, M, N), F32),
        compiler_params=_cparams("arbitrary", "arbitrary"),
        name="silu_linear",
    )(x, w, b.reshape(L, 1, N))


def _modulate(x_ref, sc_ref, sh_ref):
    sb, tt, d = x_ref.shape
    h = x_ref[...] * (1.0 + sc_ref[...]) + sh_ref[...]
    return h.reshape(sb * tt, d)


def _hgrn_in_kernel(x_ref, sh_ref, sc_ref, w_ref, lb_ref, q_ref, k_ref, g_ref, v_ref, og_ref):
    sb, tt, d = x_ref.shape
    h = _modulate(x_ref, sc_ref, sh_ref).astype(BF16)
    proj = jnp.dot(h, w_ref[...], preferred_element_type=F32)
    qz = proj[:, :A_DF]
    fz = proj[:, A_DF:2 * A_DF]
    lb = lb_ref[...]
    log_sig = jnp.minimum(fz, 0.0) - jnp.log1p(jnp.exp(-jnp.abs(fz)))
    pos = lb > 0.0
    log_lb = jnp.where(pos, jnp.log(jnp.where(pos, lb, 1.0)), NEG_INF)
    other = jnp.log1p(-lb) + log_sig
    logf = jnp.maximum(log_lb, other) + jnp.log1p(jnp.exp(-jnp.abs(log_lb - other)))
    q_ref[...] = (qz * _sigmoid(qz)).reshape(sb, tt, A_DF)
    g_ref[...] = logf.reshape(sb, tt, A_DF)
    k_ref[...] = ((1.0 - lb) * _sigmoid(-fz)).reshape(sb, tt, A_DF)
    v_ref[...] = proj[:, 2 * A_DF:2 * A_DF + D_MODEL].reshape(sb, tt, D_MODEL)
    og_ref[...] = proj[:, 2 * A_DF + D_MODEL:].reshape(sb, tt, D_MODEL)


def _mod_spec(sb, comp):
    return pl.BlockSpec((sb, 1, D_MODEL), lambda i, j, c=comp: (i, 0, c))


def _tok_spec(sb, tt, width=D_MODEL):
    return pl.BlockSpec((sb, tt, width), lambda i, j: (i, j, 0))


def _hgrn_in(x, mod, w_bf, lb):
    B, T, _ = x.shape
    sb, tt = _row_tiles(B, T)
    n_out = w_bf.shape[1]
    shp = jax.ShapeDtypeStruct((B, T, D_MODEL), F32)
    return pl.pallas_call(
        _hgrn_in_kernel,
        grid=(B // sb, T // tt),
        in_specs=[_tok_spec(sb, tt), _mod_spec(sb, 0), _mod_spec(sb, 1),
                  pl.BlockSpec((D_MODEL, n_out), lambda i, j: (0, 0)),
                  pl.BlockSpec((1, A_DF), lambda i, j: (0, 0))],
        out_specs=[_tok_spec(sb, tt)] * 5,
        out_shape=[shp] * 5,
        compiler_params=_cparams("arbitrary", "arbitrary"),
        name="hgrn_in",
    )(x, mod, mod, w_bf, lb.reshape(1, A_DF))


def _nsa_in_kernel(x_ref, sh_ref, sc_ref, wq_ref, wg_ref, q_ref, gate_ref):
    sb, tt, d = x_ref.shape
    h = _modulate(x_ref, sc_ref, sh_ref).astype(BF16)
    q = jnp.dot(h, wq_ref[...], preferred_element_type=F32) * (B_DH ** -0.5)
    gz = jnp.dot(h, wg_ref[...], preferred_element_type=F32)
    q_ref[...] = q.reshape(sb, tt, D_MODEL)
    gate_ref[...] = _sigmoid(gz).reshape(sb, tt, LANES)


def _nsa_in(x, mod, wq_bf, wg_bf):
    B, T, _ = x.shape
    sb, tt = _row_tiles(B, T)
    return pl.pallas_call(
        _nsa_in_kernel,
        grid=(B // sb, T // tt),
        in_specs=[_tok_spec(sb, tt), _mod_spec(sb, 0), _mod_spec(sb, 1),
                  pl.BlockSpec((D_MODEL, D_MODEL), lambda i, j: (0, 0)),
                  pl.BlockSpec((D_MODEL, LANES), lambda i, j: (0, 0))],
        out_specs=[_tok_spec(sb, tt), _tok_spec(sb, tt, LANES)],
        out_shape=[jax.ShapeDtypeStruct((B, T, D_MODEL), F32), jax.ShapeDtypeStruct((B, T, LANES), F32)],
        compiler_params=_cparams("arbitrary", "arbitrary"),
        name="nsa_in",
    )(x, mod, mod, wq_bf, wg_bf)


KV_RAW = 6 * B_KV_HEADS * B_DH
KV_DUP = 4 * B_KV_HEADS * LANES


def _kv_proj_kernel(x_ref, sh_ref, sc_ref, w_ref, kv_ref, dup_ref):
    sb, tt, d = x_ref.shape
    h = _modulate(x_ref, sc_ref, sh_ref).astype(BF16)
    kv = jnp.dot(h, w_ref[...], preferred_element_type=F32)
    kv_ref[...] = kv[:, :KV_RAW].reshape(sb, tt, KV_RAW)
    dup_ref[...] = kv[:, KV_RAW:].reshape(sb, tt, KV_DUP).astype(dup_ref.dtype)


def _kv_proj(x, kvmod, w_bf, dup_dtype):
    B, T, _ = x.shape
    sb, tt = _row_tiles(B, T)
    n_out = w_bf.shape[1]
    return pl.pallas_call(
        _kv_proj_kernel,
        grid=(B // sb, T // tt),
        in_specs=[_tok_spec(sb, tt), _mod_spec(sb, 0), _mod_spec(sb, 1),
                  pl.BlockSpec((D_MODEL, n_out), lambda i, j: (0, 0))],
        out_specs=[_tok_spec(sb, tt, KV_RAW), _tok_spec(sb, tt, KV_DUP)],
        out_shape=[jax.ShapeDtypeStruct((B, T, KV_RAW), F32), jax.ShapeDtypeStruct((B, T, KV_DUP), dup_dtype)],
        compiler_params=_cparams("arbitrary", "arbitrary"),
        name="kv_proj",
    )(x, kvmod, kvmod, w_bf)


def _layer_norm_rows(z, g, b):
    mu = jnp.mean(z, axis=-1, keepdims=True)
    zc = z - mu
    var = jnp.mean(zc * zc, axis=-1, keepdims=True)
    return zc * lax.rsqrt(var + LN_EPS) * g + b


def _mixer_out_kernel(o_ref, w_ref, x_ref, gm_ref, shf_ref, scf_ref, lng_ref, lnb_ref, wr_ref, br_ref,
                      xn_ref, hf_ref, te_ref, tg_ref):
    sb, tt, d = x_ref.shape
    rows = sb * tt
    o = o_ref[...].reshape(rows, d).astype(BF16)
    y = jnp.dot(o, w_ref[...], preferred_element_type=F32).reshape(sb, tt, d)
    z = DN_ALPHA * x_ref[...] + (1.0 + gm_ref[...]) * y
    xn = _layer_norm_rows(z, lng_ref[...], lnb_ref[...])
    xn_ref[...] = xn
    hf = xn * (1.0 + scf_ref[...]) + shf_ref[...]
    bits = pltpu.bitcast(hf.astype(BF16).astype(F32).reshape(rows, d), jnp.uint32)
    hf_ref[...] = (bits[:, :d // 2] | (bits[:, d // 2:] >> 16)).reshape(sb, tt, d // 2)
    logits = jnp.dot(hf.reshape(rows, d), wr_ref[...], precision=HIGHEST,
                     preferred_element_type=F32) + br_ref[...]
    lane = lax.broadcasted_iota(I32, (rows, LANES), 1)
    te = jnp.zeros((rows, LANES), I32)
    tv = jnp.zeros((rows, LANES), F32)
    v0 = None
    denom = None
    for k in range(TOP_K):
        m = jnp.max(logits, axis=-1, keepdims=True)
        idx = jnp.min(jnp.where(logits == m, lane, LANES), axis=-1, keepdims=True)
        if k == 0:
            v0 = m
            e = jnp.ones_like(m)
            denom = e
        else:
            e = jnp.exp(m - v0)
            denom = denom + e
        te = jnp.where(lane == k, idx, te)
        tv = jnp.where(lane == k, e, tv)
        logits = jnp.where(lane == idx, -jnp.inf, logits)
    te_ref[...] = te.reshape(sb, tt, LANES)
    tg_ref[...] = (tv / denom).reshape(sb, tt, LANES)


def _mixer_out(o, w_bf, x, mod, ln_g, ln_b, wr_pad, br_pad):
    B, T, _ = x.shape
    sb, tt = _row_tiles(B, T)
    vec = pl.BlockSpec((1, 1, D_MODEL), lambda i, j: (0, 0, 0))
    return pl.pallas_call(
        _mixer_out_kernel,
        grid=(B // sb, T // tt),
        in_specs=[_tok_spec(sb, tt), pl.BlockSpec((D_MODEL, D_MODEL), lambda i, j: (0, 0)), _tok_spec(sb, tt),
                  _mod_spec(sb, 2), _mod_spec(sb, 3), _mod_spec(sb, 4), vec, vec,
                  pl.BlockSpec((D_MODEL, LANES), lambda i, j: (0, 0)),
                  pl.BlockSpec((1, LANES), lambda i, j: (0, 0))],
        out_specs=[_tok_spec(sb, tt), _tok_spec(sb, tt, D_MODEL // 2), _tok_spec(sb, tt, LANES), _tok_spec(sb, tt, LANES)],
        out_shape=[jax.ShapeDtypeStruct((B, T, D_MODEL), F32), jax.ShapeDtypeStruct((B, T, D_MODEL // 2), jnp.uint32),
                   jax.ShapeDtypeStruct((B, T, LANES), I32), jax.ShapeDtypeStruct((B, T, LANES), F32)],
        compiler_params=_cparams("arbitrary", "arbitrary"),
        name="mixer_out",
    )(o, w_bf, x, mod, mod, mod, ln_g.reshape(1, 1, D_MODEL), ln_b.reshape(1, 1, D_MODEL), wr_pad, br_pad)


def _hgrn_scan_kernel(q_ref, k_ref, g_ref, v_ref, og_ref, s0_ref, gn_ref, o_ref, sout_ref,
                      st_ref, b_ref, k_scr, v_scr, *, chunk, n_chunks):
    tb = pl.program_id(1)
    C = chunk
    nsub = C // SUBLANES

    @pl.when(tb == 0)
    def _():
        for h in range(A_HEADS):
            st_ref[h] = s0_ref[0, h].T

    row = lax.broadcasted_iota(I32, (C, C), 0)
    col = lax.broadcasted_iota(I32, (C, C), 1)
    tri = (row >= col).astype(F32)
    sub_row = lax.broadcasted_iota(I32, (SUBLANES, A_DK), 0)
    gn = gn_ref[...]

    def chunk_body(c, carry):
        r0 = pl.multiple_of(c * C, C)
        for h in range(A_HEADS):
            ls = slice(h * A_DK, (h + 1) * A_DK)
            g = g_ref[0, pl.ds(r0, C), ls]
            b = jnp.dot(tri, g, precision=HIGHEST, preferred_element_type=F32)
            b_ref[...] = b
            q = q_ref[0, pl.ds(r0, C), ls]
            k = k_ref[0, pl.ds(r0, C), ls]
            v = v_ref[0, pl.ds(r0, C), ls]
            k_scr[...] = k
            v_scr[...] = v
            st = st_ref[h]
            o_inter = lax.dot_general((q * jnp.exp(b)).astype(BF16), st.astype(BF16),
                                      (((1,), (1,)), ((), ())), preferred_element_type=F32)
            acc = [jnp.zeros((SUBLANES, A_DV), F32) for _ in range(nsub)]
            for s in range(C):
                j = s // SUBLANES
                b_s = b_ref[pl.ds(s, 1), :]
                k_s = k_scr[pl.ds(s, 1), :]
                v_s = v_scr[pl.ds(s, 1), :]
                for i in range(j, nsub):
                    rs = slice(i * SUBLANES, (i + 1) * SUBLANES)
                    dlt = b[rs] - b_s
                    if i == j:
                        causal = sub_row >= (s % SUBLANES)
                        e = jnp.where(causal, jnp.exp(jnp.where(causal, dlt, 0.0)), 0.0)
                    else:
                        e = jnp.exp(dlt)
                    w = jnp.sum(q[rs] * e * k_s, axis=-1, keepdims=True)
                    acc[i] = acc[i] + w * v_s
            o = o_inter + (jnp.concatenate(acc, axis=0) if nsub > 1 else acc[0])
            b_last = b_ref[pl.ds(C - 1, 1), :]
            kd = k * jnp.exp(b_last - b)
            st_new = st * jnp.exp(b_last) + lax.dot_general(
                v.astype(BF16), kd.astype(BF16), (((0,), (0,)), ((), ())), preferred_element_type=F32)
            st_ref[h] = st_new
            og = og_ref[0, pl.ds(r0, C), ls]
            on = o * lax.rsqrt(jnp.mean(o * o, axis=-1, keepdims=True) + RMS_EPS) * gn
            o_ref[0, pl.ds(r0, C), ls] = on * (og * _sigmoid(og))
        return carry

    lax.fori_loop(0, n_chunks, chunk_body, 0)

    @pl.when(tb == pl.num_programs(1) - 1)
    def _():
        for h in range(A_HEADS):
            sout_ref[0, h] = st_ref[h].T


def _hgrn_scan(q, k, g, v, og, s0_all, layer, gn):
    B, T, _ = q.shape
    C = HGRN_CHUNK if T % HGRN_CHUNK == 0 else T
    assert C % SUBLANES == 0
    tblk = min(T, ROW_TILE)
    assert T % tblk == 0 and tblk % C == 0
    tok = pl.BlockSpec((1, tblk, D_MODEL), lambda b, t: (b, t, 0))
    st = pl.BlockSpec((1, A_HEADS, A_DK, A_DV), lambda b, t: (b, 0, 0, 0))
    st_in = pl.BlockSpec((None, 1, A_HEADS, A_DK, A_DV), lambda b, t: (layer, b, 0, 0, 0))
    return pl.pallas_call(
        functools.partial(_hgrn_scan_kernel, chunk=C, n_chunks=tblk // C),
        grid=(B, T // tblk),
        in_specs=[tok, tok, tok, tok, tok, st_in, pl.BlockSpec((1, A_DV), lambda b, t: (0, 0))],
        out_specs=[tok, st],
        out_shape=[jax.ShapeDtypeStruct((B, T, D_MODEL), F32),
                   jax.ShapeDtypeStruct((B, A_HEADS, A_DK, A_DV), F32)],
        scratch_shapes=[pltpu.VMEM((A_HEADS, A_DV, A_DK), F32), pltpu.VMEM((C, A_DK), F32),
                        pltpu.VMEM((C, A_DK), F32), pltpu.VMEM((C, A_DV), F32)],
        compiler_params=_cparams("arbitrary", "arbitrary"),
        name="hgrn_scan",
    )(q, k, g, v, og, s0_all, gn.reshape(1, A_DV))


def _cmp_const_kernel(pos_ref, w1_ref, b1_ref, o_ref):
    pos8 = jnp.broadcast_to(pos_ref[...], (SUBLANES, pos_ref.shape[1]))
    o_ref[...] = jnp.dot(pos8, w1_ref[...], precision=HIGHEST, preferred_element_type=F32)[0:1] + b1_ref[...]


def _cmp_const(pos_flat, w1, b1):
    kdim = w1.shape[1]
    return pl.pallas_call(
        _cmp_const_kernel,
        grid=(2,),
        in_specs=[pl.BlockSpec((None, 1, kdim), lambda a: (a, 0, 0)),
                  pl.BlockSpec((None, kdim, CMP_HID), lambda a: (a, 0, 0)),
                  pl.BlockSpec((None, 1, CMP_HID), lambda a: (a, 0, 0))],
        out_specs=pl.BlockSpec((None, 1, CMP_HID), lambda a: (a, 0, 0)),
        out_shape=jax.ShapeDtypeStruct((2, 1, CMP_HID), F32),
        compiler_params=_cparams("arbitrary"),
        name="cmp_const",
    )(pos_flat, w1, b1)


def _compress_kernel(x_ref, w1_ref, cvec_ref, w2_ref, b2_ref, o_ref):
    gb, n, kdim = x_ref.shape
    rows = gb * n
    uv = jnp.dot(x_ref[...].reshape(rows, kdim).astype(BF16), w1_ref[...].astype(BF16),
                 preferred_element_type=F32)
    u = uv[:, :CMP_HID]
    v = uv[:, CMP_HID:]
    v_next = pltpu.roll(v, rows - 1, axis=0)
    hid = u + v_next + cvec_ref[...]
    act = hid * _sigmoid(hid)
    out = jnp.dot(act.astype(BF16), w2_ref[...].astype(BF16), preferred_element_type=F32) + b2_ref[...]
    o_ref[...] = out.reshape(gb, n, LANES)


def _compress(xc, w1cat, cvec, w2dup, b2dup):
    _, BG, nch, kdim = xc.shape
    gb = _largest_divisor(BG, (8, 4, 2, 1))
    return pl.pallas_call(
        _compress_kernel,
        grid=(2, BG // gb),
        in_specs=[pl.BlockSpec((None, gb, nch, kdim), lambda a, i: (a, i, 0, 0)),
                  pl.BlockSpec((None, kdim, 2 * CMP_HID), lambda a, i: (a, 0, 0)),
                  pl.BlockSpec((None, 1, CMP_HID), lambda a, i: (a, 0, 0)),
                  pl.BlockSpec((None, CMP_HID, LANES), lambda a, i: (a, 0, 0)),
                  pl.BlockSpec((None, 1, LANES), lambda a, i: (a, 0, 0))],
        out_specs=pl.BlockSpec((None, gb, nch, LANES), lambda a, i: (a, i, 0, 0)),
        out_shape=jax.ShapeDtypeStruct((2, BG, nch, LANES), F32),
        compiler_params=_cparams("arbitrary", "arbitrary"),
        name="compress_kv",
    )(xc, w1cat, cvec, w2dup, b2dup)


def _compress_paged_kernel(pt_ref, *refs, n_seq_pages):
    del pt_ref
    page_refs = refs[:n_seq_pages]
    w1_ref, cvec_ref, w2_ref, b2_ref, o_ref = refs[n_seq_pages:]
    x = jnp.concatenate([r[...] for r in page_refs], axis=0)
    rows = x.shape[0]
    uv = jnp.dot(x.astype(BF16), w1_ref[...].astype(BF16), preferred_element_type=F32)
    v_next = pltpu.roll(uv[:, CMP_HID:], rows - 1, axis=0)
    hid = uv[:, :CMP_HID] + v_next + cvec_ref[...]
    act = hid * _sigmoid(hid)
    out = jnp.dot(act.astype(BF16), w2_ref[...].astype(BF16), preferred_element_type=F32) + b2_ref[...]
    o_ref[...] = out.reshape(o_ref.shape)


def _compress_paged(pool_t, table, w1cat, cvec, w2dup, b2dup):
    _, G, _, cpp, kdim = pool_t.shape
    B, n_pages = table.shape
    nb = _largest_divisor(B, (2, 1))
    nch = n_pages * cpp
    page_specs = [pl.BlockSpec((None, None, None, cpp, kdim),
                               lambda a, g, i, pt, s=s, p=p: (a, g, pt[i * nb + s, p], 0, 0))
                  for s in range(nb) for p in range(n_pages)]
    grid_spec = pltpu.PrefetchScalarGridSpec(
        num_scalar_prefetch=1,
        grid=(2, G, B // nb),
        in_specs=page_specs + [pl.BlockSpec((None, kdim, 2 * CMP_HID), lambda a, g, i, pt: (a, 0, 0)),
                               pl.BlockSpec((None, 1, CMP_HID), lambda a, g, i, pt: (a, 0, 0)),
                               pl.BlockSpec((None, CMP_HID, LANES), lambda a, g, i, pt: (a, 0, 0)),
                               pl.BlockSpec((None, 1, LANES), lambda a, g, i, pt: (a, 0, 0))],
        out_specs=pl.BlockSpec((None, nb, None, nch, LANES), lambda a, g, i, pt: (a, i, g, 0, 0)),
    )
    return pl.pallas_call(
        functools.partial(_compress_paged_kernel, n_seq_pages=nb * n_pages),
        grid_spec=grid_spec,
        out_shape=jax.ShapeDtypeStruct((2, B, G, nch, LANES), F32),
        compiler_params=_cparams("arbitrary", "arbitrary", "arbitrary"),
        name="compress_paged",
    )(table, *([pool_t] * (nb * n_pages)), w1cat, cvec, w2dup, b2dup)


def _pages_kernel(pt_ref, *refs, n_pages):
    del pt_ref
    page_refs = refs[:n_pages]
    newk_ref, newv_ref, kt_ref, v_ref = refs[n_pages:]
    for p in range(n_pages):
        rows = slice(p * PAGE_SIZE, (p + 1) * PAGE_SIZE)
        page = page_refs[p]
        for g in range(B_KV_HEADS):
            kt = page[0, g]
            kt_ref[0, g, :, rows] = jnp.concatenate([kt, kt], axis=0)
            vt = jnp.concatenate([page[1, g], page[1, g ^ 1]], axis=0).astype(F32)
            v_ref[0, rows, g * LANES:(g + 1) * LANES] = vt.T.astype(BF16)
    t_new = newk_ref.shape[1]
    tail = v_ref.shape[1] - n_pages * PAGE_SIZE
    pad = jnp.zeros((tail - t_new, newk_ref.shape[2]), F32)
    newk = jnp.concatenate([newk_ref[0], pad], axis=0)
    newv = jnp.concatenate([newv_ref[0], pad], axis=0)
    v_ref[0, n_pages * PAGE_SIZE:, :] = newv.astype(BF16)
    for g in range(B_KV_HEADS):
        for j in range(tail // PAGE_SIZE):
            blk = newk[j * PAGE_SIZE:(j + 1) * PAGE_SIZE, g * LANES:(g + 1) * LANES]
            c0 = (n_pages + j) * PAGE_SIZE
            kt_ref[0, g, :, c0:c0 + PAGE_SIZE] = blk.T.astype(BF16)


def _pages_to_kv(pool, table, new_dup, k_col, v_col, total_len):
    B, n_pages = table.shape
    T = new_dup.shape[1]
    assert total_len % PAGE_SIZE == 0 and total_len >= n_pages * PAGE_SIZE + T
    width = B_KV_HEADS * LANES
    page_specs = [pl.BlockSpec((None, 2, B_KV_HEADS, B_DH, PAGE_SIZE), lambda b, pt, p=p: (pt[b, p], 0, 0, 0, 0))
                  for p in range(n_pages)]
    grid_spec = pltpu.PrefetchScalarGridSpec(
        num_scalar_prefetch=1,
        grid=(B,),
        in_specs=page_specs + [pl.BlockSpec((1, T, width), lambda b, pt, c=k_col: (b, 0, c)),
                               pl.BlockSpec((1, T, width), lambda b, pt, c=v_col: (b, 0, c))],
        out_specs=[pl.BlockSpec((1, B_KV_HEADS, LANES, total_len), lambda b, pt: (b, 0, 0, 0)),
                   pl.BlockSpec((1, total_len, width), lambda b, pt: (b, 0, 0))],
    )
    return pl.pallas_call(
        functools.partial(_pages_kernel, n_pages=n_pages),
        grid_spec=grid_spec,
        out_shape=[jax.ShapeDtypeStruct((B, B_KV_HEADS, LANES, total_len), BF16),
                   jax.ShapeDtypeStruct((B, total_len, width), BF16)],
        compiler_params=_cparams("arbitrary"),
        name="pages_to_kv",
    )(table, *([pool] * n_pages), new_dup, new_dup)


def _cmp_to_slc_map_t(ncp, ns):
    j = np.arange(LANES)[:, None]
    i = np.arange(ncp)[None, :]
    lo = np.maximum(i * CMP_STRIDE, j * SLC_BLOCK)
    hi = np.minimum(i * CMP_STRIDE + CMP_BLOCK, (j + 1) * SLC_BLOCK)
    m = np.maximum(hi - lo, 0) / CMP_STRIDE
    return jnp.asarray(np.where(j < ns, m, 0.0), dtype=F32)


def _aug_table_t(length, pos0, with_blocks):
    pos = np.arange(length) + pos0
    a = np.zeros((LANES, length), np.float32)
    a[0] = (pos // MXU_DIM) * MXU_DIM
    a[1] = pos % MXU_DIM
    a[2] = a[0]
    a[3] = a[1]
    if with_blocks:
        blk = np.arange(length) // SLC_BLOCK
        assert AUG_BLK_LANE0 + blk.max() < LANES
        a[AUG_BLK_LANE0 + blk, np.arange(length)] = 1.0
    return jnp.asarray(a, dtype=BF16)


def _split_bf16(x):
    hi = float(np.float32(x).astype(BF16).astype(np.float32))
    lo = float(np.float32(x - hi).astype(BF16).astype(np.float32))
    return hi, lo


def _nsa_attn_kernel(q_ref, gate_ref, ck_ref, cv_ref, cmapt_ref, kts_ref, vs_ref, ktw_ref, vw_ref,
                     augs_ref, augw_ref, o_ref, qa_ref, m_ref, acc_ref, ocmp_ref, sc_ref,
                     *, tq, gpar, pos0, win_pos0, nc, ns, n_sel):
    qi = pl.program_id(1)
    R = B_GROUP
    grows = R * tq
    rows = gpar * grows
    chunk = min(grows, LANES)
    n_chunk = rows // chunk
    ncp = ck_ref.shape[2]
    nsp = sc_ref.shape[0]
    n_slc_tiles = vs_ref.shape[1] // KEY_TILE
    n_win_tiles = vw_ref.shape[1] // KEY_TILE
    half = B_DH
    lane_q = lax.broadcasted_iota(I32, (tq, LANES), 1)
    lo_q = lane_q < half
    ones_lane = lax.broadcasted_iota(I32, (KEY_TILE, LANES), 1) == half
    t_idx = lax.broadcasted_iota(I32, (tq, 1), 0)
    q_lo = pos0 + qi * tq
    q_hi = q_lo + tq - 1
    qpos_i = q_lo + t_idx
    qpos_g = jnp.concatenate([qpos_i] * R, axis=0)
    qpos_all = jnp.concatenate([qpos_g] * gpar, axis=0) if gpar > 1 else qpos_g
    qpos_f = qpos_g.astype(F32)
    kidx = lax.broadcasted_iota(I32, (1, KEY_TILE), 1)
    blk_t = lax.broadcasted_iota(I32, (nsp, tq), 0)
    qpos_t = q_lo + lax.broadcasted_iota(I32, (nsp, tq), 1)
    place = (lax.broadcasted_iota(I32, (nsp, LANES), 1) ==
             lax.broadcasted_iota(I32, (nsp, LANES), 0) + AUG_BLK_LANE0).astype(BF16)
    sel_lanes = (lane_q >= AUG_BLK_LANE0) & (lane_q < AUG_BLK_LANE0 + ns)

    def flash_init():
        m_ref[...] = jnp.full(m_ref.shape, NEG_INF, F32)
        acc_ref[...] = jnp.zeros(acc_ref.shape, F32)

    def flash_tile(kbts, vbs, mask_fn):
        cs = range(n_chunk)
        rs = [pl.ds(c * chunk, chunk) for c in cs]
        gi = [(c * chunk) // grows for c in cs]
        ss = []
        for c in cs:
            s = jnp.dot(qa_ref[rs[c], :], kbts[gi[c]], preferred_element_type=F32)
            if mask_fn is not None:
                s = jnp.where(mask_fn(qpos_all[c * chunk:(c + 1) * chunk]), s, NEG_INF)
            ss.append(s)
        m_old = [m_ref[rs[c], :] for c in cs]
        m_new = [jnp.maximum(m_old[c], jnp.broadcast_to(jnp.max(ss[c], axis=-1, keepdims=True), (chunk, LANES)))
                 for c in cs]
        es = [jnp.exp(ss[c] - jnp.concatenate([m_new[c]] * (KEY_TILE // LANES), axis=1)) for c in cs]
        for c in cs:
            alpha = jnp.exp(m_old[c] - m_new[c])
            acc_ref[rs[c], :] = alpha * acc_ref[rs[c], :] + jnp.dot(es[c].astype(BF16), vbs[gi[c]],
                                                                   preferred_element_type=F32)
            m_ref[rs[c], :] = m_new[c]

    def flash_done():
        acc = acc_ref[...]
        return acc / acc[:, half:half + 1]

    for g0 in range(0, B_KV_HEADS, gpar):
        groups = list(range(g0, g0 + gpar))
        for gi, g in enumerate(groups):
            c0 = q_ref[0, :, g * 2 * LANES:g * 2 * LANES + LANES]
            c1 = q_ref[0, :, g * 2 * LANES + LANES:(g + 1) * 2 * LANES]
            q_st = jnp.concatenate([jnp.where(lo_q, c0, 0.0), jnp.where(lo_q, 0.0, c0),
                                    jnp.where(lo_q, c1, 0.0), jnp.where(lo_q, 0.0, c1)], axis=0).astype(BF16)
            slopes = [2.0 ** (-8.0 * (g * R + r + 1) / B_HEADS) for r in range(R)]
            slope = jnp.concatenate([jnp.full((tq, 1), s, F32) for s in slopes], axis=0)

            ckb = ck_ref[0, g].astype(BF16)
            cvb = cv_ref[0, g].astype(BF16)
            s = lax.dot_general(q_st, ckb, (((1,), (1,)), ((), ())), preferred_element_type=F32)
            c_iota = lax.broadcasted_iota(I32, (1, ncp), 1)
            c_end = (c_iota * CMP_STRIDE + (CMP_BLOCK - 1)).astype(F32)
            dist = qpos_f - c_end
            mask = (dist >= 0.0) & (c_iota < nc)
            s = jnp.where(mask, s - slope * dist, NEG_INF)
            m = jnp.max(s, axis=-1, keepdims=True)
            e = jnp.where(mask, jnp.exp(s - m), 0.0)
            p = e / jnp.maximum(jnp.sum(e, axis=-1, keepdims=True), 1e-30)
            ocmp_ref[gi * grows:(gi + 1) * grows, :] = jnp.dot(p.astype(BF16), cvb, preferred_element_type=F32)

            p_sum = p[0:tq] + p[tq:2 * tq] + p[2 * tq:3 * tq] + p[3 * tq:4 * tq]
            imp_t = lax.dot_general(cmapt_ref[...], p_sum, (((1,), (1,)), ((), ())), precision=HIGHEST,
                                    preferred_element_type=F32)[0:nsp]
            cur = qpos_t // SLC_BLOCK
            forced = (blk_t == 0) | (blk_t == cur) | (blk_t == cur - 1)
            score = jnp.where(forced, FORCE_SCORE, jnp.where(blk_t * SLC_BLOCK <= qpos_t, imp_t, -1.0))
            score = jnp.where(blk_t < ns, score, -jnp.inf)
            sc_ref[...] = score
            rank = jnp.zeros((nsp, tq), F32)
            for j in range(ns):
                other = sc_ref[pl.ds(j, 1), :]
                ahead = (other > score) | ((other == score) & (j < blk_t))
                rank = rank + ahead.astype(F32)
            sel_t = ((rank < float(n_sel)) & (score >= 0.0)).astype(BF16)
            sel_q = lax.dot_general(sel_t, place, (((0,), (0,)), ((), ())), preferred_element_type=F32)
            sel_term = jnp.where(sel_q > 0.5, 0.0, UNSELECTED)

            q_hi_half = []
            for r in range(R):
                s_hi, s_lo = _split_bf16(slopes[r])
                consts = jnp.where(lane_q < 2, s_hi, jnp.where(lane_q < AUG_POS_LANES, s_lo, 0.0))
                q_hi_half.append(jnp.where(sel_lanes, sel_term, consts))
            qa_ref[gi * grows:(gi + 1) * grows, :] = jnp.concatenate(
                [q_st, jnp.concatenate(q_hi_half, axis=0).astype(BF16)], axis=1)

        def key_tiles(kt_ref, v_ref, aug_ref, t0):
            kbts = [jnp.concatenate([kt_ref[0, g, :, pl.ds(t0, KEY_TILE)], aug_ref[:, pl.ds(t0, KEY_TILE)]], axis=0)
                    for g in groups]
            vbs = [jnp.where(ones_lane, 1.0, v_ref[0, pl.ds(t0, KEY_TILE), g * LANES:(g + 1) * LANES]).astype(BF16)
                   for g in groups]
            return kbts, vbs

        n_kt = jnp.minimum(q_hi // KEY_TILE + 1, n_slc_tiles)
        n_full = jnp.minimum((q_lo + 1) // KEY_TILE, n_kt)

        def slc_tile(kt, masked):
            t0 = pl.multiple_of(kt * KEY_TILE, KEY_TILE)
            kbts, vbs = key_tiles(kts_ref, vs_ref, augs_ref, t0)
            flash_tile(kbts, vbs, (lambda qp: (t0 + kidx) <= qp) if masked else None)

        flash_init()
        lax.fori_loop(0, n_full, lambda kt, c: (slc_tile(kt, False), c)[1], 0)
        lax.fori_loop(n_full, n_kt, lambda kt, c: (slc_tile(kt, True), c)[1], 0)
        o_slc = flash_done()

        w_first = jnp.maximum(q_lo - (WINDOW - 1) - win_pos0, 0) // KEY_TILE
        w_last = jnp.minimum((q_hi - win_pos0) // KEY_TILE + 1, n_win_tiles)

        def win_tile(kt, c):
            t0 = pl.multiple_of(kt * KEY_TILE, KEY_TILE)
            kbts, vbs = key_tiles(ktw_ref, vw_ref, augw_ref, t0)
            kpos = win_pos0 + t0 + kidx
            flash_tile(kbts, vbs, lambda qp: (kpos <= qp) & (kpos > qp - WINDOW))
            return c

        flash_init()
        lax.fori_loop(w_first, w_last, win_tile, 0)
        o_win = flash_done()

        for gi, g in enumerate(groups):
            gr = slice(gi * grows, (gi + 1) * grows)

            def gate_col(j):
                base = j * B_HEADS + g * R
                return jnp.concatenate([gate_ref[0, :, base + r:base + r + 1] for r in range(R)], axis=0)

            o_st = gate_col(0) * ocmp_ref[gr, :] + gate_col(1) * o_slc[gr] + gate_col(2) * o_win[gr]
            o_up = pltpu.roll(o_st, half, axis=1)
            o_ref[0, :, g * 2 * LANES:g * 2 * LANES + LANES] = jnp.where(lo_q, o_st[0:tq], o_up[tq:2 * tq])
            o_ref[0, :, g * 2 * LANES + LANES:(g + 1) * 2 * LANES] = jnp.where(lo_q, o_st[2 * tq:3 * tq], o_up[3 * tq:])


def _nsa_attn(q, gates, ck, cv, kt_slc, v_slc, v_slc_col, kt_win, v_win, v_win_col, *, pos0, win_pos0, nc, ns):
    B, T, _ = q.shape
    tq = _largest_divisor(T, (4 * LANES, 2 * LANES, LANES, 64, 32, 16, 8))
    gpar = B_KV_HEADS if B_GROUP * tq < LANES else 1
    ncp = ck.shape[2]
    n_sel = min(N_SELECT, ns)
    nsp = -(-ns // (2 * SUBLANES)) * (2 * SUBLANES)
    Ls, Lw = kt_slc.shape[3], kt_win.shape[3]
    assert Ls % KEY_TILE == 0 and Lw % KEY_TILE == 0 and v_slc.shape[1] == Ls and v_win.shape[1] == Lw
    rows = gpar * B_GROUP * tq
    width = B_KV_HEADS * LANES
    kern = functools.partial(_nsa_attn_kernel, tq=tq, gpar=gpar, pos0=pos0, win_pos0=win_pos0, nc=nc, ns=ns,
                             n_sel=n_sel)
    kt_spec = lambda L: pl.BlockSpec((1, B_KV_HEADS, LANES, L), lambda b, i: (b, 0, 0, 0))
    v_spec = lambda L, col: pl.BlockSpec((1, L, width), lambda b, i, c=col: (b, 0, c))
    return pl.pallas_call(
        kern,
        grid=(B, T // tq),
        in_specs=[pl.BlockSpec((1, tq, D_MODEL), lambda b, i: (b, i, 0)),
                  pl.BlockSpec((1, tq, LANES), lambda b, i: (b, i, 0)),
                  pl.BlockSpec((1, B_KV_HEADS, ncp, LANES), lambda b, i: (b, 0, 0, 0)),
                  pl.BlockSpec((1, B_KV_HEADS, ncp, LANES), lambda b, i: (b, 0, 0, 0)),
                  pl.BlockSpec((LANES, ncp), lambda b, i: (0, 0)),
                  kt_spec(Ls), v_spec(Ls, v_slc_col), kt_spec(Lw), v_spec(Lw, v_win_col),
                  pl.BlockSpec((LANES, Ls), lambda b, i: (0, 0)),
                  pl.BlockSpec((LANES, Lw), lambda b, i: (0, 0))],
        out_specs=pl.BlockSpec((1, tq, D_MODEL), lambda b, i: (b, i, 0)),
        out_shape=jax.ShapeDtypeStruct((B, T, D_MODEL), F32),
        scratch_shapes=[pltpu.VMEM((rows, 2 * LANES), BF16), pltpu.VMEM((rows, LANES), F32),
                        pltpu.VMEM((rows, LANES), F32), pltpu.VMEM((rows, LANES), F32),
                        pltpu.VMEM((nsp, tq), F32)],
        compiler_params=_cparams("arbitrary", "arbitrary"),
        name="nsa_attn",
    )(q, gates, ck, cv, _cmp_to_slc_map_t(ncp, ns), kt_slc, v_slc, kt_win, v_win,
      _aug_table_t(Ls, 0, True), _aug_table_t(Lw, win_pos0, False))


def _deinterleave_kernel(w_ref, perm_ref, glu_ref, lin_ref):
    for c in range(w_ref.shape[1] // MXU_DIM):
        t = jnp.dot(w_ref[:, c * MXU_DIM:(c + 1) * MXU_DIM].astype(BF16), perm_ref[...],
                    preferred_element_type=F32)
        glu_ref[:, c * LANES:(c + 1) * LANES] = t[:, :LANES].astype(BF16)
        lin_ref[:, c * LANES:(c + 1) * LANES] = t[:, LANES:].astype(BF16)


def _deinterleave_up(w_up):
    E, K, N2 = w_up.shape
    rt = 512
    i = np.arange(MXU_DIM)[:, None]
    j = np.arange(MXU_DIM)[None, :]
    perm = jnp.asarray(((i % 2) * LANES + i // 2) == j, dtype=BF16)
    return pl.pallas_call(
        _deinterleave_kernel,
        grid=(E, K // rt),
        in_specs=[pl.BlockSpec((None, rt, N2), lambda e, r: (e, r, 0)),
                  pl.BlockSpec((MXU_DIM, MXU_DIM), lambda e, r: (0, 0))],
        out_specs=[pl.BlockSpec((None, rt, N2 // 2), lambda e, r: (e, r, 0))] * 2,
        out_shape=[jax.ShapeDtypeStruct((E, K, N2 // 2), BF16)] * 2,
        compiler_params=_cparams("arbitrary", "arbitrary"),
        name="deinterleave_up",
    )(w_up, perm)


def _route_kernel(te_ref, start_ref, pos_ref, cnt_ref):
    rt = te_ref.shape[0]

    @pl.when(pl.program_id(0) == 0)
    def _():
        cnt_ref[...] = start_ref[...]

    te = te_ref[...]
    lane = lax.broadcasted_iota(I32, (rt, LANES), 1)
    onehots = [(lane == te[:, k:k + 1]).astype(F32) for k in range(TOP_K)]
    total = onehots[0] + onehots[1] + onehots[2] + onehots[3]
    r = lax.broadcasted_iota(I32, (rt, rt), 0)
    c = lax.broadcasted_iota(I32, (rt, rt), 1)
    earlier = (c < r).astype(BF16)
    base = cnt_ref[...] + jnp.dot(earlier, total.astype(BF16), preferred_element_type=F32)
    pos = jnp.zeros((rt, LANES), I32)
    for k in range(TOP_K):
        pk = jnp.sum(onehots[k] * base, axis=-1, keepdims=True)
        pos = jnp.where(lane == k, pk.astype(I32), pos)
    pos_ref[...] = pos
    cnt_ref[...] = cnt_ref[...] + jnp.sum(total, axis=0, keepdims=True)


def _route(te, start):
    N = te.shape[0]
    rt = _largest_divisor(N, (256, 128, 64, 32, 16, 8))
    return pl.pallas_call(
        _route_kernel,
        grid=(N // rt,),
        in_specs=[pl.BlockSpec((rt, LANES), lambda i: (i, 0)), pl.BlockSpec((1, LANES), lambda i: (0, 0))],
        out_specs=pl.BlockSpec((rt, LANES), lambda i: (i, 0)),
        out_shape=jax.ShapeDtypeStruct((N, LANES), I32),
        scratch_shapes=[pltpu.VMEM((1, LANES), F32)],
        compiler_params=_cparams("arbitrary"),
        name="moe_route",
    )(te, start)


def _row_copy_wait(src_rows, dst_rows, sem):
    pltpu.make_async_copy(src_rows, dst_rows, sem).wait()


def _dispatch_kernel(pos_ref, x_ref, xr_in_ref, xr_ref, sem):
    del xr_in_ref
    rt = x_ref.shape[0]

    def body(i, carry):
        for k in range(TOP_K):
            dst = pos_ref[i * TOP_K + k]
            pltpu.make_async_copy(x_ref.at[pl.ds(i, 1)], xr_ref.at[pl.ds(dst, 1)], sem).start(priority=k % 2)
        return carry

    lax.fori_loop(0, rt, body, 0)
    for k in range(TOP_K):
        _row_copy_wait(x_ref, xr_ref.at[pl.ds(0, rt)], sem)


def _dispatch(x, pos_flat, xr_init):
    N, D = x.shape
    rt = _largest_divisor(N, (256, 128, 64, 32, 16, 8))
    return pl.pallas_call(
        _dispatch_kernel,
        grid=(N // rt,),
        in_specs=[pl.BlockSpec((rt * TOP_K,), lambda i: (i,), memory_space=pltpu.SMEM),
                  pl.BlockSpec((rt, D), lambda i: (i, 0)),
                  pl.BlockSpec(memory_space=pl.ANY)],
        out_specs=pl.BlockSpec(memory_space=pl.ANY),
        out_shape=jax.ShapeDtypeStruct(xr_init.shape, xr_init.dtype),
        scratch_shapes=[pltpu.SemaphoreType.DMA(())],
        input_output_aliases={2: 0},
        compiler_params=_cparams("arbitrary"),
        name="moe_dispatch",
    )(pos_flat, x, xr_init)


def _moe_kernel(be_ref, nv_ref, x_ref, wg_ref, wl_ref, bg_ref, bl_ref, wd_ref, bd_ref, o_ref):
    i = pl.program_id(0)

    @pl.when(i < nv_ref[0])
    def _():
        packed = x_ref[...]
        x = jnp.concatenate([pltpu.bitcast(packed & jnp.uint32(0xFFFF0000), F32),
                             pltpu.bitcast(packed << 16, F32)], axis=1).astype(BF16)
        glu = jnp.dot(x, wg_ref[...], preferred_element_type=F32) + bg_ref[...]
        lin = jnp.dot(x, wl_ref[...], preferred_element_type=F32) + bl_ref[...]
        glu = jnp.minimum(glu, SWIGLU_LIMIT)
        lin = jnp.clip(lin, -SWIGLU_LIMIT, SWIGLU_LIMIT)
        act = glu * _sigmoid(SWIGLU_ALPHA * glu) * (lin + 1.0)
        o_ref[...] = jnp.dot(act.astype(BF16), wd_ref[...].astype(BF16), preferred_element_type=F32) + bd_ref[...]

    @pl.when(i >= nv_ref[0])
    def _():
        o_ref[...] = jnp.zeros_like(o_ref)


def _moe_ffn_blocks(xr, blk_exp, nvalid, wg, wl, bg, bl, wd, bd):
    m_pad = xr.shape[0]
    n_blk = m_pad // MOE_ROWS
    wspec = lambda cols: pl.BlockSpec((None, D_MODEL, cols), lambda i, be, nv: (be[i], 0, 0))
    bspec = lambda cols: pl.BlockSpec((None, 1, cols), lambda i, be, nv: (be[i], 0, 0))
    grid_spec = pltpu.PrefetchScalarGridSpec(
        num_scalar_prefetch=2,
        grid=(n_blk,),
        in_specs=[pl.BlockSpec((MOE_ROWS, D_MODEL // 2), lambda i, be, nv: (i, 0)),
                  wspec(D_FF), wspec(D_FF), bspec(D_FF), bspec(D_FF), wspec(D_MODEL), bspec(D_MODEL)],
        out_specs=pl.BlockSpec((MOE_ROWS, D_MODEL), lambda i, be, nv: (i, 0)),
    )
    return pl.pallas_call(
        _moe_kernel,
        grid_spec=grid_spec,
        out_shape=jax.ShapeDtypeStruct((m_pad, D_MODEL), F32),
        compiler_params=_cparams("arbitrary"),
        name="moe_ffn",
    )(blk_exp, nvalid, xr, wg, wl, bg, bl, wd, bd)


def _combine_kernel(pos_ref, pos_next_ref, x_ref, tg_ref, gf_ref, lng_ref, lnb_ref, yr_ref, xn_ref, ybuf, sems,
                    *, n_steps):
    sb, tt, d = x_ref.shape
    rows = sb * tt
    step = pl.program_id(0) * pl.num_programs(1) + pl.program_id(1)
    slot = step % 2

    def issue(p_ref, sl):
        def body(i, carry):
            for k in range(TOP_K):
                src = p_ref[i * TOP_K + k]
                pltpu.make_async_copy(yr_ref.at[pl.ds(src, 1)], ybuf.at[sl, k, pl.ds(i, 1)],
                                      sems.at[sl]).start(priority=k % 2)
            return carry

        lax.fori_loop(0, rows, body, 0)

    @pl.when(step == 0)
    def _():
        issue(pos_ref, 0)

    @pl.when(step + 1 < n_steps)
    def _():
        issue(pos_next_ref, 1 - slot)

    for k in range(TOP_K):
        _row_copy_wait(yr_ref.at[pl.ds(0, rows)], ybuf.at[slot, k], sems.at[slot])
    tg = tg_ref[...].reshape(rows, LANES)
    y = tg[:, 0:1] * ybuf[slot, 0]
    for k in range(1, TOP_K):
        y = y + tg[:, k:k + 1] * ybuf[slot, k]
    z = DN_ALPHA * x_ref[...] + (1.0 + gf_ref[...]) * y.reshape(sb, tt, d)
    xn_ref[...] = _layer_norm_rows(z, lng_ref[...], lnb_ref[...])


def _combine(x, yr, pos_flat, tg, mod, ln_g, ln_b):
    B, T, _ = x.shape
    sb, tt = _row_tiles(B, T)
    rows = sb * tt
    nj = T // tt
    n_steps = (B // sb) * nj
    vec = pl.BlockSpec((1, 1, D_MODEL), lambda i, j: (0, 0, 0))
    return pl.pallas_call(
        functools.partial(_combine_kernel, n_steps=n_steps),
        grid=(B // sb, nj),
        in_specs=[pl.BlockSpec((rows * TOP_K,), lambda i, j: (i * nj + j,), memory_space=pltpu.SMEM),
                  pl.BlockSpec((rows * TOP_K,), lambda i, j: (jnp.minimum(i * nj + j + 1, n_steps - 1),),
                               memory_space=pltpu.SMEM),
                  _tok_spec(sb, tt), _tok_spec(sb, tt, LANES), _mod_spec(sb, 5), vec, vec,
                  pl.BlockSpec(memory_space=pl.ANY)],
        out_specs=_tok_spec(sb, tt),
        out_shape=jax.ShapeDtypeStruct((B, T, D_MODEL), F32),
        scratch_shapes=[pltpu.VMEM((2, TOP_K, rows, D_MODEL), F32), pltpu.SemaphoreType.DMA((2,))],
        compiler_params=_cparams("arbitrary", "arbitrary"),
        name="moe_combine",
    )(pos_flat, pos_flat, x, tg, mod, ln_g.reshape(1, 1, D_MODEL), ln_b.reshape(1, 1, D_MODEL), yr)


def _moe_plan(te_all):
    N = te_all.shape[0]
    M = N * TOP_K
    onehot = (te_all[:, :TOP_K, None] == jnp.arange(N_EXPERTS, dtype=I32)[None, None, :])
    counts = jnp.sum(onehot.astype(I32), axis=(0, 1))
    padded = (counts + MOE_ROWS - 1) // MOE_ROWS * MOE_ROWS
    pend = jnp.cumsum(padded)
    pstart = pend - padded
    n_blk = -(-(M + N_EXPERTS * (MOE_ROWS - 1)) // MOE_ROWS)
    blk_first = jnp.arange(n_blk, dtype=I32)[:, None] * MOE_ROWS
    blk_exp = jnp.minimum(jnp.sum((pend[None, :] <= blk_first).astype(I32), axis=1), N_EXPERTS - 1).astype(I32)
    nvalid = (pend[-1] // MOE_ROWS).astype(I32).reshape(1)
    start = jnp.pad(pstart.astype(F32), (0, LANES - N_EXPERTS)).reshape(1, LANES)
    return start, blk_exp, nvalid, n_blk * MOE_ROWS


def _compress_inputs(cmp_raw):
    B, L, _ = cmp_raw.shape
    nch = L // CMP_STRIDE
    x = cmp_raw[:, :nch * CMP_STRIDE].reshape(B, nch, CMP_STRIDE, 2, B_KV_HEADS, B_DH)
    x = x.transpose(3, 0, 4, 1, 2, 5)
    return x.reshape(2, B * B_KV_HEADS, nch, CMP_STRIDE * B_DH), nch


def _pad_rows(a, mult, axis=1):
    n = a.shape[axis]
    pad = (-n) % mult
    if pad == 0:
        return a
    widths = [(0, 0)] * a.ndim
    widths[axis] = (0, pad)
    return jnp.pad(a, widths)


def kernel(x_prompt, x_sample, state_hgrn, cache_cmp_kv, cache_slc_kv, state_win_kv, page_table, c_prompt, c_sample,
           ada_w, ada_b, ln_g, ln_b, a_w_in, a_lb, a_onorm_g, a_w_o, kv_ada_w, kv_ada_b, kv_w, cmp_pos, cmp_w1, cmp_b1,
           cmp_w2, cmp_b2, b_w_in, b_w_o, moe_wr, moe_br, moe_w_up, moe_b_up, moe_w_down, moe_b_down):
    Bp, Tp, _ = x_prompt.shape
    Bs, Ts, _ = x_sample.shape
    n_pages = page_table.shape[1]
    past_len = n_pages * PAGE_SIZE
    keep_s = state_win_kv.shape[1]
    G, DH = B_KV_HEADS, B_DH
    GD = G * DH
    Np, Ns = Bp * Tp, Bs * Ts
    assert keep_s % PAGE_SIZE == 0

    a_w_in_bf = a_w_in.astype(BF16)
    a_w_o_bf = a_w_o.astype(BF16)
    b_w_o_bf = b_w_o.astype(BF16)
    dup_cols = [jnp.concatenate([kv_w[:, c * GD + g * DH:c * GD + (g + 1) * DH]] * 2, axis=1)
                for c in range(2, 6) for g in range(G)]
    kv_w_bf = jnp.concatenate([kv_w] + dup_cols, axis=1).astype(BF16)
    wq_bf = b_w_in[:, :, :D_MODEL].astype(BF16)
    n_l = b_w_in.shape[0]
    wg = b_w_in[:, :, D_MODEL:].reshape(n_l, D_MODEL, B_HEADS, 3).transpose(0, 1, 3, 2).reshape(n_l, D_MODEL, 3 * B_HEADS)
    wg_bf = jnp.pad(wg, ((0, 0), (0, 0), (0, LANES - 3 * B_HEADS))).astype(BF16)
    wr_pad = jnp.pad(moe_wr, ((0, 0), (0, 0), (0, LANES - N_EXPERTS)))
    br_pad = jnp.pad(moe_br, ((0, 0), (0, LANES - N_EXPERTS)), constant_values=NEG_INF).reshape(DEPTH, 1, LANES)
    w_glu, w_lin = _deinterleave_up(moe_w_up.reshape(DEPTH * N_EXPERTS, D_MODEL, 2 * D_FF))
    b_glu = moe_b_up[..., 0::2].reshape(DEPTH * N_EXPERTS, 1, D_FF)
    b_lin = moe_b_up[..., 1::2].reshape(DEPTH * N_EXPERTS, 1, D_FF)
    w_dn = moe_w_down.reshape(DEPTH * N_EXPERTS, D_FF, D_MODEL)
    b_dn = moe_b_down.reshape(DEPTH * N_EXPERTS, 1, D_MODEL)
    lb_p = jax.nn.softmax(a_lb.astype(F32), axis=0)
    lb_all = jnp.maximum(jnp.cumsum(lb_p, axis=0) - lb_p[0], 0.0)
    w1cat = jnp.concatenate([cmp_w1[:, :CMP_STRIDE * DH], cmp_w1[:, CMP_STRIDE * DH:]], axis=-1)
    cvec = _cmp_const(cmp_pos.reshape(2, 1, CMP_BLOCK * DH), cmp_w1, cmp_b1.reshape(2, 1, CMP_HID))
    w2dup = jnp.concatenate([cmp_w2, cmp_w2], axis=-1)
    b2dup = jnp.concatenate([cmp_b2, cmp_b2], axis=-1).reshape(2, 1, LANES)

    n_c = Bp + Bs
    c_all = _pad_rows(jnp.concatenate([c_prompt, c_sample], axis=0), SUBLANES, axis=0)
    mod_all = _silu_linear(c_all, ada_w, ada_b)
    kvmod_all = _silu_linear(c_all, kv_ada_w[None], kv_ada_b[None])[0]
    mods = [(mod_all[l, :Bp].reshape(Bp, 1, 6 * D_MODEL), mod_all[l, Bp:n_c].reshape(Bs, 1, 6 * D_MODEL))
            for l in range(DEPTH)]
    kvmods = (kvmod_all[:Bp].reshape(Bp, 1, 2 * D_MODEL), kvmod_all[Bp:n_c].reshape(Bs, 1, 2 * D_MODEL))

    xs = [x_prompt, x_sample]
    s0s = [jnp.zeros((1, Bp, A_HEADS, A_DK, A_DV), F32), state_hgrn]
    hgrn_out = [[], []]
    ctx = [None, None]
    new_kv = [None, None]
    xr = None

    def keys_t(dup, col):
        b, length, _ = dup.shape
        return dup[:, :, col * G * LANES:(col + 1) * G * LANES].reshape(b, length, G, LANES).transpose(0, 2, 3, 1)

    def build_ctx(t, x):
        B, T, _ = x.shape
        if t == 0:
            kv, dup = _kv_proj(x, kvmods[t], kv_w_bf, BF16)
            cmp_raw = kv[:, :, :2 * GD]
            if T % KEY_TILE:
                dup = _pad_rows(dup, KEY_TILE)
            attn_kv = (keys_t(dup, 0), dup, 1, keys_t(dup, 2), dup, 3)
            L, pos0, win_pos0 = T, 0, 0
            win_out = kv[:, T - min(WINDOW, T):, 4 * GD:].reshape(B, -1, 2, G, DH)
            xc, nch = _compress_inputs(cmp_raw)
            ckv = _compress(xc, w1cat, cvec, w2dup, b2dup).reshape(2, B, G, nch, LANES)
        else:
            kv, dup = _kv_proj(x, kvmods[t], kv_w_bf, F32)
            L = past_len + T
            nch = L // CMP_STRIDE
            assert nch * CMP_STRIDE <= past_len and T <= keep_s
            cpp = PAGE_SIZE // CMP_STRIDE
            pool_t = cache_cmp_kv.reshape(-1, cpp, CMP_STRIDE, 2, G, DH).transpose(3, 4, 0, 1, 2, 5)
            pool_t = pool_t.reshape(2, G, -1, cpp, CMP_STRIDE * DH)
            ckv = _compress_paged(pool_t, page_table, w1cat, cvec, w2dup, b2dup)[:, :, :, :nch]
            ls_pad = -(-L // KEY_TILE) * KEY_TILE
            lw_pad = -(-(keep_s + T) // KEY_TILE) * KEY_TILE
            kt_slc, v_slc = _pages_to_kv(cache_slc_kv.transpose(0, 2, 3, 4, 1).astype(BF16), page_table, dup, 0, 1, ls_pad)
            n_wp = keep_s // PAGE_SIZE
            win_table = jnp.arange(B * n_wp, dtype=I32).reshape(B, n_wp)
            win_pool = state_win_kv.reshape(B * n_wp, PAGE_SIZE, 2, G, DH).transpose(0, 2, 3, 4, 1).astype(BF16)
            kt_win, v_win = _pages_to_kv(win_pool, win_table, dup, 2, 3, lw_pad)
            attn_kv = (kt_slc, v_slc, 0, kt_win, v_win, 0)
            pos0 = past_len
            win_pos0 = past_len - keep_s
            win_out = jnp.concatenate([state_win_kv[:, T:], kv[:, :, 4 * GD:].reshape(B, T, 2, G, DH)], axis=1)
        nc = nch - CMP_RATIO + 1
        ns = -(-L // SLC_BLOCK)
        ckv = _pad_rows(ckv, LANES, axis=3)
        info = dict(ck=ckv[0], cv=ckv[1], attn_kv=attn_kv, pos0=pos0, win_pos0=win_pos0, nc=nc, ns=ns)
        outs = (kv[:, :, :2 * GD].reshape(B, T, 2, G, DH), kv[:, :, 2 * GD:4 * GD].reshape(B, T, 2, G, DH),
                win_out)
        return info, outs

    for l in range(DEPTH):
        o_mix = []
        for t in range(2):
            x = xs[t]
            mod = mods[l][t]
            if l < N_A_LAYERS:
                q, k, g, v, og = _hgrn_in(x, mod, a_w_in_bf[l], lb_all[l])
                o, S = _hgrn_scan(q, k, g, v, og, s0s[t], l if t else 0, a_onorm_g[l])
                hgrn_out[t].append(S)
            else:
                j = l - N_A_LAYERS
                if ctx[t] is None:
                    ctx[t], new_kv[t] = build_ctx(t, x)
                c = ctx[t]
                q, gates = _nsa_in(x, mod, wq_bf[j], wg_bf[j])
                o = _nsa_attn(q, gates, c['ck'], c['cv'], *c['attn_kv'],
                              pos0=c['pos0'], win_pos0=c['win_pos0'], nc=c['nc'], ns=c['ns'])
            o_mix.append(o)
        w_o = a_w_o_bf[l] if l < N_A_LAYERS else b_w_o_bf[l - N_A_LAYERS]
        hf, te, tg = [], [], []
        for t in range(2):
            xn, h, e, gt = _mixer_out(o_mix[t], w_o, xs[t], mods[l][t], ln_g[l, 0], ln_b[l, 0], wr_pad[l], br_pad[l])
            xs[t] = xn
            hf.append(h.reshape(-1, D_MODEL // 2))
            te.append(e.reshape(-1, LANES))
            tg.append(gt)
        te_all = jnp.concatenate(te, axis=0)
        start, blk_exp, nvalid, m_pad = _moe_plan(te_all)
        pos_flat = _route(te_all, start)[:, :TOP_K].reshape(-1)
        if xr is None:
            xr = jnp.zeros((m_pad, D_MODEL // 2), jnp.uint32)
        xr = _dispatch(jnp.concatenate(hf, axis=0), pos_flat, xr)
        yr = _moe_ffn_blocks(xr, blk_exp + l * N_EXPERTS, nvalid, w_glu, w_lin, b_glu, b_lin, w_dn, b_dn)
        pos_t = [pos_flat[:Np * TOP_K], pos_flat[Np * TOP_K:]]
        for t in range(2):
            xs[t] = _combine(xs[t], yr, pos_t[t], tg[t], mods[l][t], ln_g[l, 1], ln_b[l, 1])

    return (xs[0], xs[1], jnp.stack(hgrn_out[0], axis=0), jnp.stack(hgrn_out[1], axis=0),
            new_kv[0][0], new_kv[1][0], new_kv[0][1], new_kv[1][1], new_kv[0][2], new_kv[1][2])
```

```python
import functools

import numpy as np
import jax
import jax.numpy as jnp
from jax import lax
from jax.experimental import pallas as pl
from jax.experimental.pallas import tpu as pltpu

F32 = jnp.float32
BF16 = jnp.bfloat16
I32 = jnp.int32
HIGHEST = lax.Precision.HIGHEST

D_MODEL = 1024
DEPTH = 4
PAGE_SIZE = 128
N_A_LAYERS = DEPTH // 2
A_HEADS = 8
A_DK = 128
A_DV = D_MODEL // A_HEADS
A_DF = A_HEADS * A_DK
HGRN_CHUNK = 32
B_HEADS = 16
B_DH = D_MODEL // B_HEADS
B_KV_HEADS = 4
B_GROUP = B_HEADS // B_KV_HEADS
CMP_BLOCK = 32
CMP_STRIDE = 16
CMP_RATIO = CMP_BLOCK // CMP_STRIDE
CMP_HID = 2 * B_DH
SLC_BLOCK = 64
N_SELECT = 16
WINDOW = 512
FORCE_SCORE = 1e4
N_EXPERTS = 32
TOP_K = 4
D_FF = D_MODEL
SWIGLU_LIMIT = 7.0
SWIGLU_ALPHA = 1.702
DN_ALPHA = (2 * DEPTH) ** 0.25
LN_EPS = 1e-5
RMS_EPS = 1e-6
NEG_INF = -1e30

LANES = 128
SUBLANES = 8
MXU_DIM = 256
ROW_TILE = 256
MOE_ROWS = 256
KEY_TILE = 512
VMEM_LIMIT = 56 * 1024 * 1024

AUG_POS_LANES = 4
AUG_BLK_LANE0 = 8
UNSELECTED = -(2.0 ** 30)


def _cparams(*sem):
    return pltpu.CompilerParams(dimension_semantics=sem, vmem_limit_bytes=VMEM_LIMIT)


def _row_tiles(B, T):
    if T >= ROW_TILE:
        assert T % ROW_TILE == 0
        return 1, ROW_TILE
    sb = max(1, ROW_TILE // T)
    while B % sb:
        sb //= 2
    return sb, T


def _largest_divisor(n, candidates):
    for c in candidates:
        if n % c == 0:
            return c
    raise ValueError(n)


def _sigmoid(x):
    return 1.0 / (1.0 + jnp.exp(-x))


def _linear_kernel(x_ref, w_ref, b_ref, o_ref):
    x = x_ref[...]
    x = x * _sigmoid(x)
    o_ref[...] = jnp.dot(x.astype(BF16), w_ref[...].astype(BF16), preferred_element_type=F32) + b_ref[...]


def _silu_linear(x, w, b):
    L, K, N = w.shape
    M = x.shape[0]
    tn = 1024
    return pl.pallas_call(
        _linear_kernel,
        grid=(L, N // tn),
        in_specs=[pl.BlockSpec((M, K), lambda l, j: (0, 0)),
                  pl.BlockSpec((None, K, tn), lambda l, j: (l, 0, j)),
                  pl.BlockSpec((None, 1, tn), lambda l, j: (l, 0, j))],
        out_specs=pl.BlockSpec((None, M, tn), lambda l, j: (l, 0, j)),
        out_shape=jax.ShapeDtypeStruct((L, M, N), F32),
        compiler_params=_cparams("arbitrary", "arbitrary"),
        name="silu_linear",
    )(x, w, b.reshape(L, 1, N))


def _modulate(x_ref, sc_ref, sh_ref):
    sb, tt, d = x_ref.shape
    h = x_ref[...] * (1.0 + sc_ref[...]) + sh_ref[...]
    return h.reshape(sb * tt, d)


def _hgrn_in_kernel(x_ref, sh_ref, sc_ref, w_ref, lb_ref, q_ref, k_ref, g_ref, v_ref, og_ref):
    sb, tt, d = x_ref.shape
    h = _modulate(x_ref, sc_ref, sh_ref).astype(BF16)
    proj = jnp.dot(h, w_ref[...], preferred_element_type=F32)
    qz = proj[:, :A_DF]
    fz = proj[:, A_DF:2 * A_DF]
    lb = lb_ref[...]
    log_sig = jnp.minimum(fz, 0.0) - jnp.log1p(jnp.exp(-jnp.abs(fz)))
    pos = lb > 0.0
    log_lb = jnp.where(pos, jnp.log(jnp.where(pos, lb, 1.0)), NEG_INF)
    other = jnp.log1p(-lb) + log_sig
    logf = jnp.maximum(log_lb, other) + jnp.log1p(jnp.exp(-jnp.abs(log_lb - other)))
    q_ref[...] = (qz * _sigmoid(qz)).reshape(sb, tt, A_DF)
    g_ref[...] = logf.reshape(sb, tt, A_DF)
    k_ref[...] = ((1.0 - lb) * _sigmoid(-fz)).reshape(sb, tt, A_DF)
    v_ref[...] = proj[:, 2 * A_DF:2 * A_DF + D_MODEL].reshape(sb, tt, D_MODEL)
    og_ref[...] = proj[:, 2 * A_DF + D_MODEL:].reshape(sb, tt, D_MODEL)


def _mod_spec(sb, comp):
    return pl.BlockSpec((sb, 1, D_MODEL), lambda i, j, c=comp: (i, 0, c))


def _tok_spec(sb, tt, width=D_MODEL):
    return pl.BlockSpec((sb, tt, width), lambda i, j: (i, j, 0))


def _hgrn_in(x, mod, w_bf, lb):
    B, T, _ = x.shape
    sb, tt = _row_tiles(B, T)
    n_out = w_bf.shape[1]
    shp = jax.ShapeDtypeStruct((B, T, D_MODEL), F32)
    return pl.pallas_call(
        _hgrn_in_kernel,
        grid=(B // sb, T // tt),
        in_specs=[_tok_spec(sb, tt), _mod_spec(sb, 0), _mod_spec(sb, 1),
                  pl.BlockSpec((D_MODEL, n_out), lambda i, j: (0, 0)),
                  pl.BlockSpec((1, A_DF), lambda i, j: (0, 0))],
        out_specs=[_tok_spec(sb, tt)] * 5,
        out_shape=[shp] * 5,
        compiler_params=_cparams("arbitrary", "arbitrary"),
        name="hgrn_in",
    )(x, mod, mod, w_bf, lb.reshape(1, A_DF))


def _nsa_in_kernel(x_ref, sh_ref, sc_ref, wq_ref, wg_ref, q_ref, gate_ref):
    sb, tt, d = x_ref.shape
    h = _modulate(x_ref, sc_ref, sh_ref).astype(BF16)
    q = jnp.dot(h, wq_ref[...], preferred_element_type=F32) * (B_DH ** -0.5)
    gz = jnp.dot(h, wg_ref[...], preferred_element_type=F32)
    q_ref[...] = q.reshape(sb, tt, D_MODEL)
    gate_ref[...] = _sigmoid(gz).reshape(sb, tt, LANES)


def _nsa_in(x, mod, wq_bf, wg_bf):
    B, T, _ = x.shape
    sb, tt = _row_tiles(B, T)
    return pl.pallas_call(
        _nsa_in_kernel,
        grid=(B // sb, T // tt),
        in_specs=[_tok_spec(sb, tt), _mod_spec(sb, 0), _mod_spec(sb, 1),
                  pl.BlockSpec((D_MODEL, D_MODEL), lambda i, j: (0, 0)),
                  pl.BlockSpec((D_MODEL, LANES), lambda i, j: (0, 0))],
        out_specs=[_tok_spec(sb, tt), _tok_spec(sb, tt, LANES)],
        out_shape=[jax.ShapeDtypeStruct((B, T, D_MODEL), F32), jax.ShapeDtypeStruct((B, T, LANES), F32)],
        compiler_params=_cparams("arbitrary", "arbitrary"),
        name="nsa_in",
    )(x, mod, mod, wq_bf, wg_bf)


KV_RAW = 6 * B_KV_HEADS * B_DH
KV_DUP = 4 * B_KV_HEADS * LANES


def _kv_proj_kernel(x_ref, sh_ref, sc_ref, w_ref, kv_ref, dup_ref):
    sb, tt, d = x_ref.shape
    h = _modulate(x_ref, sc_ref, sh_ref).astype(BF16)
    kv = jnp.dot(h, w_ref[...], preferred_element_type=F32)
    kv_ref[...] = kv[:, :KV_RAW].reshape(sb, tt, KV_RAW)
    dup_ref[...] = kv[:, KV_RAW:].reshape(sb, tt, KV_DUP).astype(dup_ref.dtype)


def _kv_proj(x, kvmod, w_bf, dup_dtype):
    B, T, _ = x.shape
    sb, tt = _row_tiles(B, T)
    n_out = w_bf.shape[1]
    return pl.pallas_call(
        _kv_proj_kernel,
        grid=(B // sb, T // tt),
        in_specs=[_tok_spec(sb, tt), _mod_spec(sb, 0), _mod_spec(sb, 1),
                  pl.BlockSpec((D_MODEL, n_out), lambda i, j: (0, 0))],
        out_specs=[_tok_spec(sb, tt, KV_RAW), _tok_spec(sb, tt, KV_DUP)],
        out_shape=[jax.ShapeDtypeStruct((B, T, KV_RAW), F32), jax.ShapeDtypeStruct((B, T, KV_DUP), dup_dtype)],
        compiler_params=_cparams("arbitrary", "arbitrary"),
        name="kv_proj",
    )(x, kvmod, kvmod, w_bf)


def _layer_norm_rows(z, g, b):
    mu = jnp.mean(z, axis=-1, keepdims=True)
    zc = z - mu
    var = jnp.mean(zc * zc, axis=-1, keepdims=True)
    return zc * lax.rsqrt(var + LN_EPS) * g + b


def _mixer_out_kernel(o_ref, w_ref, x_ref, gm_ref, shf_ref, scf_ref, lng_ref, lnb_ref, wr_ref, br_ref,
                      xn_ref, hf_ref, te_ref, tg_ref):
    sb, tt, d = x_ref.shape
    rows = sb * tt
    o = o_ref[...].reshape(rows, d).astype(BF16)
    y = jnp.dot(o, w_ref[...], preferred_element_type=F32).reshape(sb, tt, d)
    z = DN_ALPHA * x_ref[...] + (1.0 + gm_ref[...]) * y
    xn = _layer_norm_rows(z, lng_ref[...], lnb_ref[...])
    xn_ref[...] = xn
    hf = xn * (1.0 + scf_ref[...]) + shf_ref[...]
    bits = pltpu.bitcast(hf.astype(BF16).astype(F32).reshape(rows, d), jnp.uint32)
    hf_ref[...] = (bits[:, :d // 2] | (bits[:, d // 2:] >> 16)).reshape(sb, tt, d // 2)
    logits = jnp.dot(hf.reshape(rows, d), wr_ref[...], precision=HIGHEST,
                     preferred_element_type=F32) + br_ref[...]
    lane = lax.broadcasted_iota(I32, (rows, LANES), 1)
    te = jnp.zeros((rows, LANES), I32)
    tv = jnp.zeros((rows, LANES), F32)
    v0 = None
    denom = None
    for k in range(TOP_K):
        m = jnp.max(logits, axis=-1, keepdims=True)
        idx = jnp.min(jnp.where(logits == m, lane, LANES), axis=-1, keepdims=True)
        if k == 0:
            v0 = m
            e = jnp.ones_like(m)
            denom = e
        else:
            e = jnp.exp(m - v0)
            denom = denom + e
        te = jnp.where(lane == k, idx, te)
        tv = jnp.where(lane == k, e, tv)
        logits = jnp.where(lane == idx, -jnp.inf, logits)
    te_ref[...] = te.reshape(sb, tt, LANES)
    tg_ref[...] = (tv / denom).reshape(sb, tt, LANES)


def _mixer_out(o, w_bf, x, mod, ln_g, ln_b, wr_pad, br_pad):
    B, T, _ = x.shape
    sb, tt = _row_tiles(B, T)
    vec = pl.BlockSpec((1, 1, D_MODEL), lambda i, j: (0, 0, 0))
    return pl.pallas_call(
        _mixer_out_kernel,
        grid=(B // sb, T // tt),
        in_specs=[_tok_spec(sb, tt), pl.BlockSpec((D_MODEL, D_MODEL), lambda i, j: (0, 0)), _tok_spec(sb, tt),
                  _mod_spec(sb, 2), _mod_spec(sb, 3), _mod_spec(sb, 4), vec, vec,
                  pl.BlockSpec((D_MODEL, LANES), lambda i, j: (0, 0)),
                  pl.BlockSpec((1, LANES), lambda i, j: (0, 0))],
        out_specs=[_tok_spec(sb, tt), _tok_spec(sb, tt, D_MODEL // 2), _tok_spec(sb, tt, LANES), _tok_spec(sb, tt, LANES)],
        out_shape=[jax.ShapeDtypeStruct((B, T, D_MODEL), F32), jax.ShapeDtypeStruct((B, T, D_MODEL // 2), jnp.uint32),
                   jax.ShapeDtypeStruct((B, T, LANES), I32), jax.ShapeDtypeStruct((B, T, LANES), F32)],
        compiler_params=_cparams("arbitrary", "arbitrary"),
        name="mixer_out",
    )(o, w_bf, x, mod, mod, mod, ln_g.reshape(1, 1, D_MODEL), ln_b.reshape(1, 1, D_MODEL), wr_pad, br_pad)


def _hgrn_scan_kernel(q_ref, k_ref, g_ref, v_ref, og_ref, s0_ref, gn_ref, o_ref, sout_ref,
                      st_ref, b_ref, k_scr, v_scr, *, chunk, n_chunks):
    tb = pl.program_id(1)
    C = chunk
    nsub = C // SUBLANES

    @pl.when(tb == 0)
    def _():
        for h in range(A_HEADS):
            st_ref[h] = s0_ref[0, h].T

    row = lax.broadcasted_iota(I32, (C, C), 0)
    col = lax.broadcasted_iota(I32, (C, C), 1)
    tri = (row >= col).astype(F32)
    sub_row = lax.broadcasted_iota(I32, (SUBLANES, A_DK), 0)
    gn = gn_ref[...]

    def chunk_body(c, carry):
        r0 = pl.multiple_of(c * C, C)
        for h in range(A_HEADS):
            ls = slice(h * A_DK, (h + 1) * A_DK)
            g = g_ref[0, pl.ds(r0, C), ls]
            b = jnp.dot(tri, g, precision=HIGHEST, preferred_element_type=F32)
            b_ref[...] = b
            q = q_ref[0, pl.ds(r0, C), ls]
            k = k_ref[0, pl.ds(r0, C), ls]
            v = v_ref[0, pl.ds(r0, C), ls]
            k_scr[...] = k
            v_scr[...] = v
            st = st_ref[h]
            o_inter = lax.dot_general((q * jnp.exp(b)).astype(BF16), st.astype(BF16),
                                      (((1,), (1,)), ((), ())), preferred_element_type=F32)
            acc = [jnp.zeros((SUBLANES, A_DV), F32) for _ in range(nsub)]
            for s in range(C):
                j = s // SUBLANES
                b_s = b_ref[pl.ds(s, 1), :]
                k_s = k_scr[pl.ds(s, 1), :]
                v_s = v_scr[pl.ds(s, 1), :]
                for i in range(j, nsub):
                    rs = slice(i * SUBLANES, (i + 1) * SUBLANES)
                    dlt = b[rs] - b_s
                    if i == j:
                        causal = sub_row >= (s % SUBLANES)
                        e = jnp.where(causal, jnp.exp(jnp.where(causal, dlt, 0.0)), 0.0)
                    else:
                        e = jnp.exp(dlt)
                    w = jnp.sum(q[rs] * e * k_s, axis=-1, keepdims=True)
                    acc[i] = acc[i] + w * v_s
            o = o_inter + (jnp.concatenate(acc, axis=0) if nsub > 1 else acc[0])
            b_last = b_ref[pl.ds(C - 1, 1), :]
            kd = k * jnp.exp(b_last - b)
            st_new = st * jnp.exp(b_last) + lax.dot_general(
                v.astype(BF16), kd.astype(BF16), (((0,), (0,)), ((), ())), preferred_element_type=F32)
            st_ref[h] = st_new
            og = og_ref[0, pl.ds(r0, C), ls]
            on = o * lax.rsqrt(jnp.mean(o * o, axis=-1, keepdims=True) + RMS_EPS) * gn
            o_ref[0, pl.ds(r0, C), ls] = on * (og * _sigmoid(og))
        return carry

    lax.fori_loop(0, n_chunks, chunk_body, 0)

    @pl.when(tb == pl.num_programs(1) - 1)
    def _():
        for h in range(A_HEADS):
            sout_ref[0, h] = st_ref[h].T


def _hgrn_scan(q, k, g, v, og, s0_all, layer, gn):
    B, T, _ = q.shape
    C = HGRN_CHUNK if T % HGRN_CHUNK == 0 else T
    assert C % SUBLANES == 0
    tblk = min(T, ROW_TILE)
    assert T % tblk == 0 and tblk % C == 0
    tok = pl.BlockSpec((1, tblk, D_MODEL), lambda b, t: (b, t, 0))
    st = pl.BlockSpec((1, A_HEADS, A_DK, A_DV), lambda b, t: (b, 0, 0, 0))
    st_in = pl.BlockSpec((None, 1, A_HEADS, A_DK, A_DV), lambda b, t: (layer, b, 0, 0, 0))
    return pl.pallas_call(
        functools.partial(_hgrn_scan_kernel, chunk=C, n_chunks=tblk // C),
        grid=(B, T // tblk),
        in_specs=[tok, tok, tok, tok, tok, st_in, pl.BlockSpec((1, A_DV), lambda b, t: (0, 0))],
        out_specs=[tok, st],
        out_shape=[jax.ShapeDtypeStruct((B, T, D_MODEL), F32),
                   jax.ShapeDtypeStruct((B, A_HEADS, A_DK, A_DV), F32)],
        scratch_shapes=[pltpu.VMEM((A_HEADS, A_DV, A_DK), F32), pltpu.VMEM((C, A_DK), F32),
                        pltpu.VMEM((C, A_DK), F32), pltpu.VMEM((C, A_DV), F32)],
        compiler_params=_cparams("arbitrary", "arbitrary"),
        name="hgrn_scan",
    )(q, k, g, v, og, s0_all, gn.reshape(1, A_DV))


def _cmp_const_kernel(pos_ref, w1_ref, b1_ref, o_ref):
    pos8 = jnp.broadcast_to(pos_ref[...], (SUBLANES, pos_ref.shape[1]))
    o_ref[...] = jnp.dot(pos8, w1_ref[...], precision=HIGHEST, preferred_element_type=F32)[0:1] + b1_ref[...]


def _cmp_const(pos_flat, w1, b1):
    kdim = w1.shape[1]
    return pl.pallas_call(
        _cmp_const_kernel,
        grid=(2,),
        in_specs=[pl.BlockSpec((None, 1, kdim), lambda a: (a, 0, 0)),
                  pl.BlockSpec((None, kdim, CMP_HID), lambda a: (a, 0, 0)),
                  pl.BlockSpec((None, 1, CMP_HID), lambda a: (a, 0, 0))],
        out_specs=pl.BlockSpec((None, 1, CMP_HID), lambda a: (a, 0, 0)),
        out_shape=jax.ShapeDtypeStruct((2, 1, CMP_HID), F32),
        compiler_params=_cparams("arbitrary"),
        name="cmp_const",
    )(pos_flat, w1, b1)


def _compress_kernel(x_ref, w1_ref, cvec_ref, w2_ref, b2_ref, o_ref):
    gb, n, kdim = x_ref.shape
    rows = gb * n
    uv = jnp.dot(x_ref[...].reshape(rows, kdim).astype(BF16), w1_ref[...].astype(BF16),
                 preferred_element_type=F32)
    u = uv[:, :CMP_HID]
    v = uv[:, CMP_HID:]
    v_next = pltpu.roll(v, rows - 1, axis=0)
    hid = u + v_next + cvec_ref[...]
    act = hid * _sigmoid(hid)
    out = jnp.dot(act.astype(BF16), w2_ref[...].astype(BF16), preferred_element_type=F32) + b2_ref[...]
    o_ref[...] = out.reshape(gb, n, LANES)


def _compress(xc, w1cat, cvec, w2dup, b2dup):
    _, BG, nch, kdim = xc.shape
    gb = _largest_divisor(BG, (8, 4, 2, 1))
    return pl.pallas_call(
        _compress_kernel,
        grid=(2, BG // gb),
        in_specs=[pl.BlockSpec((None, gb, nch, kdim), lambda a, i: (a, i, 0, 0)),
                  pl.BlockSpec((None, kdim, 2 * CMP_HID), lambda a, i: (a, 0, 0)),
                  pl.BlockSpec((None, 1, CMP_HID), lambda a, i: (a, 0, 0)),
                  pl.BlockSpec((None, CMP_HID, LANES), lambda a, i: (a, 0, 0)),
                  pl.BlockSpec((None, 1, LANES), lambda a, i: (a, 0, 0))],
        out_specs=pl.BlockSpec((None, gb, nch, LANES), lambda a, i: (a, i, 0, 0)),
        out_shape=jax.ShapeDtypeStruct((2, BG, nch, LANES), F32),
        compiler_params=_cparams("arbitrary", "arbitrary"),
        name="compress_kv",
    )(xc, w1cat, cvec, w2dup, b2dup)


def _compress_paged_kernel(pt_ref, *refs, n_seq_pages):
    del pt_ref
    page_refs = refs[:n_seq_pages]
    w1_ref, cvec_ref, w2_ref, b2_ref, o_ref = refs[n_seq_pages:]
    nb, n_groups, nch, _ = o_ref.shape
    kdim = w1_ref.shape[0]
    w1 = w1_ref[...].astype(BF16)
    w2 = w2_ref[...].astype(BF16)
    for g in range(n_groups):
        x = jnp.concatenate([r[:, g * kdim:(g + 1) * kdim] for r in page_refs], axis=0)
        rows = x.shape[0]
        uv = jnp.dot(x.astype(BF16), w1, preferred_element_type=F32)
        v_next = pltpu.roll(uv[:, CMP_HID:], rows - 1, axis=0)
        hid = uv[:, :CMP_HID] + v_next + cvec_ref[...]
        act = hid * _sigmoid(hid)
        out = jnp.dot(act.astype(BF16), w2, preferred_element_type=F32) + b2_ref[...]
        o_ref[:, g] = out.reshape(nb, nch, LANES)


def _compress_paged(pool_t, table, w1cat, cvec, w2dup, b2dup):
    _, _, cpp, width = pool_t.shape
    G = B_KV_HEADS
    kdim = width // G
    B, n_pages = table.shape
    nb = _largest_divisor(B, (2, 1))
    nch = n_pages * cpp
    page_specs = [pl.BlockSpec((None, None, cpp, width), lambda a, i, pt, s=s, p=p: (a, pt[i * nb + s, p], 0, 0))
                  for s in range(nb) for p in range(n_pages)]
    grid_spec = pltpu.PrefetchScalarGridSpec(
        num_scalar_prefetch=1,
        grid=(2, B // nb),
        in_specs=page_specs + [pl.BlockSpec((None, kdim, 2 * CMP_HID), lambda a, i, pt: (a, 0, 0)),
                               pl.BlockSpec((None, 1, CMP_HID), lambda a, i, pt: (a, 0, 0)),
                               pl.BlockSpec((None, CMP_HID, LANES), lambda a, i, pt: (a, 0, 0)),
                               pl.BlockSpec((None, 1, LANES), lambda a, i, pt: (a, 0, 0))],
        out_specs=pl.BlockSpec((None, nb, G, nch, LANES), lambda a, i, pt: (a, i, 0, 0, 0)),
    )
    return pl.pallas_call(
        functools.partial(_compress_paged_kernel, n_seq_pages=nb * n_pages),
        grid_spec=grid_spec,
        out_shape=jax.ShapeDtypeStruct((2, B, G, nch, LANES), F32),
        compiler_params=_cparams("arbitrary", "arbitrary"),
        name="compress_paged",
    )(table, *([pool_t] * (nb * n_pages)), w1cat, cvec, w2dup, b2dup)


def _pages_kernel(pt_ref, *refs, n_pages):
    del pt_ref
    page_refs = refs[:n_pages]
    newk_ref, newv_ref, kt_ref, v_ref = refs[n_pages:]
    for p in range(n_pages):
        rows = slice(p * PAGE_SIZE, (p + 1) * PAGE_SIZE)
        page = page_refs[p]
        for g in range(B_KV_HEADS):
            kt = page[0, g]
            kt_ref[0, g, :, rows] = jnp.concatenate([kt, kt], axis=0)
            vt = jnp.concatenate([page[1, g], page[1, g ^ 1]], axis=0).astype(F32)
            v_ref[0, rows, g * LANES:(g + 1) * LANES] = vt.T.astype(BF16)
    t_new = newk_ref.shape[1]
    tail = v_ref.shape[1] - n_pages * PAGE_SIZE
    pad = jnp.zeros((tail - t_new, newk_ref.shape[2]), F32)
    newk = jnp.concatenate([newk_ref[0], pad], axis=0)
    newv = jnp.concatenate([newv_ref[0], pad], axis=0)
    v_ref[0, n_pages * PAGE_SIZE:, :] = newv.astype(BF16)
    for g in range(B_KV_HEADS):
        for j in range(tail // PAGE_SIZE):
            blk = newk[j * PAGE_SIZE:(j + 1) * PAGE_SIZE, g * LANES:(g + 1) * LANES]
            c0 = (n_pages + j) * PAGE_SIZE
            kt_ref[0, g, :, c0:c0 + PAGE_SIZE] = blk.T.astype(BF16)


def _pages_to_kv(pool, table, new_dup, k_col, v_col, total_len):
    B, n_pages = table.shape
    T = new_dup.shape[1]
    assert total_len % PAGE_SIZE == 0 and total_len >= n_pages * PAGE_SIZE + T
    width = B_KV_HEADS * LANES
    page_specs = [pl.BlockSpec((None, 2, B_KV_HEADS, B_DH, PAGE_SIZE), lambda b, pt, p=p: (pt[b, p], 0, 0, 0, 0))
                  for p in range(n_pages)]
    grid_spec = pltpu.PrefetchScalarGridSpec(
        num_scalar_prefetch=1,
        grid=(B,),
        in_specs=page_specs + [pl.BlockSpec((1, T, width), lambda b, pt, c=k_col: (b, 0, c)),
                               pl.BlockSpec((1, T, width), lambda b, pt, c=v_col: (b, 0, c))],
        out_specs=[pl.BlockSpec((1, B_KV_HEADS, LANES, total_len), lambda b, pt: (b, 0, 0, 0)),
                   pl.BlockSpec((1, total_len, width), lambda b, pt: (b, 0, 0))],
    )
    return pl.pallas_call(
        functools.partial(_pages_kernel, n_pages=n_pages),
        grid_spec=grid_spec,
        out_shape=[jax.ShapeDtypeStruct((B, B_KV_HEADS, LANES, total_len), BF16),
                   jax.ShapeDtypeStruct((B, total_len, width), BF16)],
        compiler_params=_cparams("arbitrary"),
        name="pages_to_kv",
    )(table, *([pool] * n_pages), new_dup, new_dup)


def _cmp_to_slc_map_t(ncp, ns):
    j = np.arange(LANES)[:, None]
    i = np.arange(ncp)[None, :]
    lo = np.maximum(i * CMP_STRIDE, j * SLC_BLOCK)
    hi = np.minimum(i * CMP_STRIDE + CMP_BLOCK, (j + 1) * SLC_BLOCK)
    m = np.maximum(hi - lo, 0) / CMP_STRIDE
    return jnp.asarray(np.where(j < ns, m, 0.0), dtype=F32)


def _aug_table_t(length, pos0, with_blocks):
    pos = np.arange(length) + pos0
    a = np.zeros((LANES, length), np.float32)
    a[0] = (pos // MXU_DIM) * MXU_DIM
    a[1] = pos % MXU_DIM
    a[2] = a[0]
    a[3] = a[1]
    if with_blocks:
        blk = np.arange(length) // SLC_BLOCK
        assert AUG_BLK_LANE0 + blk.max() < LANES
        a[AUG_BLK_LANE0 + blk, np.arange(length)] = 1.0
    return jnp.asarray(a, dtype=BF16)


def _split_bf16(x):
    hi = float(np.float32(x).astype(BF16).astype(np.float32))
    lo = float(np.float32(x - hi).astype(BF16).astype(np.float32))
    return hi, lo


def _nsa_attn_kernel(q_ref, gate_ref, ck_ref, cv_ref, cmapt_ref, kts_ref, vs_ref, ktw_ref, vw_ref,
                     augs_ref, augw_ref, o_ref, qa_ref, m_ref, acc_ref, ocmp_ref, sc_ref,
                     *, tq, gpar, pos0, win_pos0, nc, ns, n_sel):
    qi = pl.program_id(1)
    R = B_GROUP
    grows = R * tq
    rows = gpar * grows
    chunk = min(grows, LANES)
    n_chunk = rows // chunk
    ncp = ck_ref.shape[2]
    nsp = sc_ref.shape[0]
    n_slc_tiles = vs_ref.shape[1] // KEY_TILE
    n_win_tiles = vw_ref.shape[1] // KEY_TILE
    half = B_DH
    lane_q = lax.broadcasted_iota(I32, (tq, LANES), 1)
    lo_q = lane_q < half
    ones_lane = lax.broadcasted_iota(I32, (KEY_TILE, LANES), 1) == half
    t_idx = lax.broadcasted_iota(I32, (tq, 1), 0)
    q_lo = pos0 + qi * tq
    q_hi = q_lo + tq - 1
    qpos_i = q_lo + t_idx
    qpos_g = jnp.concatenate([qpos_i] * R, axis=0)
    qpos_all = jnp.concatenate([qpos_g] * gpar, axis=0) if gpar > 1 else qpos_g
    qpos_f = qpos_g.astype(F32)
    kidx = lax.broadcasted_iota(I32, (1, KEY_TILE), 1)
    blk_t = lax.broadcasted_iota(I32, (nsp, tq), 0)
    qpos_t = q_lo + lax.broadcasted_iota(I32, (nsp, tq), 1)
    place = (lax.broadcasted_iota(I32, (nsp, LANES), 1) ==
             lax.broadcasted_iota(I32, (nsp, LANES), 0) + AUG_BLK_LANE0).astype(BF16)
    sel_lanes = (lane_q >= AUG_BLK_LANE0) & (lane_q < AUG_BLK_LANE0 + ns)

    def flash_init():
        m_ref[...] = jnp.full(m_ref.shape, NEG_INF, F32)
        acc_ref[...] = jnp.zeros(acc_ref.shape, F32)

    def flash_tile(kbts, vbs, mask_fn):
        cs = range(n_chunk)
        rs = [pl.ds(c * chunk, chunk) for c in cs]
        gi = [(c * chunk) // grows for c in cs]
        ss = []
        for c in cs:
            s = jnp.dot(qa_ref[rs[c], :], kbts[gi[c]], preferred_element_type=F32)
            if mask_fn is not None:
                s = jnp.where(mask_fn(qpos_all[c * chunk:(c + 1) * chunk]), s, NEG_INF)
            ss.append(s)
        m_old = [m_ref[rs[c], :] for c in cs]
        m_new = [jnp.maximum(m_old[c], jnp.broadcast_to(jnp.max(ss[c], axis=-1, keepdims=True), (chunk, LANES)))
                 for c in cs]
        es = [jnp.exp(ss[c] - jnp.concatenate([m_new[c]] * (KEY_TILE // LANES), axis=1)) for c in cs]
        for c in cs:
            alpha = jnp.exp(m_old[c] - m_new[c])
            acc_ref[rs[c], :] = alpha * acc_ref[rs[c], :] + jnp.dot(es[c].astype(BF16), vbs[gi[c]],
                                                                   preferred_element_type=F32)
            m_ref[rs[c], :] = m_new[c]

    def flash_done():
        acc = acc_ref[...]
        return acc / acc[:, half:half + 1]

    for g0 in range(0, B_KV_HEADS, gpar):
        groups = list(range(g0, g0 + gpar))
        for gi, g in enumerate(groups):
            c0 = q_ref[0, :, g * 2 * LANES:g * 2 * LANES + LANES]
            c1 = q_ref[0, :, g * 2 * LANES + LANES:(g + 1) * 2 * LANES]
            q_st = jnp.concatenate([jnp.where(lo_q, c0, 0.0), jnp.where(lo_q, 0.0, c0),
                                    jnp.where(lo_q, c1, 0.0), jnp.where(lo_q, 0.0, c1)], axis=0).astype(BF16)
            slopes = [2.0 ** (-8.0 * (g * R + r + 1) / B_HEADS) for r in range(R)]
            slope = jnp.concatenate([jnp.full((tq, 1), s, F32) for s in slopes], axis=0)

            ckb = ck_ref[0, g].astype(BF16)
            cvb = cv_ref[0, g].astype(BF16)
            s = lax.dot_general(q_st, ckb, (((1,), (1,)), ((), ())), preferred_element_type=F32)
            c_iota = lax.broadcasted_iota(I32, (1, ncp), 1)
            c_end = (c_iota * CMP_STRIDE + (CMP_BLOCK - 1)).astype(F32)
            dist = qpos_f - c_end
            mask = (dist >= 0.0) & (c_iota < nc)
            s = jnp.where(mask, s - slope * dist, NEG_INF)
            m = jnp.max(s, axis=-1, keepdims=True)
            e = jnp.where(mask, jnp.exp(s - m), 0.0)
            p = e / jnp.maximum(jnp.sum(e, axis=-1, keepdims=True), 1e-30)
            ocmp_ref[gi * grows:(gi + 1) * grows, :] = jnp.dot(p.astype(BF16), cvb, preferred_element_type=F32)

            p_sum = p[0:tq] + p[tq:2 * tq] + p[2 * tq:3 * tq] + p[3 * tq:4 * tq]
            imp_t = lax.dot_general(cmapt_ref[...], p_sum, (((1,), (1,)), ((), ())), precision=HIGHEST,
                                    preferred_element_type=F32)[0:nsp]
            cur = qpos_t // SLC_BLOCK
            forced = (blk_t == 0) | (blk_t == cur) | (blk_t == cur - 1)
            score = jnp.where(forced, FORCE_SCORE, jnp.where(blk_t * SLC_BLOCK <= qpos_t, imp_t, -1.0))
            score = jnp.where(blk_t < ns, score, -jnp.inf)
            sc_ref[...] = score
            rank = jnp.zeros((nsp, tq), F32)
            for j in range(ns):
                other = sc_ref[pl.ds(j, 1), :]
                ahead = (other > score) | ((other == score) & (j < blk_t))
                rank = rank + ahead.astype(F32)
            sel_t = ((rank < float(n_sel)) & (score >= 0.0)).astype(BF16)
            sel_q = lax.dot_general(sel_t, place, (((0,), (0,)), ((), ())), preferred_element_type=F32)
            sel_term = jnp.where(sel_q > 0.5, 0.0, UNSELECTED)

            q_hi_half = []
            for r in range(R):
                s_hi, s_lo = _split_bf16(slopes[r])
                consts = jnp.where(lane_q < 2, s_hi, jnp.where(lane_q < AUG_POS_LANES, s_lo, 0.0))
                q_hi_half.append(jnp.where(sel_lanes, sel_term, consts))
            qa_ref[gi * grows:(gi + 1) * grows, :] = jnp.concatenate(
                [q_st, jnp.concatenate(q_hi_half, axis=0).astype(BF16)], axis=1)

        def key_tiles(kt_ref, v_ref, aug_ref, t0):
            kbts = [jnp.concatenate([kt_ref[0, g, :, pl.ds(t0, KEY_TILE)], aug_ref[:, pl.ds(t0, KEY_TILE)]], axis=0)
                    for g in groups]
            vbs = [jnp.where(ones_lane, 1.0, v_ref[0, pl.ds(t0, KEY_TILE), g * LANES:(g + 1) * LANES]).astype(BF16)
                   for g in groups]
            return kbts, vbs

        n_kt = jnp.minimum(q_hi // KEY_TILE + 1, n_slc_tiles)
        n_full = jnp.minimum((q_lo + 1) // KEY_TILE, n_kt)

        def slc_tile(kt, masked):
            t0 = pl.multiple_of(kt * KEY_TILE, KEY_TILE)
            kbts, vbs = key_tiles(kts_ref, vs_ref, augs_ref, t0)
            flash_tile(kbts, vbs, (lambda qp: (t0 + kidx) <= qp) if masked else None)

        flash_init()
        lax.fori_loop(0, n_full, lambda kt, c: (slc_tile(kt, False), c)[1], 0)
        lax.fori_loop(n_full, n_kt, lambda kt, c: (slc_tile(kt, True), c)[1], 0)
        o_slc = flash_done()

        w_first = jnp.maximum(q_lo - (WINDOW - 1) - win_pos0, 0) // KEY_TILE
        w_last = jnp.minimum((q_hi - win_pos0) // KEY_TILE + 1, n_win_tiles)

        def win_tile(kt, c):
            t0 = pl.multiple_of(kt * KEY_TILE, KEY_TILE)
            kbts, vbs = key_tiles(ktw_ref, vw_ref, augw_ref, t0)
            kpos = win_pos0 + t0 + kidx
            flash_tile(kbts, vbs, lambda qp: (kpos <= qp) & (kpos > qp - WINDOW))
            return c

        flash_init()
        lax.fori_loop(w_first, w_last, win_tile, 0)
        o_win = flash_done()

        for gi, g in enumerate(groups):
            gr = slice(gi * grows, (gi + 1) * grows)

            def gate_col(j):
                base = j * B_HEADS + g * R
                return jnp.concatenate([gate_ref[0, :, base + r:base + r + 1] for r in range(R)], axis=0)

            o_st = gate_col(0) * ocmp_ref[gr, :] + gate_col(1) * o_slc[gr] + gate_col(2) * o_win[gr]
            o_up = pltpu.roll(o_st, half, axis=1)
            o_ref[0, :, g * 2 * LANES:g * 2 * LANES + LANES] = jnp.where(lo_q, o_st[0:tq], o_up[tq:2 * tq])
            o_ref[0, :, g * 2 * LANES + LANES:(g + 1) * 2 * LANES] = jnp.where(lo_q, o_st[2 * tq:3 * tq], o_up[3 * tq:])


def _nsa_attn(q, gates, ck, cv, kt_slc, v_slc, v_slc_col, kt_win, v_win, v_win_col, *, pos0, win_pos0, nc, ns):
    B, T, _ = q.shape
    tq = _largest_divisor(T, (4 * LANES, 2 * LANES, LANES, 64, 32, 16, 8))
    gpar = B_KV_HEADS if B_GROUP * tq < LANES else 1
    ncp = ck.shape[2]
    n_sel = min(N_SELECT, ns)
    nsp = -(-ns // (2 * SUBLANES)) * (2 * SUBLANES)
    Ls, Lw = kt_slc.shape[3], kt_win.shape[3]
    assert Ls % KEY_TILE == 0 and Lw % KEY_TILE == 0 and v_slc.shape[1] == Ls and v_win.shape[1] == Lw
    rows = gpar * B_GROUP * tq
    width = B_KV_HEADS * LANES
    kern = functools.partial(_nsa_attn_kernel, tq=tq, gpar=gpar, pos0=pos0, win_pos0=win_pos0, nc=nc, ns=ns,
                             n_sel=n_sel)
    kt_spec = lambda L: pl.BlockSpec((1, B_KV_HEADS, LANES, L), lambda b, i: (b, 0, 0, 0))
    v_spec = lambda L, col: pl.BlockSpec((1, L, width), lambda b, i, c=col: (b, 0, c))
    return pl.pallas_call(
        kern,
        grid=(B, T // tq),
        in_specs=[pl.BlockSpec((1, tq, D_MODEL), lambda b, i: (b, i, 0)),
                  pl.BlockSpec((1, tq, LANES), lambda b, i: (b, i, 0)),
                  pl.BlockSpec((1, B_KV_HEADS, ncp, LANES), lambda b, i: (b, 0, 0, 0)),
                  pl.BlockSpec((1, B_KV_HEADS, ncp, LANES), lambda b, i: (b, 0, 0, 0)),
                  pl.BlockSpec((LANES, ncp), lambda b, i: (0, 0)),
                  kt_spec(Ls), v_spec(Ls, v_slc_col), kt_spec(Lw), v_spec(Lw, v_win_col),
                  pl.BlockSpec((LANES, Ls), lambda b, i: (0, 0)),
                  pl.BlockSpec((LANES, Lw), lambda b, i: (0, 0))],
        out_specs=pl.BlockSpec((1, tq, D_MODEL), lambda b, i: (b, i, 0)),
        out_shape=jax.ShapeDtypeStruct((B, T, D_MODEL), F32),
        scratch_shapes=[pltpu.VMEM((rows, 2 * LANES), BF16), pltpu.VMEM((rows, LANES), F32),
                        pltpu.VMEM((rows, LANES), F32), pltpu.VMEM((rows, LANES), F32),
                        pltpu.VMEM((nsp, tq), F32)],
        compiler_params=_cparams("arbitrary", "arbitrary"),
        name="nsa_attn",
    )(q, gates, ck, cv, _cmp_to_slc_map_t(ncp, ns), kt_slc, v_slc, kt_win, v_win,
      _aug_table_t(Ls, 0, True), _aug_table_t(Lw, win_pos0, False))


def _deinterleave_kernel(w_ref, perm_ref, glu_ref, lin_ref):
    for c in range(w_ref.shape[1] // MXU_DIM):
        t = jnp.dot(w_ref[:, c * MXU_DIM:(c + 1) * MXU_DIM].astype(BF16), perm_ref[...],
                    preferred_element_type=F32)
        glu_ref[:, c * LANES:(c + 1) * LANES] = t[:, :LANES].astype(BF16)
        lin_ref[:, c * LANES:(c + 1) * LANES] = t[:, LANES:].astype(BF16)


def _deinterleave_up(w_up):
    E, K, N2 = w_up.shape
    rt = 512
    i = np.arange(MXU_DIM)[:, None]
    j = np.arange(MXU_DIM)[None, :]
    perm = jnp.asarray(((i % 2) * LANES + i // 2) == j, dtype=BF16)
    return pl.pallas_call(
        _deinterleave_kernel,
        grid=(E, K // rt),
        in_specs=[pl.BlockSpec((None, rt, N2), lambda e, r: (e, r, 0)),
                  pl.BlockSpec((MXU_DIM, MXU_DIM), lambda e, r: (0, 0))],
        out_specs=[pl.BlockSpec((None, rt, N2 // 2), lambda e, r: (e, r, 0))] * 2,
        out_shape=[jax.ShapeDtypeStruct((E, K, N2 // 2), BF16)] * 2,
        compiler_params=_cparams("arbitrary", "arbitrary"),
        name="deinterleave_up",
    )(w_up, perm)


def _route_kernel(te_ref, start_ref, pos_ref, cnt_ref):
    rt = te_ref.shape[0]

    @pl.when(pl.program_id(0) == 0)
    def _():
        cnt_ref[...] = start_ref[...]

    te = te_ref[...]
    lane = lax.broadcasted_iota(I32, (rt, LANES), 1)
    onehots = [(lane == te[:, k:k + 1]).astype(F32) for k in range(TOP_K)]
    total = onehots[0] + onehots[1] + onehots[2] + onehots[3]
    r = lax.broadcasted_iota(I32, (rt, rt), 0)
    c = lax.broadcasted_iota(I32, (rt, rt), 1)
    earlier = (c < r).astype(BF16)
    base = cnt_ref[...] + jnp.dot(earlier, total.astype(BF16), preferred_element_type=F32)
    pos = jnp.zeros((rt, LANES), I32)
    for k in range(TOP_K):
        pk = jnp.sum(onehots[k] * base, axis=-1, keepdims=True)
        pos = jnp.where(lane == k, pk.astype(I32), pos)
    pos_ref[...] = pos
    cnt_ref[...] = cnt_ref[...] + jnp.sum(total, axis=0, keepdims=True)


def _route(te, start):
    N = te.shape[0]
    rt = _largest_divisor(N, (256, 128, 64, 32, 16, 8))
    return pl.pallas_call(
        _route_kernel,
        grid=(N // rt,),
        in_specs=[pl.BlockSpec((rt, LANES), lambda i: (i, 0)), pl.BlockSpec((1, LANES), lambda i: (0, 0))],
        out_specs=pl.BlockSpec((rt, LANES), lambda i: (i, 0)),
        out_shape=jax.ShapeDtypeStruct((N, LANES), I32),
        scratch_shapes=[pltpu.VMEM((1, LANES), F32)],
        compiler_params=_cparams("arbitrary"),
        name="moe_route",
    )(te, start)


def _row_copy_wait(src_rows, dst_rows, sem):
    pltpu.make_async_copy(src_rows, dst_rows, sem).wait()


def _dispatch_kernel(pos_ref, x_ref, xr_in_ref, xr_ref, sem):
    del xr_in_ref
    rt = x_ref.shape[0]

    def body(i, carry):
        for k in range(TOP_K):
            dst = pos_ref[i * TOP_K + k]
            pltpu.make_async_copy(x_ref.at[pl.ds(i, 1)], xr_ref.at[pl.ds(dst, 1)], sem).start(priority=k % 2)
        return carry

    lax.fori_loop(0, rt, body, 0)
    for k in range(TOP_K):
        _row_copy_wait(x_ref, xr_ref.at[pl.ds(0, rt)], sem)


def _dispatch(x, pos_flat, xr_init):
    N, D = x.shape
    rt = _largest_divisor(N, (256, 128, 64, 32, 16, 8))
    return pl.pallas_call(
        _dispatch_kernel,
        grid=(N // rt,),
        in_specs=[pl.BlockSpec((rt * TOP_K,), lambda i: (i,), memory_space=pltpu.SMEM),
                  pl.BlockSpec((rt, D), lambda i: (i, 0)),
                  pl.BlockSpec(memory_space=pl.ANY)],
        out_specs=pl.BlockSpec(memory_space=pl.ANY),
        out_shape=jax.ShapeDtypeStruct(xr_init.shape, xr_init.dtype),
        scratch_shapes=[pltpu.SemaphoreType.DMA(())],
        input_output_aliases={2: 0},
        compiler_params=_cparams("arbitrary"),
        name="moe_dispatch",
    )(pos_flat, x, xr_init)


def _moe_kernel(be_ref, nv_ref, x_ref, wg_ref, wl_ref, bg_ref, bl_ref, wd_ref, bd_ref, o_ref):
    i = pl.program_id(0)

    @pl.when(i < nv_ref[0])
    def _():
        packed = x_ref[...]
        x = jnp.concatenate([pltpu.bitcast(packed & jnp.uint32(0xFFFF0000), F32),
                             pltpu.bitcast(packed << 16, F32)], axis=1).astype(BF16)
        glu = jnp.dot(x, wg_ref[...], preferred_element_type=F32) + bg_ref[...]
        lin = jnp.dot(x, wl_ref[...], preferred_element_type=F32) + bl_ref[...]
        glu = jnp.minimum(glu, SWIGLU_LIMIT)
        lin = jnp.clip(lin, -SWIGLU_LIMIT, SWIGLU_LIMIT)
        act = glu * _sigmoid(SWIGLU_ALPHA * glu) * (lin + 1.0)
        o_ref[...] = jnp.dot(act.astype(BF16), wd_ref[...].astype(BF16), preferred_element_type=F32) + bd_ref[...]

    @pl.when(i >= nv_ref[0])
    def _():
        o_ref[...] = jnp.zeros_like(o_ref)


def _moe_ffn_blocks(xr, blk_exp, nvalid, wg, wl, bg, bl, wd, bd):
    m_pad = xr.shape[0]
    n_blk = m_pad // MOE_ROWS
    wspec = lambda cols: pl.BlockSpec((None, D_MODEL, cols), lambda i, be, nv: (be[i], 0, 0))
    bspec = lambda cols: pl.BlockSpec((None, 1, cols), lambda i, be, nv: (be[i], 0, 0))
    grid_spec = pltpu.PrefetchScalarGridSpec(
        num_scalar_prefetch=2,
        grid=(n_blk,),
        in_specs=[pl.BlockSpec((MOE_ROWS, D_MODEL // 2), lambda i, be, nv: (i, 0)),
                  wspec(D_FF), wspec(D_FF), bspec(D_FF), bspec(D_FF), wspec(D_MODEL), bspec(D_MODEL)],
        out_specs=pl.BlockSpec((MOE_ROWS, D_MODEL), lambda i, be, nv: (i, 0)),
    )
    return pl.pallas_call(
        _moe_kernel,
        grid_spec=grid_spec,
        out_shape=jax.ShapeDtypeStruct((m_pad, D_MODEL), F32),
        compiler_params=_cparams("arbitrary"),
        name="moe_ffn",
    )(blk_exp, nvalid, xr, wg, wl, bg, bl, wd, bd)


def _combine_kernel(pos_ref, pos_next_ref, x_ref, tg_ref, gf_ref, lng_ref, lnb_ref, yr_ref, xn_ref, ybuf, sems,
                    *, n_steps):
    sb, tt, d = x_ref.shape
    rows = sb * tt
    step = pl.program_id(0) * pl.num_programs(1) + pl.program_id(1)
    slot = step % 2

    def issue(p_ref, sl):
        def body(i, carry):
            for k in range(TOP_K):
                src = p_ref[i * TOP_K + k]
                pltpu.make_async_copy(yr_ref.at[pl.ds(src, 1)], ybuf.at[sl, k, pl.ds(i, 1)],
                                      sems.at[sl]).start(priority=k % 2)
            return carry

        lax.fori_loop(0, rows, body, 0)

    @pl.when(step == 0)
    def _():
        issue(pos_ref, 0)

    @pl.when(step + 1 < n_steps)
    def _():
        issue(pos_next_ref, 1 - slot)

    for k in range(TOP_K):
        _row_copy_wait(yr_ref.at[pl.ds(0, rows)], ybuf.at[slot, k], sems.at[slot])
    tg = tg_ref[...].reshape(rows, LANES)
    y = tg[:, 0:1] * ybuf[slot, 0]
    for k in range(1, TOP_K):
        y = y + tg[:, k:k + 1] * ybuf[slot, k]
    z = DN_ALPHA * x_ref[...] + (1.0 + gf_ref[...]) * y.reshape(sb, tt, d)
    xn_ref[...] = _layer_norm_rows(z, lng_ref[...], lnb_ref[...])


def _combine(x, yr, pos_flat, tg, mod, ln_g, ln_b):
    B, T, _ = x.shape
    sb, tt = _row_tiles(B, T)
    rows = sb * tt
    nj = T // tt
    n_steps = (B // sb) * nj
    vec = pl.BlockSpec((1, 1, D_MODEL), lambda i, j: (0, 0, 0))
    return pl.pallas_call(
        functools.partial(_combine_kernel, n_steps=n_steps),
        grid=(B // sb, nj),
        in_specs=[pl.BlockSpec((rows * TOP_K,), lambda i, j: (i * nj + j,), memory_space=pltpu.SMEM),
                  pl.BlockSpec((rows * TOP_K,), lambda i, j: (jnp.minimum(i * nj + j + 1, n_steps - 1),),
                               memory_space=pltpu.SMEM),
                  _tok_spec(sb, tt), _tok_spec(sb, tt, LANES), _mod_spec(sb, 5), vec, vec,
                  pl.BlockSpec(memory_space=pl.ANY)],
        out_specs=_tok_spec(sb, tt),
        out_shape=jax.ShapeDtypeStruct((B, T, D_MODEL), F32),
        scratch_shapes=[pltpu.VMEM((2, TOP_K, rows, D_MODEL), F32), pltpu.SemaphoreType.DMA((2,))],
        compiler_params=_cparams("arbitrary", "arbitrary"),
        name="moe_combine",
    )(pos_flat, pos_flat, x, tg, mod, ln_g.reshape(1, 1, D_MODEL), ln_b.reshape(1, 1, D_MODEL), yr)


def _moe_plan(te_all):
    N = te_all.shape[0]
    M = N * TOP_K
    onehot = (te_all[:, :TOP_K, None] == jnp.arange(N_EXPERTS, dtype=I32)[None, None, :])
    counts = jnp.sum(onehot.astype(I32), axis=(0, 1))
    padded = (counts + MOE_ROWS - 1) // MOE_ROWS * MOE_ROWS
    pend = jnp.cumsum(padded)
    pstart = pend - padded
    n_blk = -(-(M + N_EXPERTS * (MOE_ROWS - 1)) // MOE_ROWS)
    blk_first = jnp.arange(n_blk, dtype=I32)[:, None] * MOE_ROWS
    blk_exp = jnp.minimum(jnp.sum((pend[None, :] <= blk_first).astype(I32), axis=1), N_EXPERTS - 1).astype(I32)
    nvalid = (pend[-1] // MOE_ROWS).astype(I32).reshape(1)
    start = jnp.pad(pstart.astype(F32), (0, LANES - N_EXPERTS)).reshape(1, LANES)
    return start, blk_exp, nvalid, n_blk * MOE_ROWS


def _compress_inputs(cmp_raw):
    B, L, _ = cmp_raw.shape
    nch = L // CMP_STRIDE
    x = cmp_raw[:, :nch * CMP_STRIDE].reshape(B, nch, CMP_STRIDE, 2, B_KV_HEADS, B_DH)
    x = x.transpose(3, 0, 4, 1, 2, 5)
    return x.reshape(2, B * B_KV_HEADS, nch, CMP_STRIDE * B_DH), nch


def _pad_rows(a, mult, axis=1):
    n = a.shape[axis]
    pad = (-n) % mult
    if pad == 0:
        return a
    widths = [(0, 0)] * a.ndim
    widths[axis] = (0, pad)
    return jnp.pad(a, widths)


def kernel(x_prompt, x_sample, state_hgrn, cache_cmp_kv, cache_slc_kv, state_win_kv, page_table, c_prompt, c_sample,
           ada_w, ada_b, ln_g, ln_b, a_w_in, a_lb, a_onorm_g, a_w_o, kv_ada_w, kv_ada_b, kv_w, cmp_pos, cmp_w1, cmp_b1,
           cmp_w2, cmp_b2, b_w_in, b_w_o, moe_wr, moe_br, moe_w_up, moe_b_up, moe_w_down, moe_b_down):
    Bp, Tp, _ = x_prompt.shape
    Bs, Ts, _ = x_sample.shape
    n_pages = page_table.shape[1]
    past_len = n_pages * PAGE_SIZE
    keep_s = state_win_kv.shape[1]
    G, DH = B_KV_HEADS, B_DH
    GD = G * DH
    Np, Ns = Bp * Tp, Bs * Ts
    assert keep_s % PAGE_SIZE == 0

    a_w_in_bf = a_w_in.astype(BF16)
    a_w_o_bf = a_w_o.astype(BF16)
    b_w_o_bf = b_w_o.astype(BF16)
    dup_cols = [jnp.concatenate([kv_w[:, c * GD + g * DH:c * GD + (g + 1) * DH]] * 2, axis=1)
                for c in range(2, 6) for g in range(G)]
    kv_w_bf = jnp.concatenate([kv_w] + dup_cols, axis=1).astype(BF16)
    wq_bf = b_w_in[:, :, :D_MODEL].astype(BF16)
    n_l = b_w_in.shape[0]
    wg = b_w_in[:, :, D_MODEL:].reshape(n_l, D_MODEL, B_HEADS, 3).transpose(0, 1, 3, 2).reshape(n_l, D_MODEL, 3 * B_HEADS)
    wg_bf = jnp.pad(wg, ((0, 0), (0, 0), (0, LANES - 3 * B_HEADS))).astype(BF16)
    wr_pad = jnp.pad(moe_wr, ((0, 0), (0, 0), (0, LANES - N_EXPERTS)))
    br_pad = jnp.pad(moe_br, ((0, 0), (0, LANES - N_EXPERTS)), constant_values=NEG_INF).reshape(DEPTH, 1, LANES)
    w_glu, w_lin = _deinterleave_up(moe_w_up.reshape(DEPTH * N_EXPERTS, D_MODEL, 2 * D_FF))
    b_glu = moe_b_up[..., 0::2].reshape(DEPTH * N_EXPERTS, 1, D_FF)
    b_lin = moe_b_up[..., 1::2].reshape(DEPTH * N_EXPERTS, 1, D_FF)
    w_dn = moe_w_down.reshape(DEPTH * N_EXPERTS, D_FF, D_MODEL)
    b_dn = moe_b_down.reshape(DEPTH * N_EXPERTS, 1, D_MODEL)
    lb_p = jax.nn.softmax(a_lb.astype(F32), axis=0)
    lb_all = jnp.maximum(jnp.cumsum(lb_p, axis=0) - lb_p[0], 0.0)
    w1cat = jnp.concatenate([cmp_w1[:, :CMP_STRIDE * DH], cmp_w1[:, CMP_STRIDE * DH:]], axis=-1)
    cvec = _cmp_const(cmp_pos.reshape(2, 1, CMP_BLOCK * DH), cmp_w1, cmp_b1.reshape(2, 1, CMP_HID))
    w2dup = jnp.concatenate([cmp_w2, cmp_w2], axis=-1)
    b2dup = jnp.concatenate([cmp_b2, cmp_b2], axis=-1).reshape(2, 1, LANES)

    n_c = Bp + Bs
    c_all = _pad_rows(jnp.concatenate([c_prompt, c_sample], axis=0), SUBLANES, axis=0)
    mod_all = _silu_linear(c_all, ada_w, ada_b)
    kvmod_all = _silu_linear(c_all, kv_ada_w[None], kv_ada_b[None])[0]
    mods = [(mod_all[l, :Bp].reshape(Bp, 1, 6 * D_MODEL), mod_all[l, Bp:n_c].reshape(Bs, 1, 6 * D_MODEL))
            for l in range(DEPTH)]
    kvmods = (kvmod_all[:Bp].reshape(Bp, 1, 2 * D_MODEL), kvmod_all[Bp:n_c].reshape(Bs, 1, 2 * D_MODEL))

    xs = [x_prompt, x_sample]
    s0s = [jnp.zeros((1, Bp, A_HEADS, A_DK, A_DV), F32), state_hgrn]
    hgrn_out = [[], []]
    ctx = [None, None]
    new_kv = [None, None]
    xr = None

    def keys_t(dup, col):
        b, length, _ = dup.shape
        return dup[:, :, col * G * LANES:(col + 1) * G * LANES].reshape(b, length, G, LANES).transpose(0, 2, 3, 1)

    def build_ctx(t, x):
        B, T, _ = x.shape
        if t == 0:
            kv, dup = _kv_proj(x, kvmods[t], kv_w_bf, BF16)
            cmp_raw = kv[:, :, :2 * GD]
            if T % KEY_TILE:
                dup = _pad_rows(dup, KEY_TILE)
            attn_kv = (keys_t(dup, 0), dup, 1, keys_t(dup, 2), dup, 3)
            L, pos0, win_pos0 = T, 0, 0
            win_out = kv[:, T - min(WINDOW, T):, 4 * GD:].reshape(B, -1, 2, G, DH)
            xc, nch = _compress_inputs(cmp_raw)
            ckv = _compress(xc, w1cat, cvec, w2dup, b2dup).reshape(2, B, G, nch, LANES)
        else:
            kv, dup = _kv_proj(x, kvmods[t], kv_w_bf, F32)
            L = past_len + T
            nch = L // CMP_STRIDE
            assert nch * CMP_STRIDE <= past_len and T <= keep_s
            cpp = PAGE_SIZE // CMP_STRIDE
            pool_t = cache_cmp_kv.reshape(-1, cpp, CMP_STRIDE, 2, G, DH).transpose(3, 0, 1, 4, 2, 5)
            pool_t = pool_t.reshape(2, -1, cpp, G * CMP_STRIDE * DH)
            ckv = _compress_paged(pool_t, page_table, w1cat, cvec, w2dup, b2dup)[:, :, :, :nch]
            ls_pad = -(-L // KEY_TILE) * KEY_TILE
            lw_pad = -(-(keep_s + T) // KEY_TILE) * KEY_TILE
            kt_slc, v_slc = _pages_to_kv(cache_slc_kv.transpose(0, 2, 3, 4, 1).astype(BF16), page_table, dup, 0, 1, ls_pad)
            n_wp = keep_s // PAGE_SIZE
            win_table = jnp.arange(B * n_wp, dtype=I32).reshape(B, n_wp)
            win_pool = state_win_kv.reshape(B * n_wp, PAGE_SIZE, 2, G, DH).transpose(0, 2, 3, 4, 1).astype(BF16)
            kt_win, v_win = _pages_to_kv(win_pool, win_table, dup, 2, 3, lw_pad)
            attn_kv = (kt_slc, v_slc, 0, kt_win, v_win, 0)
            pos0 = past_len
            win_pos0 = past_len - keep_s
            win_out = jnp.concatenate([state_win_kv[:, T:], kv[:, :, 4 * GD:].reshape(B, T, 2, G, DH)], axis=1)
        nc = nch - CMP_RATIO + 1
        ns = -(-L // SLC_BLOCK)
        ckv = _pad_rows(ckv, LANES, axis=3)
        info = dict(ck=ckv[0], cv=ckv[1], attn_kv=attn_kv, pos0=pos0, win_pos0=win_pos0, nc=nc, ns=ns)
        outs = (kv[:, :, :2 * GD].reshape(B, T, 2, G, DH), kv[:, :, 2 * GD:4 * GD].reshape(B, T, 2, G, DH),
                win_out)
        return info, outs

    for l in range(DEPTH):
        o_mix = []
        for t in range(2):
            x = xs[t]
            mod = mods[l][t]
            if l < N_A_LAYERS:
                q, k, g, v, og = _hgrn_in(x, mod, a_w_in_bf[l], lb_all[l])
                o, S = _hgrn_scan(q, k, g, v, og, s0s[t], l if t else 0, a_onorm_g[l])
                hgrn_out[t].append(S)
            else:
                j = l - N_A_LAYERS
                if ctx[t] is None:
                    ctx[t], new_kv[t] = build_ctx(t, x)
                c = ctx[t]
                q, gates = _nsa_in(x, mod, wq_bf[j], wg_bf[j])
                o = _nsa_attn(q, gates, c['ck'], c['cv'], *c['attn_kv'],
                              pos0=c['pos0'], win_pos0=c['win_pos0'], nc=c['nc'], ns=c['ns'])
            o_mix.append(o)
        w_o = a_w_o_bf[l] if l < N_A_LAYERS else b_w_o_bf[l - N_A_LAYERS]
        hf, te, tg = [], [], []
        for t in range(2):
            xn, h, e, gt = _mixer_out(o_mix[t], w_o, xs[t], mods[l][t], ln_g[l, 0], ln_b[l, 0], wr_pad[l], br_pad[l])
            xs[t] = xn
            hf.append(h.reshape(-1, D_MODEL // 2))
            te.append(e.reshape(-1, LANES))
            tg.append(gt)
        te_all = jnp.concatenate(te, axis=0)
        start, blk_exp, nvalid, m_pad = _moe_plan(te_all)
        pos_flat = _route(te_all, start)[:, :TOP_K].reshape(-1)
        if xr is None:
            xr = jnp.zeros((m_pad, D_MODEL // 2), jnp.uint32)
        xr = _dispatch(jnp.concatenate(hf, axis=0), pos_flat, xr)
        yr = _moe_ffn_blocks(xr, blk_exp + l * N_EXPERTS, nvalid, w_glu, w_lin, b_glu, b_lin, w_dn, b_dn)
        pos_t = [pos_flat[:Np * TOP_K], pos_flat[Np * TOP_K:]]
        for t in range(2):
            xs[t] = _combine(xs[t], yr, pos_t[t], tg[t], mods[l][t], ln_g[l, 1], ln_b[l, 1])

    return (xs[0], xs[1], jnp.stack(hgrn_out[0], axis=0), jnp.stack(hgrn_out[1], axis=0),
            new_kv[0][0], new_kv[1][0], new_kv[0][1], new_kv[1][1], new_kv[0][2], new_kv[1][2])
```
